```python
import jax, jax.numpy as jnp
from jax import lax
import numpy as np

D_MODEL = 1024
BATCH = 8
SEQ = 2048
DEPTH = 2
DEC_BATCH = 128
DEC_SEQ = 8
PAST_LEN = 16384
PAGE_SIZE = 128

CHUNK = 128
NH_A = 4
HD_A = 128
D_A = NH_A * HD_A
NH_B = 8
HD_B = 64
D_B = NH_B * HD_B
CONV_W = 3
D_IN_EVEN = 2 * D_A + 3 * D_B
POOL_WINDOWS = (2, 4, 8, 16)
N_POOL_GROUPS = len(POOL_WINDOWS)
POOL_GC = D_MODEL // N_POOL_GROUPS
POOL_CTX = max(POOL_WINDOWS) - 1
D_FF = 128 * ((8 * D_MODEL // 3 + 127) // 128)
N_EVEN = (DEPTH + 1) // 2
N_ODD = DEPTH // 2
EPS = 1e-6

kernel_name = 'hybrid_sgu_shortconv_pool_macaron_step'


def _rmsnorm(x, g):
    xf = x.astype(jnp.float32)
    y = xf * lax.rsqrt(jnp.mean(xf * xf, axis=-1, keepdims=True) + EPS) * g.astype(jnp.float32)
    return y.astype(x.dtype)


def _layernorm(x, g, b):
    xf = x.astype(jnp.float32)
    mu = jnp.mean(xf, axis=-1, keepdims=True)
    var = jnp.mean(jnp.square(xf - mu), axis=-1, keepdims=True)
    y = (xf - mu) * lax.rsqrt(var + EPS) * g.astype(jnp.float32) + b.astype(jnp.float32)
    return y.astype(x.dtype)


def _swiglu(h, w_gate, w_up, w_down):
    return (jax.nn.silu(h @ w_gate) * (h @ w_up)) @ w_down


def _sgu(u, v, w_s, b_s, chunk_len):
    b, L, _ = v.shape
    n_c = L // chunk_len
    mask = jnp.tril(jnp.ones((chunk_len, chunk_len), dtype=bool))
    w = jnp.where(mask[None], w_s[:, :chunk_len, :chunk_len], 0.0).astype(v.dtype)
    vc = v.reshape(b, n_c, chunk_len, NH_A, HD_A)
    mixed = jnp.einsum('hts,bcshd->bcthd', w, vc) + b_s[:, :chunk_len].T[None, None, :, :, None]
    return u * mixed.reshape(b, L, D_A)


def _even_mixer(h, conv_buf, chunk_len, w_in, ln_g, ln_b, sgu_w, sgu_b, conv_w, w_out):
    L = h.shape[1]
    z = h @ w_in
    uv = jax.nn.gelu(z[..., :2 * D_A], approximate=False)
    u = uv[..., :D_A]
    v = _layernorm(uv[..., D_A:], ln_g, ln_b)
    y_a = _sgu(u, v, sgu_w, sgu_b, chunk_len)
    o = 2 * D_A
    gate_b = z[..., o:o + D_B]
    gate_c = z[..., o + D_B:o + 2 * D_B]
    x_in = z[..., o + 2 * D_B:o + 3 * D_B]
    xg = gate_c * x_in
    ext = jnp.concatenate([conv_buf.astype(xg.dtype), xg], axis=1)
    conv = ext[:, 0:L] * conv_w[0]
    for k in range(1, CONV_W):
        conv = conv + ext[:, k:k + L] * conv_w[k]
    y_b = gate_b * conv
    out = jnp.concatenate([y_a, y_b], axis=-1) @ w_out
    return out, ext[:, -(CONV_W - 1):], v


def _odd_mixer(h, pool_buf, pos, w_in, pool_w, pool_scale, w_out):
    b, T, _ = h.shape
    p = h @ w_in
    ext = jnp.concatenate([pool_buf.astype(p.dtype), p], axis=1)
    c = jnp.cumsum(ext.astype(jnp.float32), axis=1)
    c = jnp.concatenate([jnp.zeros_like(c[:, :1]), c], axis=1)
    end = c[:, POOL_CTX + 1:]
    pf = p.astype(jnp.float32)
    outs = []
    for g, w in enumerate(POOL_WINDOWS):
        sl = slice(g * POOL_GC, (g + 1) * POOL_GC)
        start = c[:, POOL_CTX + 1 - w:POOL_CTX + 1 - w + T, sl]
        cnt = jnp.minimum(pos + 1, w).astype(jnp.float32)[None, :, None]
        outs.append((end[..., sl] - start) / cnt - pf[..., sl])
    d = jnp.stack(outs, axis=2).astype(p.dtype)
    y = jnp.einsum('btgc,gcd->btgd', d, pool_w).reshape(b, T, D_MODEL) * pool_scale
    return y @ w_out, ext[:, -POOL_CTX:]


def _trunk(x, conv_bufs, pool_bufs, pos, chunk_len, keep_v,
           norm_g, final_norm_g, ffn_w_gate, ffn_w_up, ffn_w_down,
           e_w_in, e_ln_g, e_ln_b, e_sgu_w, e_sgu_b, e_conv_w, e_w_out,
           o_w_in, o_pool_w, o_pool_scale, o_w_out):
    new_conv, new_pool, v_rows = [], [], []
    for layer in range(DEPTH):
        i = layer // 2
        x = x + 0.5 * _swiglu(_rmsnorm(x, norm_g[layer, 0]),
                              ffn_w_gate[layer, 0], ffn_w_up[layer, 0], ffn_w_down[layer, 0])
        h = _rmsnorm(x, norm_g[layer, 1])
        if layer % 2 == 0:
            m, cb, v = _even_mixer(h, conv_bufs[i], chunk_len, e_w_in[i], e_ln_g[i], e_ln_b[i],
                                   e_sgu_w[i], e_sgu_b[i], e_conv_w[i], e_w_out[i])
            new_conv.append(cb)
            v_rows.append(v)
        else:
            m, pb = _odd_mixer(h, pool_bufs[i], pos, o_w_in[i], o_pool_w[i], o_pool_scale[i], o_w_out[i])
            new_pool.append(pb)
        x = x + m
        x = x + 0.5 * _swiglu(_rmsnorm(x, norm_g[layer, 2]),
                              ffn_w_gate[layer, 1], ffn_w_up[layer, 1], ffn_w_down[layer, 1])
    y = _rmsnorm(x, final_norm_g)
    v_out = jnp.stack(v_rows) if keep_v else None
    return y, jnp.stack(new_conv), v_out, jnp.stack(new_pool)


def setup_inputs(seed: int = 0) -> dict:
    key = jax.random.key(seed)
    ks = jax.random.split(key, 24)
    f32 = jnp.float32
    nrm = lambda k, shape, s: (jax.random.normal(k, shape, f32) * s).astype(f32)
    return {
        'x_prompt': nrm(ks[0], (BATCH, SEQ, D_MODEL), 1.0),
        'x_sample': nrm(ks[1], (DEC_BATCH, DEC_SEQ, D_MODEL), 1.0),
        'state_conv': nrm(ks[2], (N_EVEN, DEC_BATCH, CONV_W - 1, D_B), 0.5),
        'state_pool': nrm(ks[3], (N_ODD, DEC_BATCH, POOL_CTX, D_MODEL), 0.5),
        'norm_g': 1.0 + nrm(ks[4], (DEPTH, 3, D_MODEL), 0.02),
        'final_norm_g': 1.0 + nrm(ks[5], (D_MODEL,), 0.02),
        'ffn_w_gate': nrm(ks[6], (DEPTH, 2, D_MODEL, D_FF), D_MODEL ** -0.5),
        'ffn_w_up': nrm(ks[7], (DEPTH, 2, D_MODEL, D_FF), D_MODEL ** -0.5),
        'ffn_w_down': nrm(ks[8], (DEPTH, 2, D_FF, D_MODEL), D_FF ** -0.5),
        'e_w_in': nrm(ks[9], (N_EVEN, D_MODEL, D_IN_EVEN), D_MODEL ** -0.5),
        'e_ln_g': 1.0 + nrm(ks[10], (N_EVEN, D_A), 0.02),
        'e_ln_b': nrm(ks[11], (N_EVEN, D_A), 0.02),
        'e_sgu_w': nrm(ks[12], (N_EVEN, NH_A, CHUNK, CHUNK), CHUNK ** -0.5),
        'e_sgu_b': 1.0 + nrm(ks[13], (N_EVEN, NH_A, CHUNK), 0.02),
        'e_conv_w': nrm(ks[14], (N_EVEN, CONV_W, D_B), CONV_W ** -0.5),
        'e_w_out': nrm(ks[15], (N_EVEN, D_A + D_B, D_MODEL), (D_A + D_B) ** -0.5),
        'o_w_in': nrm(ks[16], (N_ODD, D_MODEL, D_MODEL), D_MODEL ** -0.5),
        'o_pool_w': nrm(ks[17], (N_ODD, N_POOL_GROUPS, POOL_GC, POOL_GC), POOL_GC ** -0.5),
        'o_pool_scale': 1.0 + nrm(ks[18], (N_ODD, D_MODEL), 0.02),
        'o_w_out': nrm(ks[19], (N_ODD, D_MODEL, D_MODEL), D_MODEL ** -0.5),
    }


def reference(x_prompt, x_sample, state_conv, state_pool, norm_g, final_norm_g,
              ffn_w_gate, ffn_w_up, ffn_w_down, e_w_in, e_ln_g, e_ln_b, e_sgu_w, e_sgu_b,
              e_conv_w, e_w_out, o_w_in, o_pool_w, o_pool_scale, o_w_out):
    weights = (norm_g, final_norm_g, ffn_w_gate, ffn_w_up, ffn_w_down,
               e_w_in, e_ln_g, e_ln_b, e_sgu_w, e_sgu_b, e_conv_w, e_w_out,
               o_w_in, o_pool_w, o_pool_scale, o_w_out)
    pos_p = jnp.arange(SEQ, dtype=jnp.int32)
    conv0 = jnp.zeros((N_EVEN, BATCH, CONV_W - 1, D_B), x_prompt.dtype)
    pool0 = jnp.zeros((N_ODD, BATCH, POOL_CTX, D_MODEL), x_prompt.dtype)
    y_prompt, conv_prompt, _, pool_prompt = _trunk(
        x_prompt, conv0, pool0, pos_p, CHUNK, False, *weights)
    pos_s = PAST_LEN + jnp.arange(DEC_SEQ, dtype=jnp.int32)
    y_sample, conv_sample, chunk_v_sample, pool_sample = _trunk(
        x_sample, state_conv, state_pool, pos_s, DEC_SEQ, True, *weights)
    return (y_prompt, y_sample, conv_prompt, conv_sample, chunk_v_sample, pool_prompt, pool_sample)
```

```python
import functools

import jax
import jax.numpy as jnp
from jax import lax
from jax.experimental import pallas as pl
from jax.experimental.pallas import tpu as pltpu

_CHUNK = 128
_NH_A = 4
_HD_A = 128
_CONV_W = 3
_POOL_WINDOWS = (2, 4, 8, 16)
_POOL_CTX = max(_POOL_WINDOWS) - 1
_PAST_LEN = 16384
_EPS = 1e-6

_SUBLANES = 8
_VMEM_LIMIT = 56 * 1024 * 1024

_TM = 512
_FFN_CHUNKS = ((0, 1536), (1536, 1280))

_BF = jnp.bfloat16
_F32 = jnp.float32


def _rms(x, g):
    return x * lax.rsqrt(jnp.mean(x * x, axis=-1, keepdims=True) + _EPS) * g


def _dot(a, b):
    return jnp.dot(a, b, preferred_element_type=_F32)


def _resident(shape):
    nd = len(shape)
    return pl.BlockSpec(shape, lambda i: (0,) * nd, pipeline_mode=pl.Buffered(1))


def _params():
    return pltpu.CompilerParams(dimension_semantics=("arbitrary",), vmem_limit_bytes=_VMEM_LIMIT)


def _ffn_body(*refs, split_in, final, n_prompt_tiles):
    refs = list(refs)
    i = pl.program_id(0)
    if split_in:
        xp_ref, xs_ref = refs[:2]
        refs = refs[2:]
        x = jnp.where(i < n_prompt_tiles, xp_ref[...], xs_ref[...])
    else:
        x = refs.pop(0)[...]
    g_ref, wg_ref, wu_ref, wd_ref = refs[:4]
    refs = refs[4:]
    h = _rms(x, g_ref[...]).astype(_BF)
    acc = None
    for c0, cn in _FFN_CHUNKS:
        a = _dot(h, wg_ref[:, c0:c0 + cn])
        b = _dot(h, wu_ref[:, c0:c0 + cn])
        act = (a * jax.nn.sigmoid(a) * b).astype(_BF)
        d = _dot(act, wd_ref[c0:c0 + cn, :])
        acc = d if acc is None else acc + d
    y = x + 0.5 * acc
    if final:
        fg_ref, yp_ref, ys_ref = refs
        y = _rms(y, fg_ref[...])

        @pl.when(i < n_prompt_tiles)
        def _():
            yp_ref[...] = y

        @pl.when(i >= n_prompt_tiles)
        def _():
            ys_ref[...] = y
    else:
        (o_ref,) = refs
        o_ref[...] = y


def _ffn(xs, g, wg, wu, wd, *, n_prompt, n_sample, final_g=None):
    d = wg.shape[0]
    dff = wg.shape[1]
    npt, nst = n_prompt // _TM, n_sample // _TM
    split_in = len(xs) == 2
    final = final_g is not None
    row = lambda i: (i, 0)
    prow = lambda i: (jnp.minimum(i, npt - 1), 0)
    srow = lambda i: (jnp.maximum(i - npt, 0), 0)
    tile = (_TM, d)
    in_specs = [pl.BlockSpec(tile, prow), pl.BlockSpec(tile, srow)] if split_in else [pl.BlockSpec(tile, row)]
    in_specs += [_resident((1, d)), _resident((d, dff)), _resident((d, dff)), _resident((dff, d))]
    args = list(xs) + [g.reshape(1, d), wg, wu, wd]
    if final:
        in_specs.append(_resident((1, d)))
        args.append(final_g.reshape(1, d))
        out_specs = [pl.BlockSpec(tile, prow), pl.BlockSpec(tile, srow)]
        out_shape = [jax.ShapeDtypeStruct((n_prompt, d), _F32), jax.ShapeDtypeStruct((n_sample, d), _F32)]
    else:
        out_specs = pl.BlockSpec(tile, row)
        out_shape = jax.ShapeDtypeStruct((n_prompt + n_sample, d), _F32)
    return pl.pallas_call(
        functools.partial(_ffn_body, split_in=split_in, final=final, n_prompt_tiles=npt),
        grid=(npt + nst,),
        in_specs=in_specs,
        out_specs=out_specs,
        out_shape=out_shape,
        compiler_params=_params(),
        name="ffn_final" if final else ("ffn_first" if split_in else "ffn"),
    )(*args)


def _gelu(x):
    return 0.5 * x * (1.0 + lax.erf(x * (2.0 ** -0.5)))


def _even_body(*refs, sample, chunk_len, tiles_per_seq):
    if sample:
        (x_ref, g_ref, win_ref, lng_ref, lnb_ref, sw_ref, sb_ref, cw_ref, wout_ref, b1_ref, b2_ref,
         o_ref, xg_ref, v_ref, yab_ref) = refs
    else:
        (x_ref, g_ref, win_ref, lng_ref, lnb_ref, sw_ref, sb_ref, cw_ref, wout_ref,
         o_ref, cs_ref, yab_ref, carry_ref) = refs
    i = pl.program_id(0)
    tm = x_ref.shape[0]
    d_a = _NH_A * _HD_A
    d_b = cw_ref.shape[1]
    x = x_ref[...]
    h = _rms(x, g_ref[...]).astype(_BF)
    z = _dot(h, win_ref[...])
    u = _gelu(z[:, 0:d_a])
    vv = _gelu(z[:, d_a:2 * d_a])
    mu = jnp.mean(vv, axis=-1, keepdims=True)
    vc = vv - mu
    var = jnp.mean(vc * vc, axis=-1, keepdims=True)
    v = vc * lax.rsqrt(var + _EPS) * lng_ref[...] + lnb_ref[...]
    if sample:
        v_ref[...] = v
    o = 2 * d_a
    gate_b = z[:, o:o + d_b]
    gate_c = z[:, o + d_b:o + 2 * d_b]
    x_in = z[:, o + 2 * d_b:o + 3 * d_b]

    tt = lax.broadcasted_iota(jnp.int32, (_CHUNK, _CHUNK), 0)
    ss = lax.broadcasted_iota(jnp.int32, (_CHUNK, _CHUNK), 1)
    keep = (ss <= tt) & ((tt // chunk_len) == (ss // chunk_len))
    vb = v.astype(_BF)
    for hd in range(_NH_A):
        w = jnp.where(keep, sw_ref[hd], 0.0).astype(_BF)
        bias = sb_ref[hd]
        cols = slice(hd * _HD_A, (hd + 1) * _HD_A)
        for c in range(tm // _CHUNK):
            rows = slice(c * _CHUNK, (c + 1) * _CHUNK)
            mixed = _dot(w, vb[rows, cols]) + bias
            yab_ref[rows, cols] = (u[rows, cols] * mixed).astype(_BF)

    xg = gate_c * x_in
    cw = cw_ref[...]
    if sample:
        t8 = lax.broadcasted_iota(jnp.int32, (tm, 1), 0) % chunk_len
        back1 = jnp.where(t8 >= 1, pltpu.roll(xg, 1, axis=0), b1_ref[...])
        back2 = jnp.where(t8 >= 2, pltpu.roll(xg, 2, axis=0), b2_ref[...])
        xg_ref[...] = xg
    else:
        @pl.when((i % tiles_per_seq) == 0)
        def _():
            carry_ref[...] = jnp.zeros_like(carry_ref)

        ext = jnp.concatenate([carry_ref[...], xg], axis=0)
        back1 = pltpu.roll(ext, 1, axis=0)[_SUBLANES:]
        back2 = pltpu.roll(ext, 2, axis=0)[_SUBLANES:]
        carry_ref[...] = xg[tm - _SUBLANES:]
        cs_ref[0] = xg[tm - _SUBLANES:]
    conv = back2 * cw[0:1] + back1 * cw[1:2] + xg * cw[2:3]
    yab_ref[:, d_a:d_a + d_b] = (gate_b * conv).astype(_BF)

    o_ref[...] = x + _dot(yab_ref[...], wout_ref[...])


def _even_mixer(xc, g, w_in, ln_g, ln_b, sgu_w, sgu_b, conv_w, w_out, *, row0, n_rows, seq_len,
                chunk_len, conv_state=None):
    sample = conv_state is not None
    t, d = xc.shape
    d_in = w_in.shape[1]
    d_a = _NH_A * _HD_A
    d_b = conv_w.shape[1]
    tile0 = row0 // _TM
    n_tiles = n_rows // _TM
    row = lambda i: (tile0 + i, 0)
    reps = _CHUNK // chunk_len
    sw = jnp.tile(sgu_w[:, :chunk_len, :chunk_len], (1, reps, reps))
    sb = jnp.broadcast_to(jnp.tile(sgu_b[:, :chunk_len], (1, reps))[:, :, None], (_NH_A, _CHUNK, _HD_A))
    in_specs = [pl.BlockSpec((_TM, d), row), _resident((1, d)), _resident((d, d_in)), _resident((1, d_a)),
                _resident((1, d_a)), _resident(sw.shape), _resident(sb.shape), _resident(conv_w.shape),
                _resident(w_out.shape)]
    args = [xc, g.reshape(1, d), w_in, ln_g.reshape(1, d_a), ln_b.reshape(1, d_a), sw, sb, conv_w, w_out]
    scratch = [pltpu.VMEM((_TM, d_a + d_b), _BF)]
    if sample:
        n_seq = n_rows // chunk_len
        pad = lambda a: jnp.pad(a, ((0, 0), (0, chunk_len - a.shape[1]), (0, 0))).reshape(n_rows, d_b)
        b1 = pad(conv_state[:, 1:2])
        b2 = pad(conv_state)
        local = lambda i: (i, 0)
        in_specs += [pl.BlockSpec((_TM, d_b), local), pl.BlockSpec((_TM, d_b), local)]
        args += [b1, b2]
        out_specs = [pl.BlockSpec((_TM, d), row), pl.BlockSpec((_TM, d_b), local), pl.BlockSpec((_TM, d_a), local)]
        out_shape = [jax.ShapeDtypeStruct((t, d), _F32), jax.ShapeDtypeStruct((n_rows, d_b), _F32),
                     jax.ShapeDtypeStruct((n_rows, d_a), _F32)]
        tiles_per_seq = 1
    else:
        n_seq = n_rows // seq_len
        tiles_per_seq = seq_len // _TM
        out_specs = [pl.BlockSpec((_TM, d), row),
                     pl.BlockSpec((1, _SUBLANES, d_b), lambda i: (i // tiles_per_seq, 0, 0))]
        out_shape = [jax.ShapeDtypeStruct((t, d), _F32), jax.ShapeDtypeStruct((n_seq, _SUBLANES, d_b), _F32)]
        scratch.append(pltpu.VMEM((_SUBLANES, d_b), _F32))
    return pl.pallas_call(
        functools.partial(_even_body, sample=sample, chunk_len=chunk_len, tiles_per_seq=tiles_per_seq),
        grid=(n_tiles,),
        in_specs=in_specs,
        out_specs=out_specs,
        out_shape=out_shape,
        scratch_shapes=scratch,
        input_output_aliases={0: 0},
        compiler_params=_params(),
        name="even_sample" if sample else "even_prompt",
    )(*args)


def _window_sums(ext):
    gc = ext.shape[1] // len(_POOL_WINDOWS)
    sums = []
    s = ext
    width = 1
    for w in _POOL_WINDOWS:
        while width < w:
            s = s + pltpu.roll(s, width, axis=0)
            width *= 2
        sums.append(s[:, 0:gc])
        s = s[:, gc:]
    return sums


def _odd_body(*refs, sample, seq_len, tiles_per_seq):
    if sample:
        (x_ref, g_ref, win_ref, pw_ref, psc_ref, wout_ref, ctx_ref, o_ref, p_ref, y_ref, ext_ref) = refs
    else:
        (x_ref, g_ref, win_ref, pw_ref, psc_ref, wout_ref, o_ref, ps_ref, y_ref, carry_ref) = refs
    i = pl.program_id(0)
    tm, d = x_ref.shape
    ctx_rows = 2 * _SUBLANES
    x = x_ref[...]
    h = _rms(x, g_ref[...]).astype(_BF)
    p = _dot(h, win_ref[...])
    t_in_tile = lax.broadcasted_iota(jnp.int32, (tm, 1), 0)
    if sample:
        nb = tm // seq_len
        ext_ref[:, 0:ctx_rows, :] = ctx_ref[...]
        ext_ref[:, ctx_rows:ctx_rows + seq_len, :] = p.reshape(nb, seq_len, d)
        ext = ext_ref[...].reshape(nb * (ctx_rows + seq_len), d)
        pick = lambda s: s.reshape(nb, ctx_rows + seq_len, s.shape[1])[:, ctx_rows:, :].reshape(tm, s.shape[1])
        pos = _PAST_LEN + t_in_tile % seq_len
        p_ref[...] = p
    else:
        @pl.when((i % tiles_per_seq) == 0)
        def _():
            carry_ref[...] = jnp.zeros_like(carry_ref)

        ext = jnp.concatenate([carry_ref[...], p], axis=0)
        pick = lambda s: s[ctx_rows:]
        pos = (i % tiles_per_seq) * tm + t_in_tile
        carry_ref[...] = p[tm - ctx_rows:]
        ps_ref[0] = p[tm - ctx_rows:]
    gc = d // len(_POOL_WINDOWS)
    for grp, (w, s) in enumerate(zip(_POOL_WINDOWS, _window_sums(ext))):
        cnt = jnp.minimum(pos + 1, w).astype(_F32)
        cols = slice(grp * gc, (grp + 1) * gc)
        dlt = pick(s) / cnt - p[:, cols]
        yg = _dot(dlt.astype(_BF), pw_ref[grp])
        y_ref[:, cols] = (yg * psc_ref[:, cols]).astype(_BF)
    o_ref[...] = x + _dot(y_ref[...], wout_ref[...])


def _odd_mixer(xc, g, w_in, pool_w, pool_scale, w_out, *, row0, n_rows, seq_len, pool_state=None):
    sample = pool_state is not None
    t, d = xc.shape
    ctx_rows = 2 * _SUBLANES
    tm = _TM // 2 if sample else _TM
    tile0 = row0 // tm
    n_tiles = n_rows // tm
    row = lambda i: (tile0 + i, 0)
    in_specs = [pl.BlockSpec((tm, d), row), _resident((1, d)), _resident((d, d)), _resident(pool_w.shape),
                _resident((1, d)), _resident((d, d))]
    args = [xc, g.reshape(1, d), w_in, pool_w, pool_scale.reshape(1, d), w_out]
    scratch = [pltpu.VMEM((tm, d), _BF)]
    if sample:
        nb = tm // seq_len
        ctx = jnp.pad(pool_state, ((0, 0), (ctx_rows - _POOL_CTX, 0), (0, 0)))
        in_specs.append(pl.BlockSpec((nb, ctx_rows, d), lambda i: (i, 0, 0)))
        args.append(ctx)
        out_specs = [pl.BlockSpec((tm, d), row), pl.BlockSpec((tm, d), lambda i: (i, 0))]
        out_shape = [jax.ShapeDtypeStruct((t, d), _F32), jax.ShapeDtypeStruct((n_rows, d), _F32)]
        scratch.append(pltpu.VMEM((nb, ctx_rows + seq_len, d), _F32))
        tiles_per_seq = 1
    else:
        n_seq = n_rows // seq_len
        tiles_per_seq = seq_len // tm
        out_specs = [pl.BlockSpec((tm, d), row),
                     pl.BlockSpec((1, ctx_rows, d), lambda i: (i // tiles_per_seq, 0, 0))]
        out_shape = [jax.ShapeDtypeStruct((t, d), _F32), jax.ShapeDtypeStruct((n_seq, ctx_rows, d), _F32)]
        scratch.append(pltpu.VMEM((ctx_rows, d), _F32))
    return pl.pallas_call(
        functools.partial(_odd_body, sample=sample, seq_len=seq_len, tiles_per_seq=tiles_per_seq),
        grid=(n_tiles,),
        in_specs=in_specs,
        out_specs=out_specs,
        out_shape=out_shape,
        scratch_shapes=scratch,
        input_output_aliases={0: 0},
        compiler_params=_params(),
        name="odd_sample" if sample else "odd_prompt",
    )(*args)


def kernel(x_prompt, x_sample, state_conv, state_pool, norm_g, final_norm_g, ffn_w_gate, ffn_w_up, ffn_w_down,
           e_w_in, e_ln_g, e_ln_b, e_sgu_w, e_sgu_b, e_conv_w, e_w_out, o_w_in, o_pool_w, o_pool_scale, o_w_out):
    batch, seq, d = x_prompt.shape
    dec_batch, dec_seq, _ = x_sample.shape
    n_prompt, n_sample = batch * seq, dec_batch * dec_seq
    d_a = _NH_A * _HD_A
    d_b = e_conv_w.shape[-1]
    bf = lambda w: w.astype(_BF)
    ffn = lambda xs, layer, k, **kw: _ffn(
        xs, norm_g[layer, 2 * k], bf(ffn_w_gate[layer, k]), bf(ffn_w_up[layer, k]), bf(ffn_w_down[layer, k]),
        n_prompt=n_prompt, n_sample=n_sample, **kw)

    xc = ffn((x_prompt.reshape(n_prompt, d), x_sample.reshape(n_sample, d)), 0, 0)
    even = functools.partial(_even_mixer, g=norm_g[0, 1], w_in=bf(e_w_in[0]), ln_g=e_ln_g[0], ln_b=e_ln_b[0],
                             sgu_w=e_sgu_w[0], sgu_b=e_sgu_b[0], conv_w=e_conv_w[0], w_out=bf(e_w_out[0]))
    xc, conv_p = even(xc, row0=0, n_rows=n_prompt, seq_len=seq, chunk_len=_CHUNK)
    xc, xg_s, v_s = even(xc, row0=n_prompt, n_rows=n_sample, seq_len=dec_seq, chunk_len=dec_seq,
                         conv_state=state_conv[0])
    xc = ffn((xc,), 0, 1)

    xc = ffn((xc,), 1, 0)
    odd = functools.partial(_odd_mixer, g=norm_g[1, 1], w_in=bf(o_w_in[0]), pool_w=bf(o_pool_w[0]),
                            pool_scale=o_pool_scale[0], w_out=bf(o_w_out[0]))
    xc, pool_p = odd(xc, row0=0, n_rows=n_prompt, seq_len=seq)
    xc, p_s = odd(xc, row0=n_prompt, n_rows=n_sample, seq_len=dec_seq, pool_state=state_pool[0])
    y_p, y_s = ffn((xc,), 1, 1, final_g=final_norm_g)

    keep = _CONV_W - 1
    conv_prompt = conv_p[:, _SUBLANES - keep:][None]
    conv_sample = xg_s.reshape(dec_batch, dec_seq, d_b)[:, dec_seq - keep:][None]
    chunk_v_sample = v_s.reshape(1, dec_batch, dec_seq, d_a)
    pool_prompt = pool_p[:, pool_p.shape[1] - _POOL_CTX:][None]
    pool_sample = jnp.concatenate([state_pool[0], p_s.reshape(dec_batch, dec_seq, d)], axis=1)[:, -_POOL_CTX:][None]
    return (y_p.reshape(batch, seq, d), y_s.reshape(dec_batch, dec_seq, d), conv_prompt, conv_sample,
            chunk_v_sample, pool_prompt, pool_sample)
```

```python
import functools

import jax
import jax.numpy as jnp
from jax import lax
from jax.experimental import pallas as pl
from jax.experimental.pallas import tpu as pltpu

_CHUNK = 128
_NH_A = 4
_HD_A = 128
_CONV_W = 3
_POOL_WINDOWS = (2, 4, 8, 16)
_POOL_CTX = max(_POOL_WINDOWS) - 1
_PAST_LEN = 16384
_EPS = 1e-6

_SUBLANES = 8
_VMEM_LIMIT = 56 * 1024 * 1024

_TM = 512
_FFN_CHUNKS = ((0, 1536), (1536, 1280))

_BF = jnp.bfloat16
_F32 = jnp.float32


def _rms(x, g):
    return x * lax.rsqrt(jnp.mean(x * x, axis=-1, keepdims=True) + _EPS) * g


def _dot(a, b):
    return jnp.dot(a, b, preferred_element_type=_F32)


def _resident(shape):
    nd = len(shape)
    return pl.BlockSpec(shape, lambda i: (0,) * nd, pipeline_mode=pl.Buffered(1))


def _params():
    return pltpu.CompilerParams(dimension_semantics=("arbitrary",), vmem_limit_bytes=_VMEM_LIMIT)


def _ffn_body(*refs, split_in, final, n_prompt_tiles):
    refs = list(refs)
    i = pl.program_id(0)
    if split_in:
        xp_ref, xs_ref = refs[:2]
        refs = refs[2:]
        x = jnp.where(i < n_prompt_tiles, xp_ref[...], xs_ref[...])
    else:
        x = refs.pop(0)[...]
    g_ref, wg_ref, wu_ref, wd_ref = refs[:4]
    refs = refs[4:]
    h = _rms(x, g_ref[...]).astype(_BF)
    acc = None
    for c0, cn in _FFN_CHUNKS:
        a = _dot(h, wg_ref[:, c0:c0 + cn])
        b = _dot(h, wu_ref[:, c0:c0 + cn])
        act = (a * jax.nn.sigmoid(a) * b).astype(_BF)
        d = _dot(act, wd_ref[c0:c0 + cn, :])
        acc = d if acc is None else acc + d
    y = x + 0.5 * acc
    if final:
        fg_ref, yp_ref, ys_ref = refs
        y = _rms(y, fg_ref[...])

        @pl.when(i < n_prompt_tiles)
        def _():
            yp_ref[...] = y

        @pl.when(i >= n_prompt_tiles)
        def _():
            ys_ref[...] = y
    else:
        (o_ref,) = refs
        o_ref[...] = y


def _ffn(xs, g, wg, wu, wd, *, layer, k, n_prompt, n_sample, final_g=None):
    d, dff = wg.shape[2:]
    pick = lambda r, c: pl.BlockSpec((None, None, r, c), lambda i: (layer, k, 0, 0), pipeline_mode=pl.Buffered(1))
    npt, nst = n_prompt // _TM, n_sample // _TM
    split_in = len(xs) == 2
    final = final_g is not None
    row = lambda i: (i, 0)
    prow = lambda i: (jnp.minimum(i, npt - 1), 0)
    srow = lambda i: (jnp.maximum(i - npt, 0), 0)
    tile = (_TM, d)
    in_specs = [pl.BlockSpec(tile, prow), pl.BlockSpec(tile, srow)] if split_in else [pl.BlockSpec(tile, row)]
    in_specs += [_resident((1, d)), pick(d, dff), pick(d, dff), pick(dff, d)]
    args = list(xs) + [g.reshape(1, d), wg, wu, wd]
    if final:
        in_specs.append(_resident((1, d)))
        args.append(final_g.reshape(1, d))
        out_specs = [pl.BlockSpec(tile, prow), pl.BlockSpec(tile, srow)]
        out_shape = [jax.ShapeDtypeStruct((n_prompt, d), _F32), jax.ShapeDtypeStruct((n_sample, d), _F32)]
    else:
        out_specs = pl.BlockSpec(tile, row)
        out_shape = jax.ShapeDtypeStruct((n_prompt + n_sample, d), _F32)
    return pl.pallas_call(
        functools.partial(_ffn_body, split_in=split_in, final=final, n_prompt_tiles=npt),
        grid=(npt + nst,),
        in_specs=in_specs,
        out_specs=out_specs,
        out_shape=out_shape,
        compiler_params=_params(),
        name="ffn_final" if final else ("ffn_first" if split_in else "ffn"),
    )(*args)


def _gelu(x):
    return 0.5 * x * (1.0 + lax.erf(x * (2.0 ** -0.5)))


def _even_body(*refs, sample, chunk_len, tiles_per_seq):
    if sample:
        (x_ref, g_ref, win_ref, lng_ref, lnb_ref, sw_ref, sb_ref, cw_ref, wout_ref, b1_ref, b2_ref,
         o_ref, xg_ref, v_ref, yab_ref) = refs
    else:
        (x_ref, g_ref, win_ref, lng_ref, lnb_ref, sw_ref, sb_ref, cw_ref, wout_ref,
         o_ref, cs_ref, yab_ref, carry_ref) = refs
    i = pl.program_id(0)
    tm = x_ref.shape[0]
    d_a = _NH_A * _HD_A
    d_b = cw_ref.shape[1]
    x = x_ref[...]
    h = _rms(x, g_ref[...]).astype(_BF)
    z = _dot(h, win_ref[...])
    u = _gelu(z[:, 0:d_a])
    vv = _gelu(z[:, d_a:2 * d_a])
    mu = jnp.mean(vv, axis=-1, keepdims=True)
    vc = vv - mu
    var = jnp.mean(vc * vc, axis=-1, keepdims=True)
    v = vc * lax.rsqrt(var + _EPS) * lng_ref[...] + lnb_ref[...]
    if sample:
        v_ref[...] = v
    o = 2 * d_a
    gate_b = z[:, o:o + d_b]
    gate_c = z[:, o + d_b:o + 2 * d_b]
    x_in = z[:, o + 2 * d_b:o + 3 * d_b]

    tt = lax.broadcasted_iota(jnp.int32, (_CHUNK, _CHUNK), 0)
    ss = lax.broadcasted_iota(jnp.int32, (_CHUNK, _CHUNK), 1)
    keep = (ss <= tt) & ((tt // chunk_len) == (ss // chunk_len))
    vb = v.astype(_BF)
    for hd in range(_NH_A):
        w = jnp.where(keep, sw_ref[hd], 0.0).astype(_BF)
        bias = sb_ref[hd]
        cols = slice(hd * _HD_A, (hd + 1) * _HD_A)
        for c in range(tm // _CHUNK):
            rows = slice(c * _CHUNK, (c + 1) * _CHUNK)
            mixed = _dot(w, vb[rows, cols]) + bias
            yab_ref[rows, cols] = (u[rows, cols] * mixed).astype(_BF)

    xg = gate_c * x_in
    cw = cw_ref[...]
    if sample:
        t8 = lax.broadcasted_iota(jnp.int32, (tm, 1), 0) % chunk_len
        back1 = jnp.where(t8 >= 1, pltpu.roll(xg, 1, axis=0), b1_ref[...])
        back2 = jnp.where(t8 >= 2, pltpu.roll(xg, 2, axis=0), b2_ref[...])
        xg_ref[...] = xg
    else:
        @pl.when((i % tiles_per_seq) == 0)
        def _():
            carry_ref[...] = jnp.zeros_like(carry_ref)

        ext = jnp.concatenate([carry_ref[...], xg], axis=0)
        back1 = pltpu.roll(ext, 1, axis=0)[_SUBLANES:]
        back2 = pltpu.roll(ext, 2, axis=0)[_SUBLANES:]
        carry_ref[...] = xg[tm - _SUBLANES:]
        cs_ref[0] = xg[tm - _SUBLANES:]
    conv = back2 * cw[0:1] + back1 * cw[1:2] + xg * cw[2:3]
    yab_ref[:, d_a:d_a + d_b] = (gate_b * conv).astype(_BF)

    o_ref[...] = x + _dot(yab_ref[...], wout_ref[...])


def _even_mixer(xc, g, w_in, ln_g, ln_b, sgu_w, sgu_b, conv_w, w_out, *, row0, n_rows, seq_len,
                chunk_len, conv_state=None):
    sample = conv_state is not None
    t, d = xc.shape
    d_in = w_in.shape[1]
    d_a = _NH_A * _HD_A
    d_b = conv_w.shape[1]
    tile0 = row0 // _TM
    n_tiles = n_rows // _TM
    row = lambda i: (tile0 + i, 0)
    reps = _CHUNK // chunk_len
    sw = jnp.tile(sgu_w[:, :chunk_len, :chunk_len], (1, reps, reps))
    sb = jnp.broadcast_to(jnp.tile(sgu_b[:, :chunk_len], (1, reps))[:, :, None], (_NH_A, _CHUNK, _HD_A))
    in_specs = [pl.BlockSpec((_TM, d), row), _resident((1, d)), _resident((d, d_in)), _resident((1, d_a)),
                _resident((1, d_a)), _resident(sw.shape), _resident(sb.shape), _resident(conv_w.shape),
                _resident(w_out.shape)]
    args = [xc, g.reshape(1, d), w_in, ln_g.reshape(1, d_a), ln_b.reshape(1, d_a), sw, sb, conv_w, w_out]
    scratch = [pltpu.VMEM((_TM, d_a + d_b), _BF)]
    if sample:
        n_seq = n_rows // chunk_len
        pad = lambda a: jnp.pad(a, ((0, 0), (0, chunk_len - a.shape[1]), (0, 0))).reshape(n_rows, d_b)
        b1 = pad(conv_state[:, 1:2])
        b2 = pad(conv_state)
        local = lambda i: (i, 0)
        in_specs += [pl.BlockSpec((_TM, d_b), local), pl.BlockSpec((_TM, d_b), local)]
        args += [b1, b2]
        out_specs = [pl.BlockSpec((_TM, d), row), pl.BlockSpec((_TM, d_b), local), pl.BlockSpec((_TM, d_a), local)]
        out_shape = [jax.ShapeDtypeStruct((t, d), _F32), jax.ShapeDtypeStruct((n_rows, d_b), _F32),
                     jax.ShapeDtypeStruct((n_rows, d_a), _F32)]
        tiles_per_seq = 1
    else:
        n_seq = n_rows // seq_len
        tiles_per_seq = seq_len // _TM
        out_specs = [pl.BlockSpec((_TM, d), row),
                     pl.BlockSpec((1, _SUBLANES, d_b), lambda i: (i // tiles_per_seq, 0, 0))]
        out_shape = [jax.ShapeDtypeStruct((t, d), _F32), jax.ShapeDtypeStruct((n_seq, _SUBLANES, d_b), _F32)]
        scratch.append(pltpu.VMEM((_SUBLANES, d_b), _F32))
    return pl.pallas_call(
        functools.partial(_even_body, sample=sample, chunk_len=chunk_len, tiles_per_seq=tiles_per_seq),
        grid=(n_tiles,),
        in_specs=in_specs,
        out_specs=out_specs,
        out_shape=out_shape,
        scratch_shapes=scratch,
        input_output_aliases={0: 0},
        compiler_params=_params(),
        name="even_sample" if sample else "even_prompt",
    )(*args)


def _window_sums(ext):
    gc = ext.shape[1] // len(_POOL_WINDOWS)
    sums = []
    s = ext
    width = 1
    for w in _POOL_WINDOWS:
        while width < w:
            s = s + pltpu.roll(s, width, axis=0)
            width *= 2
        sums.append(s[:, 0:gc])
        s = s[:, gc:]
    return sums


def _odd_body(*refs, sample, seq_len, tiles_per_seq):
    if sample:
        (x_ref, g_ref, win_ref, pw_ref, psc_ref, wout_ref, ctx_ref, o_ref, p_ref, y_ref, ext_ref) = refs
    else:
        (x_ref, g_ref, win_ref, pw_ref, psc_ref, wout_ref, o_ref, ps_ref, y_ref, carry_ref) = refs
    i = pl.program_id(0)
    tm, d = x_ref.shape
    ctx_rows = 2 * _SUBLANES
    x = x_ref[...]
    h = _rms(x, g_ref[...]).astype(_BF)
    p = _dot(h, win_ref[...])
    t_in_tile = lax.broadcasted_iota(jnp.int32, (tm, 1), 0)
    if sample:
        nb = tm // seq_len
        ext_ref[:, 0:ctx_rows, :] = ctx_ref[...]
        ext_ref[:, ctx_rows:ctx_rows + seq_len, :] = p.reshape(nb, seq_len, d)
        ext = ext_ref[...].reshape(nb * (ctx_rows + seq_len), d)
        pick = lambda s: s.reshape(nb, ctx_rows + seq_len, s.shape[1])[:, ctx_rows:, :].reshape(tm, s.shape[1])
        pos = _PAST_LEN + t_in_tile % seq_len
        p_ref[...] = p
    else:
        @pl.when((i % tiles_per_seq) == 0)
        def _():
            carry_ref[...] = jnp.zeros_like(carry_ref)

        ext = jnp.concatenate([carry_ref[...], p], axis=0)
        pick = lambda s: s[ctx_rows:]
        pos = (i % tiles_per_seq) * tm + t_in_tile
        carry_ref[...] = p[tm - ctx_rows:]
        ps_ref[0] = p[tm - ctx_rows:]
    gc = d // len(_POOL_WINDOWS)
    for grp, (w, s) in enumerate(zip(_POOL_WINDOWS, _window_sums(ext))):
        cnt = jnp.minimum(pos + 1, w).astype(_F32)
        cols = slice(grp * gc, (grp + 1) * gc)
        dlt = pick(s) / cnt - p[:, cols]
        yg = _dot(dlt.astype(_BF), pw_ref[grp])
        y_ref[:, cols] = (yg * psc_ref[:, cols]).astype(_BF)
    o_ref[...] = x + _dot(y_ref[...], wout_ref[...])


def _odd_mixer(xc, g, w_in, pool_w, pool_scale, w_out, *, row0, n_rows, seq_len, pool_state=None):
    sample = pool_state is not None
    t, d = xc.shape
    ctx_rows = 2 * _SUBLANES
    tm = _TM // 2 if sample else _TM
    tile0 = row0 // tm
    n_tiles = n_rows // tm
    row = lambda i: (tile0 + i, 0)
    in_specs = [pl.BlockSpec((tm, d), row), _resident((1, d)), _resident((d, d)), _resident(pool_w.shape),
                _resident((1, d)), _resident((d, d))]
    args = [xc, g.reshape(1, d), w_in, pool_w, pool_scale.reshape(1, d), w_out]
    scratch = [pltpu.VMEM((tm, d), _BF)]
    if sample:
        nb = tm // seq_len
        ctx = jnp.pad(pool_state, ((0, 0), (ctx_rows - _POOL_CTX, 0), (0, 0)))
        in_specs.append(pl.BlockSpec((nb, ctx_rows, d), lambda i: (i, 0, 0)))
        args.append(ctx)
        out_specs = [pl.BlockSpec((tm, d), row), pl.BlockSpec((tm, d), lambda i: (i, 0))]
        out_shape = [jax.ShapeDtypeStruct((t, d), _F32), jax.ShapeDtypeStruct((n_rows, d), _F32)]
        scratch.append(pltpu.VMEM((nb, ctx_rows + seq_len, d), _F32))
        tiles_per_seq = 1
    else:
        n_seq = n_rows // seq_len
        tiles_per_seq = seq_len // tm
        out_specs = [pl.BlockSpec((tm, d), row),
                     pl.BlockSpec((1, ctx_rows, d), lambda i: (i // tiles_per_seq, 0, 0))]
        out_shape = [jax.ShapeDtypeStruct((t, d), _F32), jax.ShapeDtypeStruct((n_seq, ctx_rows, d), _F32)]
        scratch.append(pltpu.VMEM((ctx_rows, d), _F32))
    return pl.pallas_call(
        functools.partial(_odd_body, sample=sample, seq_len=seq_len, tiles_per_seq=tiles_per_seq),
        grid=(n_tiles,),
        in_specs=in_specs,
        out_specs=out_specs,
        out_shape=out_shape,
        scratch_shapes=scratch,
        input_output_aliases={0: 0},
        compiler_params=_params(),
        name="odd_sample" if sample else "odd_prompt",
    )(*args)


def kernel(x_prompt, x_sample, state_conv, state_pool, norm_g, final_norm_g, ffn_w_gate, ffn_w_up, ffn_w_down,
           e_w_in, e_ln_g, e_ln_b, e_sgu_w, e_sgu_b, e_conv_w, e_w_out, o_w_in, o_pool_w, o_pool_scale, o_w_out):
    batch, seq, d = x_prompt.shape
    dec_batch, dec_seq, _ = x_sample.shape
    n_prompt, n_sample = batch * seq, dec_batch * dec_seq
    d_a = _NH_A * _HD_A
    d_b = e_conv_w.shape[-1]
    bf = lambda w: w.astype(_BF)
    wg, wu, wd = bf(ffn_w_gate), bf(ffn_w_up), bf(ffn_w_down)
    ffn = lambda xs, layer, k, **kw: _ffn(xs, norm_g[layer, 2 * k], wg, wu, wd, layer=layer, k=k,
                                          n_prompt=n_prompt, n_sample=n_sample, **kw)

    xc = ffn((x_prompt.reshape(n_prompt, d), x_sample.reshape(n_sample, d)), 0, 0)
    even = functools.partial(_even_mixer, g=norm_g[0, 1], w_in=bf(e_w_in[0]), ln_g=e_ln_g[0], ln_b=e_ln_b[0],
                             sgu_w=e_sgu_w[0], sgu_b=e_sgu_b[0], conv_w=e_conv_w[0], w_out=bf(e_w_out[0]))
    xc, conv_p = even(xc, row0=0, n_rows=n_prompt, seq_len=seq, chunk_len=_CHUNK)
    xc, xg_s, v_s = even(xc, row0=n_prompt, n_rows=n_sample, seq_len=dec_seq, chunk_len=dec_seq,
                         conv_state=state_conv[0])
    xc = ffn((xc,), 0, 1)

    xc = ffn((xc,), 1, 0)
    odd = functools.partial(_odd_mixer, g=norm_g[1, 1], w_in=bf(o_w_in[0]), pool_w=bf(o_pool_w[0]),
                            pool_scale=o_pool_scale[0], w_out=bf(o_w_out[0]))
    xc, pool_p = odd(xc, row0=0, n_rows=n_prompt, seq_len=seq)
    xc, p_s = odd(xc, row0=n_prompt, n_rows=n_sample, seq_len=dec_seq, pool_state=state_pool[0])
    y_p, y_s = ffn((xc,), 1, 1, final_g=final_norm_g)

    keep = _CONV_W - 1
    conv_prompt = conv_p[:, _SUBLANES - keep:][None]
    conv_sample = xg_s.reshape(dec_batch, dec_seq, d_b)[:, dec_seq - keep:][None]
    chunk_v_sample = v_s.reshape(1, dec_batch, dec_seq, d_a)
    pool_prompt = pool_p[:, pool_p.shape[1] - _POOL_CTX:][None]
    pool_sample = jnp.concatenate([state_pool[0], p_s.reshape(dec_batch, dec_seq, d)], axis=1)[:, -_POOL_CTX:][None]
    return (y_p.reshape(batch, seq, d), y_s.reshape(dec_batch, dec_seq, d), conv_prompt, conv_sample,
            chunk_v_sample, pool_prompt, pool_sample)
```

```python
import functools

import jax
import jax.numpy as jnp
from jax import lax
from jax.experimental import pallas as pl
from jax.experimental.pallas import tpu as pltpu

_CHUNK = 128
_NH_A = 4
_HD_A = 128
_CONV_W = 3
_POOL_WINDOWS = (2, 4, 8, 16)
_POOL_CTX = max(_POOL_WINDOWS) - 1
_PAST_LEN = 16384
_EPS = 1e-6

_SUBLANES = 8
_VMEM_LIMIT = 56 * 1024 * 1024

_TM = 512
_FFN_TM = 1024
_FFN_SUB = 512
_FFN_CHUNKS = ((0, 1536), (1536, 1280))

_BF = jnp.bfloat16
_F32 = jnp.float32


def _rms(x, g):
    return x * lax.rsqrt(jnp.mean(x * x, axis=-1, keepdims=True) + _EPS) * g


def _dot(a, b):
    return jnp.dot(a, b, preferred_element_type=_F32)


def _resident(shape):
    nd = len(shape)
    return pl.BlockSpec(shape, lambda i: (0,) * nd, pipeline_mode=pl.Buffered(1))


def _params():
    return pltpu.CompilerParams(dimension_semantics=("arbitrary",), vmem_limit_bytes=_VMEM_LIMIT)


def _ffn_body(*refs, split_in, final, n_prompt_tiles):
    refs = list(refs)
    i = pl.program_id(0)
    x_refs = [refs.pop(0) for _ in range(2 if split_in else 1)]
    g_ref, wg_ref, wu_ref, wd_ref = refs[:4]
    refs = refs[4:]
    is_prompt = i < n_prompt_tiles
    for r0 in range(0, _FFN_TM, _FFN_SUB):
        rows = slice(r0, r0 + _FFN_SUB)
        if split_in:
            x = jnp.where(is_prompt, x_refs[0][rows, :], x_refs[1][rows, :])
        else:
            x = x_refs[0][rows, :]
        h = _rms(x, g_ref[...]).astype(_BF)
        acc = None
        for c0, cn in _FFN_CHUNKS:
            a = _dot(h, wg_ref[:, c0:c0 + cn])
            b = _dot(h, wu_ref[:, c0:c0 + cn])
            act = (a * jax.nn.sigmoid(a) * b).astype(_BF)
            d = _dot(act, wd_ref[c0:c0 + cn, :])
            acc = d if acc is None else acc + d
        y = x + 0.5 * acc
        if final:
            fg_ref, yp_ref, ys_ref = refs
            y = _rms(y, fg_ref[...])

            @pl.when(is_prompt)
            def _():
                yp_ref[rows, :] = y

            @pl.when(jnp.logical_not(is_prompt))
            def _():
                ys_ref[rows, :] = y
        else:
            (o_ref,) = refs
            o_ref[rows, :] = y


def _ffn(xs, g, wg, wu, wd, *, layer, k, n_prompt, n_sample, final_g=None):
    d, dff = wg.shape[2:]
    pick = lambda r, c: pl.BlockSpec((None, None, r, c), lambda i: (layer, k, 0, 0), pipeline_mode=pl.Buffered(1))
    npt, nst = n_prompt // _FFN_TM, n_sample // _FFN_TM
    split_in = len(xs) == 2
    final = final_g is not None
    row = lambda i: (i, 0)
    prow = lambda i: (jnp.minimum(i, npt - 1), 0)
    srow = lambda i: (jnp.maximum(i - npt, 0), 0)
    tile = (_FFN_TM, d)
    in_specs = [pl.BlockSpec(tile, prow), pl.BlockSpec(tile, srow)] if split_in else [pl.BlockSpec(tile, row)]
    in_specs += [_resident((1, d)), pick(d, dff), pick(d, dff), pick(dff, d)]
    args = list(xs) + [g.reshape(1, d), wg, wu, wd]
    if final:
        in_specs.append(_resident((1, d)))
        args.append(final_g.reshape(1, d))
        out_specs = [pl.BlockSpec(tile, prow), pl.BlockSpec(tile, srow)]
        out_shape = [jax.ShapeDtypeStruct((n_prompt, d), _F32), jax.ShapeDtypeStruct((n_sample, d), _F32)]
    else:
        out_specs = pl.BlockSpec(tile, row)
        out_shape = jax.ShapeDtypeStruct((n_prompt + n_sample, d), _F32)
    return pl.pallas_call(
        functools.partial(_ffn_body, split_in=split_in, final=final, n_prompt_tiles=npt),
        grid=(npt + nst,),
        in_specs=in_specs,
        out_specs=out_specs,
        out_shape=out_shape,
        compiler_params=_params(),
        name="ffn_final" if final else ("ffn_first" if split_in else "ffn"),
    )(*args)


def _gelu(x):
    return 0.5 * x * (1.0 + lax.erf(x * (2.0 ** -0.5)))


def _even_body(*refs, sample, chunk_len, tiles_per_seq):
    if sample:
        (x_ref, g_ref, win_ref, lng_ref, lnb_ref, sw_ref, sb_ref, cw_ref, wout_ref, b1_ref, b2_ref,
         o_ref, xg_ref, v_ref, yab_ref) = refs
    else:
        (x_ref, g_ref, win_ref, lng_ref, lnb_ref, sw_ref, sb_ref, cw_ref, wout_ref,
         o_ref, cs_ref, yab_ref, carry_ref) = refs
    i = pl.program_id(0)
    tm = x_ref.shape[0]
    d_a = _NH_A * _HD_A
    d_b = cw_ref.shape[1]
    x = x_ref[...]
    h = _rms(x, g_ref[...]).astype(_BF)
    z = _dot(h, win_ref[...])
    u = _gelu(z[:, 0:d_a])
    vv = _gelu(z[:, d_a:2 * d_a])
    mu = jnp.mean(vv, axis=-1, keepdims=True)
    vc = vv - mu
    var = jnp.mean(vc * vc, axis=-1, keepdims=True)
    v = vc * lax.rsqrt(var + _EPS) * lng_ref[...] + lnb_ref[...]
    if sample:
        v_ref[...] = v
    o = 2 * d_a
    gate_b = z[:, o:o + d_b]
    gate_c = z[:, o + d_b:o + 2 * d_b]
    x_in = z[:, o + 2 * d_b:o + 3 * d_b]

    tt = lax.broadcasted_iota(jnp.int32, (_CHUNK, _CHUNK), 0)
    ss = lax.broadcasted_iota(jnp.int32, (_CHUNK, _CHUNK), 1)
    keep = (ss <= tt) & ((tt // chunk_len) == (ss // chunk_len))
    vb = v.astype(_BF)
    for hd in range(_NH_A):
        w = jnp.where(keep, sw_ref[hd], 0.0).astype(_BF)
        bias = sb_ref[hd]
        cols = slice(hd * _HD_A, (hd + 1) * _HD_A)
        for c in range(tm // _CHUNK):
            rows = slice(c * _CHUNK, (c + 1) * _CHUNK)
            mixed = _dot(w, vb[rows, cols]) + bias
            yab_ref[rows, cols] = (u[rows, cols] * mixed).astype(_BF)

    xg = gate_c * x_in
    cw = cw_ref[...]
    if sample:
        t8 = lax.broadcasted_iota(jnp.int32, (tm, 1), 0) % chunk_len
        back1 = jnp.where(t8 >= 1, pltpu.roll(xg, 1, axis=0), b1_ref[...])
        back2 = jnp.where(t8 >= 2, pltpu.roll(xg, 2, axis=0), b2_ref[...])
        xg_ref[...] = xg
    else:
        @pl.when((i % tiles_per_seq) == 0)
        def _():
            carry_ref[...] = jnp.zeros_like(carry_ref)

        ext = jnp.concatenate([carry_ref[...], xg], axis=0)
        back1 = pltpu.roll(ext, 1, axis=0)[_SUBLANES:]
        back2 = pltpu.roll(ext, 2, axis=0)[_SUBLANES:]
        carry_ref[...] = xg[tm - _SUBLANES:]
        cs_ref[0] = xg[tm - _SUBLANES:]
    conv = back2 * cw[0:1] + back1 * cw[1:2] + xg * cw[2:3]
    yab_ref[:, d_a:d_a + d_b] = (gate_b * conv).astype(_BF)

    o_ref[...] = x + _dot(yab_ref[...], wout_ref[...])


def _even_mixer(xc, g, w_in, ln_g, ln_b, sgu_w, sgu_b, conv_w, w_out, *, row0, n_rows, seq_len,
                chunk_len, conv_state=None):
    sample = conv_state is not None
    t, d = xc.shape
    d_in = w_in.shape[1]
    d_a = _NH_A * _HD_A
    d_b = conv_w.shape[1]
    tile0 = row0 // _TM
    n_tiles = n_rows // _TM
    row = lambda i: (tile0 + i, 0)
    reps = _CHUNK // chunk_len
    sw = jnp.tile(sgu_w[:, :chunk_len, :chunk_len], (1, reps, reps))
    sb = jnp.broadcast_to(jnp.tile(sgu_b[:, :chunk_len], (1, reps))[:, :, None], (_NH_A, _CHUNK, _HD_A))
    in_specs = [pl.BlockSpec((_TM, d), row), _resident((1, d)), _resident((d, d_in)), _resident((1, d_a)),
                _resident((1, d_a)), _resident(sw.shape), _resident(sb.shape), _resident(conv_w.shape),
                _resident(w_out.shape)]
    args = [xc, g.reshape(1, d), w_in, ln_g.reshape(1, d_a), ln_b.reshape(1, d_a), sw, sb, conv_w, w_out]
    scratch = [pltpu.VMEM((_TM, d_a + d_b), _BF)]
    if sample:
        n_seq = n_rows // chunk_len
        pad = lambda a: jnp.pad(a, ((0, 0), (0, chunk_len - a.shape[1]), (0, 0))).reshape(n_rows, d_b)
        b1 = pad(conv_state[:, 1:2])
        b2 = pad(conv_state)
        local = lambda i: (i, 0)
        in_specs += [pl.BlockSpec((_TM, d_b), local), pl.BlockSpec((_TM, d_b), local)]
        args += [b1, b2]
        out_specs = [pl.BlockSpec((_TM, d), row), pl.BlockSpec((_TM, d_b), local), pl.BlockSpec((_TM, d_a), local)]
        out_shape = [jax.ShapeDtypeStruct((t, d), _F32), jax.ShapeDtypeStruct((n_rows, d_b), _F32),
                     jax.ShapeDtypeStruct((n_rows, d_a), _F32)]
        tiles_per_seq = 1
    else:
        n_seq = n_rows // seq_len
        tiles_per_seq = seq_len // _TM
        out_specs = [pl.BlockSpec((_TM, d), row),
                     pl.BlockSpec((1, _SUBLANES, d_b), lambda i: (i // tiles_per_seq, 0, 0))]
        out_shape = [jax.ShapeDtypeStruct((t, d), _F32), jax.ShapeDtypeStruct((n_seq, _SUBLANES, d_b), _F32)]
        scratch.append(pltpu.VMEM((_SUBLANES, d_b), _F32))
    return pl.pallas_call(
        functools.partial(_even_body, sample=sample, chunk_len=chunk_len, tiles_per_seq=tiles_per_seq),
        grid=(n_tiles,),
        in_specs=in_specs,
        out_specs=out_specs,
        out_shape=out_shape,
        scratch_shapes=scratch,
        input_output_aliases={0: 0},
        compiler_params=_params(),
        name="even_sample" if sample else "even_prompt",
    )(*args)


def _window_sums(ext):
    gc = ext.shape[1] // len(_POOL_WINDOWS)
    sums = []
    s = ext
    width = 1
    for w in _POOL_WINDOWS:
        while width < w:
            s = s + pltpu.roll(s, width, axis=0)
            width *= 2
        sums.append(s[:, 0:gc])
        s = s[:, gc:]
    return sums


def _odd_body(*refs, sample, seq_len, tiles_per_seq):
    if sample:
        (x_ref, g_ref, win_ref, pw_ref, psc_ref, wout_ref, ctx_ref, o_ref, p_ref, y_ref, ext_ref) = refs
    else:
        (x_ref, g_ref, win_ref, pw_ref, psc_ref, wout_ref, o_ref, ps_ref, y_ref, carry_ref) = refs
    i = pl.program_id(0)
    tm, d = x_ref.shape
    ctx_rows = 2 * _SUBLANES
    x = x_ref[...]
    h = _rms(x, g_ref[...]).astype(_BF)
    p = _dot(h, win_ref[...])
    t_in_tile = lax.broadcasted_iota(jnp.int32, (tm, 1), 0)
    if sample:
        nb = tm // seq_len
        ext_ref[:, 0:ctx_rows, :] = ctx_ref[...]
        ext_ref[:, ctx_rows:ctx_rows + seq_len, :] = p.reshape(nb, seq_len, d)
        ext = ext_ref[...].reshape(nb * (ctx_rows + seq_len), d)
        pick = lambda s: s.reshape(nb, ctx_rows + seq_len, s.shape[1])[:, ctx_rows:, :].reshape(tm, s.shape[1])
        pos = _PAST_LEN + t_in_tile % seq_len
        p_ref[...] = p
    else:
        @pl.when((i % tiles_per_seq) == 0)
        def _():
            carry_ref[...] = jnp.zeros_like(carry_ref)

        ext = jnp.concatenate([carry_ref[...], p], axis=0)
        pick = lambda s: s[ctx_rows:]
        pos = (i % tiles_per_seq) * tm + t_in_tile
        carry_ref[...] = p[tm - ctx_rows:]
        ps_ref[0] = p[tm - ctx_rows:]
    gc = d // len(_POOL_WINDOWS)
    for grp, (w, s) in enumerate(zip(_POOL_WINDOWS, _window_sums(ext))):
        cnt = jnp.minimum(pos + 1, w).astype(_F32)
        cols = slice(grp * gc, (grp + 1) * gc)
        dlt = pick(s) / cnt - p[:, cols]
        yg = _dot(dlt.astype(_BF), pw_ref[grp])
        y_ref[:, cols] = (yg * psc_ref[:, cols]).astype(_BF)
    o_ref[...] = x + _dot(y_ref[...], wout_ref[...])


def _odd_mixer(xc, g, w_in, pool_w, pool_scale, w_out, *, row0, n_rows, seq_len, pool_state=None):
    sample = pool_state is not None
    t, d = xc.shape
    ctx_rows = 2 * _SUBLANES
    tm = _TM // 2 if sample else _TM
    tile0 = row0 // tm
    n_tiles = n_rows // tm
    row = lambda i: (tile0 + i, 0)
    in_specs = [pl.BlockSpec((tm, d), row), _resident((1, d)), _resident((d, d)), _resident(pool_w.shape),
                _resident((1, d)), _resident((d, d))]
    args = [xc, g.reshape(1, d), w_in, pool_w, pool_scale.reshape(1, d), w_out]
    scratch = [pltpu.VMEM((tm, d), _BF)]
    if sample:
        nb = tm // seq_len
        ctx = jnp.pad(pool_state, ((0, 0), (ctx_rows - _POOL_CTX, 0), (0, 0)))
        in_specs.append(pl.BlockSpec((nb, ctx_rows, d), lambda i: (i, 0, 0)))
        args.append(ctx)
        out_specs = [pl.BlockSpec((tm, d), row), pl.BlockSpec((tm, d), lambda i: (i, 0))]
        out_shape = [jax.ShapeDtypeStruct((t, d), _F32), jax.ShapeDtypeStruct((n_rows, d), _F32)]
        scratch.append(pltpu.VMEM((nb, ctx_rows + seq_len, d), _F32))
        tiles_per_seq = 1
    else:
        n_seq = n_rows // seq_len
        tiles_per_seq = seq_len // tm
        out_specs = [pl.BlockSpec((tm, d), row),
                     pl.BlockSpec((1, ctx_rows, d), lambda i: (i // tiles_per_seq, 0, 0))]
        out_shape = [jax.ShapeDtypeStruct((t, d), _F32), jax.ShapeDtypeStruct((n_seq, ctx_rows, d), _F32)]
        scratch.append(pltpu.VMEM((ctx_rows, d), _F32))
    return pl.pallas_call(
        functools.partial(_odd_body, sample=sample, seq_len=seq_len, tiles_per_seq=tiles_per_seq),
        grid=(n_tiles,),
        in_specs=in_specs,
        out_specs=out_specs,
        out_shape=out_shape,
        scratch_shapes=scratch,
        input_output_aliases={0: 0},
        compiler_params=_params(),
        name="odd_sample" if sample else "odd_prompt",
    )(*args)


def kernel(x_prompt, x_sample, state_conv, state_pool, norm_g, final_norm_g, ffn_w_gate, ffn_w_up, ffn_w_down,
           e_w_in, e_ln_g, e_ln_b, e_sgu_w, e_sgu_b, e_conv_w, e_w_out, o_w_in, o_pool_w, o_pool_scale, o_w_out):
    batch, seq, d = x_prompt.shape
    dec_batch, dec_seq, _ = x_sample.shape
    n_prompt, n_sample = batch * seq, dec_batch * dec_seq
    d_a = _NH_A * _HD_A
    d_b = e_conv_w.shape[-1]
    bf = lambda w: w.astype(_BF)
    wg, wu, wd = bf(ffn_w_gate), bf(ffn_w_up), bf(ffn_w_down)
    ffn = lambda xs, layer, k, **kw: _ffn(xs, norm_g[layer, 2 * k], wg, wu, wd, layer=layer, k=k,
                                          n_prompt=n_prompt, n_sample=n_sample, **kw)

    xc = ffn((x_prompt.reshape(n_prompt, d), x_sample.reshape(n_sample, d)), 0, 0)
    even = functools.partial(_even_mixer, g=norm_g[0, 1], w_in=bf(e_w_in[0]), ln_g=e_ln_g[0], ln_b=e_ln_b[0],
                             sgu_w=e_sgu_w[0], sgu_b=e_sgu_b[0], conv_w=e_conv_w[0], w_out=bf(e_w_out[0]))
    xc, conv_p = even(xc, row0=0, n_rows=n_prompt, seq_len=seq, chunk_len=_CHUNK)
    xc, xg_s, v_s = even(xc, row0=n_prompt, n_rows=n_sample, seq_len=dec_seq, chunk_len=dec_seq,
                         conv_state=state_conv[0])
    xc = ffn((xc,), 0, 1)

    xc = ffn((xc,), 1, 0)
    odd = functools.partial(_odd_mixer, g=norm_g[1, 1], w_in=bf(o_w_in[0]), pool_w=bf(o_pool_w[0]),
                            pool_scale=o_pool_scale[0], w_out=bf(o_w_out[0]))
    xc, pool_p = odd(xc, row0=0, n_rows=n_prompt, seq_len=seq)
    xc, p_s = odd(xc, row0=n_prompt, n_rows=n_sample, seq_len=dec_seq, pool_state=state_pool[0])
    y_p, y_s = ffn((xc,), 1, 1, final_g=final_norm_g)

    keep = _CONV_W - 1
    conv_prompt = conv_p[:, _SUBLANES - keep:][None]
    conv_sample = xg_s.reshape(dec_batch, dec_seq, d_b)[:, dec_seq - keep:][None]
    chunk_v_sample = v_s.reshape(1, dec_batch, dec_seq, d_a)
    pool_prompt = pool_p[:, pool_p.shape[1] - _POOL_CTX:][None]
    pool_sample = jnp.concatenate([state_pool[0], p_s.reshape(dec_batch, dec_seq, d)], axis=1)[:, -_POOL_CTX:][None]
    return (y_p.reshape(batch, seq, d), y_s.reshape(dec_batch, dec_seq, d), conv_prompt, conv_sample,
            chunk_v_sample, pool_prompt, pool_sample)
```

```python
import collections
import functools

import jax
import jax.numpy as jnp
from jax import lax
from jax.experimental import pallas as pl
from jax.experimental.pallas import tpu as pltpu

_CHUNK = 128
_NH_A = 4
_HD_A = 128
_CONV_W = 3
_POOL_WINDOWS = (2, 4, 8, 16)
_POOL_CTX = max(_POOL_WINDOWS) - 1
_PAST_LEN = 16384
_EPS = 1e-6

_SUBLANES = 8
_BF16_ROWS = 16
_VMEM_LIMIT = 56 * 1024 * 1024

_TM = 512
_FFN_TM = 1024
_FFN_SUB = 512
_FFN_CHUNKS = ((0, 1536), (1536, 1280))

_BF = jnp.bfloat16
_F32 = jnp.float32


def _rms(x, g):
    return x * lax.rsqrt(jnp.mean(x * x, axis=-1, keepdims=True) + _EPS) * g


def _dot(a, b):
    return jnp.dot(a, b, preferred_element_type=_F32)


def _resident(shape):
    nd = len(shape)
    return pl.BlockSpec(shape, lambda i: (0,) * nd, pipeline_mode=pl.Buffered(1))


_Cast = collections.namedtuple("_Cast", "array lead")


def _cast_specs(cast, grid):
    lead = tuple(cast.lead)
    rows, cols = cast.array.shape[len(lead):]
    steps = max(s for s in (1, 2, 4, 8, 16, 32) if s <= grid and rows % (s * _BF16_ROWS) == 0)
    blk = rows // steps
    step = lambda i: jnp.minimum(i, steps - 1)
    in_spec = pl.BlockSpec((None,) * len(lead) + (blk, cols), lambda i: lead + (step(i), 0))
    out_spec = pl.BlockSpec((blk, cols), lambda i: (step(i), 0))
    return in_spec, out_spec, jax.ShapeDtypeStruct((rows, cols), _BF)


def _call(body, *, name, grid, in_specs, args, out_specs, out_shape, scratch=(), casts=(), aliases=None):
    n_in, n_out, n_cast = len(in_specs), len(out_specs), len(casts)
    specs = [_cast_specs(c, grid) for c in casts]

    def wrapped(*refs):
        ins, refs = refs[:n_in], refs[n_in:]
        cast_in, refs = refs[:n_cast], refs[n_cast:]
        outs, refs = refs[:n_out], refs[n_out:]
        cast_out, scratch_refs = refs[:n_cast], refs[n_cast:]
        for src, dst in zip(cast_in, cast_out):
            dst[...] = src[...].astype(_BF)
        body(*ins, *outs, *scratch_refs)

    res = pl.pallas_call(
        wrapped,
        grid=(grid,),
        in_specs=list(in_specs) + [s[0] for s in specs],
        out_specs=list(out_specs) + [s[1] for s in specs],
        out_shape=list(out_shape) + [s[2] for s in specs],
        scratch_shapes=list(scratch),
        input_output_aliases=aliases or {},
        compiler_params=pltpu.CompilerParams(dimension_semantics=("arbitrary",), vmem_limit_bytes=_VMEM_LIMIT),
        name=name,
    )(*args, *[c.array for c in casts])
    return res[:n_out], res[n_out:]


def _ffn_body(*refs, split_in, final, n_prompt_tiles):
    refs = list(refs)
    i = pl.program_id(0)
    x_refs = [refs.pop(0) for _ in range(2 if split_in else 1)]
    g_ref, wg_ref, wu_ref, wd_ref = refs[:4]
    refs = refs[4:]
    is_prompt = i < n_prompt_tiles
    for r0 in range(0, _FFN_TM, _FFN_SUB):
        rows = slice(r0, r0 + _FFN_SUB)
        if split_in:
            x = jnp.where(is_prompt, x_refs[0][rows, :], x_refs[1][rows, :])
        else:
            x = x_refs[0][rows, :]
        h = _rms(x, g_ref[...]).astype(_BF)
        acc = None
        for c0, cn in _FFN_CHUNKS:
            a = _dot(h, wg_ref[:, c0:c0 + cn])
            b = _dot(h, wu_ref[:, c0:c0 + cn])
            act = (a * jax.nn.sigmoid(a) * b).astype(_BF)
            d = _dot(act, wd_ref[c0:c0 + cn, :])
            acc = d if acc is None else acc + d
        y = x + 0.5 * acc
        if final:
            fg_ref, yp_ref, ys_ref = refs
            y = _rms(y, fg_ref[...])

            @pl.when(is_prompt)
            def _():
                yp_ref[rows, :] = y

            @pl.when(jnp.logical_not(is_prompt))
            def _():
                ys_ref[rows, :] = y
        else:
            (o_ref,) = refs
            o_ref[rows, :] = y


def _ffn(xs, g, wg, wu, wd, *, n_prompt, n_sample, final_g=None, casts=()):
    d, dff = wg.shape
    npt, nst = n_prompt // _FFN_TM, n_sample // _FFN_TM
    split_in = len(xs) == 2
    final = final_g is not None
    row = lambda i: (i, 0)
    prow = lambda i: (jnp.minimum(i, npt - 1), 0)
    srow = lambda i: (jnp.maximum(i - npt, 0), 0)
    tile = (_FFN_TM, d)
    in_specs = [pl.BlockSpec(tile, prow), pl.BlockSpec(tile, srow)] if split_in else [pl.BlockSpec(tile, row)]
    in_specs += [_resident((1, d)), _resident((d, dff)), _resident((d, dff)), _resident((dff, d))]
    args = list(xs) + [g.reshape(1, d), wg, wu, wd]
    if final:
        in_specs.append(_resident((1, d)))
        args.append(final_g.reshape(1, d))
        out_specs = [pl.BlockSpec(tile, prow), pl.BlockSpec(tile, srow)]
        out_shape = [jax.ShapeDtypeStruct((n_prompt, d), _F32), jax.ShapeDtypeStruct((n_sample, d), _F32)]
    else:
        out_specs = [pl.BlockSpec(tile, row)]
        out_shape = [jax.ShapeDtypeStruct((n_prompt + n_sample, d), _F32)]
    return _call(
        functools.partial(_ffn_body, split_in=split_in, final=final, n_prompt_tiles=npt),
        name="ffn_final" if final else ("ffn_first" if split_in else "ffn"),
        grid=npt + nst, in_specs=in_specs, args=args, out_specs=out_specs, out_shape=out_shape, casts=casts)


def _gelu(x):
    return 0.5 * x * (1.0 + lax.erf(x * (2.0 ** -0.5)))


def _even_body(*refs, sample, chunk_len, tiles_per_seq):
    if sample:
        (x_ref, g_ref, win_ref, lng_ref, lnb_ref, sw_ref, sb_ref, cw_ref, wout_ref, b1_ref, b2_ref,
         o_ref, xg_ref, v_ref, yab_ref) = refs
    else:
        (x_ref, g_ref, win_ref, lng_ref, lnb_ref, sw_ref, sb_ref, cw_ref, wout_ref,
         o_ref, cs_ref, yab_ref, carry_ref) = refs

        @pl.when((pl.program_id(0) % tiles_per_seq) == 0)
        def _():
            carry_ref[...] = jnp.zeros_like(carry_ref)

    tm = x_ref.shape[0]
    d_a = _NH_A * _HD_A
    d_b = cw_ref.shape[1]
    x = x_ref[...]
    h = _rms(x, g_ref[...]).astype(_BF)
    z = _dot(h, win_ref[...])
    u = _gelu(z[:, 0:d_a])
    vv = _gelu(z[:, d_a:2 * d_a])
    mu = jnp.mean(vv, axis=-1, keepdims=True)
    vc = vv - mu
    var = jnp.mean(vc * vc, axis=-1, keepdims=True)
    v = vc * lax.rsqrt(var + _EPS) * lng_ref[...] + lnb_ref[...]
    if sample:
        v_ref[...] = v
    o = 2 * d_a
    gate_b = z[:, o:o + d_b]
    gate_c = z[:, o + d_b:o + 2 * d_b]
    x_in = z[:, o + 2 * d_b:o + 3 * d_b]

    tt = lax.broadcasted_iota(jnp.int32, (_CHUNK, _CHUNK), 0)
    ss = lax.broadcasted_iota(jnp.int32, (_CHUNK, _CHUNK), 1)
    keep = (ss <= tt) & ((tt // chunk_len) == (ss // chunk_len))
    vb = v.astype(_BF)
    for hd in range(_NH_A):
        w = jnp.where(keep, sw_ref[hd], 0.0).astype(_BF)
        bias = sb_ref[hd]
        cols = slice(hd * _HD_A, (hd + 1) * _HD_A)
        for c in range(tm // _CHUNK):
            rows = slice(c * _CHUNK, (c + 1) * _CHUNK)
            mixed = _dot(w, vb[rows, cols]) + bias
            yab_ref[rows, cols] = (u[rows, cols] * mixed).astype(_BF)

    xg = gate_c * x_in
    cw = cw_ref[...]
    if sample:
        t8 = lax.broadcasted_iota(jnp.int32, (tm, 1), 0) % chunk_len
        back1 = jnp.where(t8 >= 1, pltpu.roll(xg, 1, axis=0), b1_ref[...])
        back2 = jnp.where(t8 >= 2, pltpu.roll(xg, 2, axis=0), b2_ref[...])
        xg_ref[...] = xg
    else:
        ext = jnp.concatenate([carry_ref[...], xg], axis=0)
        back1 = pltpu.roll(ext, 1, axis=0)[_SUBLANES:]
        back2 = pltpu.roll(ext, 2, axis=0)[_SUBLANES:]
        carry_ref[...] = xg[tm - _SUBLANES:]
        cs_ref[0] = xg[tm - _SUBLANES:]
    conv = back2 * cw[0:1] + back1 * cw[1:2] + xg * cw[2:3]
    yab_ref[:, d_a:d_a + d_b] = (gate_b * conv).astype(_BF)

    o_ref[...] = x + _dot(yab_ref[...], wout_ref[...])


def _even_mixer(xc, g, w_in, ln_g, ln_b, sgu_w, sgu_b, conv_w, w_out, *, row0, n_rows, seq_len,
                chunk_len, conv_state=None, casts=()):
    sample = conv_state is not None
    t, d = xc.shape
    d_in = w_in.shape[1]
    d_a = _NH_A * _HD_A
    d_b = conv_w.shape[1]
    tile0 = row0 // _TM
    n_tiles = n_rows // _TM
    row = lambda i: (tile0 + i, 0)
    reps = _CHUNK // chunk_len
    sw = jnp.tile(sgu_w[:, :chunk_len, :chunk_len], (1, reps, reps))
    sb = jnp.broadcast_to(jnp.tile(sgu_b[:, :chunk_len], (1, reps))[:, :, None], (_NH_A, _CHUNK, _HD_A))
    in_specs = [pl.BlockSpec((_TM, d), row), _resident((1, d)), _resident((d, d_in)), _resident((1, d_a)),
                _resident((1, d_a)), _resident(sw.shape), _resident(sb.shape), _resident(conv_w.shape),
                _resident(w_out.shape)]
    args = [xc, g.reshape(1, d), w_in, ln_g.reshape(1, d_a), ln_b.reshape(1, d_a), sw, sb, conv_w, w_out]
    scratch = [pltpu.VMEM((_TM, d_a + d_b), _BF)]
    if sample:
        n_seq = n_rows // chunk_len
        pad = lambda a: jnp.pad(a, ((0, 0), (0, chunk_len - a.shape[1]), (0, 0))).reshape(n_rows, d_b)
        b1 = pad(conv_state[:, 1:2])
        b2 = pad(conv_state)
        local = lambda i: (i, 0)
        in_specs += [pl.BlockSpec((_TM, d_b), local), pl.BlockSpec((_TM, d_b), local)]
        args += [b1, b2]
        out_specs = [pl.BlockSpec((_TM, d), row), pl.BlockSpec((_TM, d_b), local), pl.BlockSpec((_TM, d_a), local)]
        out_shape = [jax.ShapeDtypeStruct((t, d), _F32), jax.ShapeDtypeStruct((n_rows, d_b), _F32),
                     jax.ShapeDtypeStruct((n_rows, d_a), _F32)]
        tiles_per_seq = 1
    else:
        n_seq = n_rows // seq_len
        tiles_per_seq = seq_len // _TM
        out_specs = [pl.BlockSpec((_TM, d), row),
                     pl.BlockSpec((1, _SUBLANES, d_b), lambda i: (i // tiles_per_seq, 0, 0))]
        out_shape = [jax.ShapeDtypeStruct((t, d), _F32), jax.ShapeDtypeStruct((n_seq, _SUBLANES, d_b), _F32)]
        scratch.append(pltpu.VMEM((_SUBLANES, d_b), _F32))
    return _call(
        functools.partial(_even_body, sample=sample, chunk_len=chunk_len, tiles_per_seq=tiles_per_seq),
        name="even_sample" if sample else "even_prompt",
        grid=n_tiles, in_specs=in_specs, args=args, out_specs=out_specs, out_shape=out_shape, scratch=scratch,
        casts=casts, aliases={0: 0})


def _window_sums(ext):
    gc = ext.shape[1] // len(_POOL_WINDOWS)
    sums = []
    s = ext
    width = 1
    for w in _POOL_WINDOWS:
        while width < w:
            s = s + pltpu.roll(s, width, axis=0)
            width *= 2
        sums.append(s[:, 0:gc])
        s = s[:, gc:]
    return sums


def _odd_body(*refs, sample, seq_len, tiles_per_seq):
    i = pl.program_id(0)
    if sample:
        (x_ref, g_ref, win_ref, pw_ref, psc_ref, wout_ref, ctx_ref, o_ref, p_ref, y_ref, ext_ref) = refs
    else:
        (x_ref, g_ref, win_ref, pw_ref, psc_ref, wout_ref, o_ref, ps_ref, y_ref, carry_ref) = refs

        @pl.when((i % tiles_per_seq) == 0)
        def _():
            carry_ref[...] = jnp.zeros_like(carry_ref)

    tm, d = x_ref.shape
    ctx_rows = 2 * _SUBLANES
    x = x_ref[...]
    h = _rms(x, g_ref[...]).astype(_BF)
    p = _dot(h, win_ref[...])
    t_in_tile = lax.broadcasted_iota(jnp.int32, (tm, 1), 0)
    if sample:
        nb = tm // seq_len
        ext_ref[:, 0:ctx_rows, :] = ctx_ref[...]
        ext_ref[:, ctx_rows:ctx_rows + seq_len, :] = p.reshape(nb, seq_len, d)
        ext = ext_ref[...].reshape(nb * (ctx_rows + seq_len), d)
        pick = lambda s: s.reshape(nb, ctx_rows + seq_len, s.shape[1])[:, ctx_rows:, :].reshape(tm, s.shape[1])
        pos = _PAST_LEN + t_in_tile % seq_len
        p_ref[...] = p
    else:
        ext = jnp.concatenate([carry_ref[...], p], axis=0)
        pick = lambda s: s[ctx_rows:]
        pos = (i % tiles_per_seq) * tm + t_in_tile
        carry_ref[...] = p[tm - ctx_rows:]
        ps_ref[0] = p[tm - ctx_rows:]
    gc = d // len(_POOL_WINDOWS)
    for grp, (w, s) in enumerate(zip(_POOL_WINDOWS, _window_sums(ext))):
        cnt = jnp.minimum(pos + 1, w).astype(_F32)
        cols = slice(grp * gc, (grp + 1) * gc)
        dlt = pick(s) / cnt - p[:, cols]
        yg = _dot(dlt.astype(_BF), pw_ref[grp])
        y_ref[:, cols] = (yg * psc_ref[:, cols]).astype(_BF)
    o_ref[...] = x + _dot(y_ref[...], wout_ref[...])


def _odd_mixer(xc, g, w_in, pool_w, pool_scale, w_out, *, row0, n_rows, seq_len, pool_state=None):
    sample = pool_state is not None
    t, d = xc.shape
    ctx_rows = 2 * _SUBLANES
    tm = _TM // 2 if sample else _TM
    tile0 = row0 // tm
    n_tiles = n_rows // tm
    row = lambda i: (tile0 + i, 0)
    in_specs = [pl.BlockSpec((tm, d), row), _resident((1, d)), _resident((d, d)), _resident(pool_w.shape),
                _resident((1, d)), _resident((d, d))]
    args = [xc, g.reshape(1, d), w_in, pool_w, pool_scale.reshape(1, d), w_out]
    scratch = [pltpu.VMEM((tm, d), _BF)]
    if sample:
        nb = tm // seq_len
        ctx = jnp.pad(pool_state, ((0, 0), (ctx_rows - _POOL_CTX, 0), (0, 0)))
        in_specs.append(pl.BlockSpec((nb, ctx_rows, d), lambda i: (i, 0, 0)))
        args.append(ctx)
        out_specs = [pl.BlockSpec((tm, d), row), pl.BlockSpec((tm, d), lambda i: (i, 0))]
        out_shape = [jax.ShapeDtypeStruct((t, d), _F32), jax.ShapeDtypeStruct((n_rows, d), _F32)]
        scratch.append(pltpu.VMEM((nb, ctx_rows + seq_len, d), _F32))
        tiles_per_seq = 1
    else:
        n_seq = n_rows // seq_len
        tiles_per_seq = seq_len // tm
        out_specs = [pl.BlockSpec((tm, d), row),
                     pl.BlockSpec((1, ctx_rows, d), lambda i: (i // tiles_per_seq, 0, 0))]
        out_shape = [jax.ShapeDtypeStruct((t, d), _F32), jax.ShapeDtypeStruct((n_seq, ctx_rows, d), _F32)]
        scratch.append(pltpu.VMEM((ctx_rows, d), _F32))
    outs, _ = _call(
        functools.partial(_odd_body, sample=sample, seq_len=seq_len, tiles_per_seq=tiles_per_seq),
        name="odd_sample" if sample else "odd_prompt",
        grid=n_tiles, in_specs=in_specs, args=args, out_specs=out_specs, out_shape=out_shape, scratch=scratch,
        aliases={0: 0})
    return outs


def kernel(x_prompt, x_sample, state_conv, state_pool, norm_g, final_norm_g, ffn_w_gate, ffn_w_up, ffn_w_down,
           e_w_in, e_ln_g, e_ln_b, e_sgu_w, e_sgu_b, e_conv_w, e_w_out, o_w_in, o_pool_w, o_pool_scale, o_w_out):
    batch, seq, d = x_prompt.shape
    dec_batch, dec_seq, _ = x_sample.shape
    n_prompt, n_sample = batch * seq, dec_batch * dec_seq
    d_a = _NH_A * _HD_A
    d_b = e_conv_w.shape[-1]
    n_groups, gc = o_pool_w.shape[1:3]
    ffn = functools.partial(_ffn, n_prompt=n_prompt, n_sample=n_sample)
    ffn_casts = lambda layer, k: [_Cast(w, (layer, k)) for w in (ffn_w_gate, ffn_w_up, ffn_w_down)]

    w00 = [w[0, 0].astype(_BF) for w in (ffn_w_gate, ffn_w_up, ffn_w_down)]
    mixer_casts = [_Cast(e_w_in, (0,)), _Cast(e_w_out, (0,)), _Cast(o_w_in, (0,)), _Cast(o_w_out, (0,)),
                   _Cast(o_pool_w.reshape(-1, n_groups * gc, gc), (0,))]
    (xc,), (e_win, e_wout, o_win, o_wout, o_pw) = ffn(
        (x_prompt.reshape(n_prompt, d), x_sample.reshape(n_sample, d)), norm_g[0, 0], *w00, casts=mixer_casts)
    even = functools.partial(_even_mixer, g=norm_g[0, 1], w_in=e_win, ln_g=e_ln_g[0], ln_b=e_ln_b[0],
                             sgu_w=e_sgu_w[0], sgu_b=e_sgu_b[0], conv_w=e_conv_w[0], w_out=e_wout)
    (xc, conv_p), w01 = even(xc, row0=0, n_rows=n_prompt, seq_len=seq, chunk_len=_CHUNK, casts=ffn_casts(0, 1))
    (xc, xg_s, v_s), _ = even(xc, row0=n_prompt, n_rows=n_sample, seq_len=dec_seq, chunk_len=dec_seq,
                              conv_state=state_conv[0])
    (xc,), w10 = ffn((xc,), norm_g[0, 2], *w01, casts=ffn_casts(1, 0))

    (xc,), w11 = ffn((xc,), norm_g[1, 0], *w10, casts=ffn_casts(1, 1))
    odd = functools.partial(_odd_mixer, g=norm_g[1, 1], w_in=o_win, pool_w=o_pw.reshape(n_groups, gc, gc),
                            pool_scale=o_pool_scale[0], w_out=o_wout)
    xc, pool_p = odd(xc, row0=0, n_rows=n_prompt, seq_len=seq)
    xc, p_s = odd(xc, row0=n_prompt, n_rows=n_sample, seq_len=dec_seq, pool_state=state_pool[0])
    (y_p, y_s), _ = ffn((xc,), norm_g[1, 2], *w11, final_g=final_norm_g)

    keep = _CONV_W - 1
    conv_prompt = conv_p[:, _SUBLANES - keep:][None]
    conv_sample = xg_s.reshape(dec_batch, dec_seq, d_b)[:, dec_seq - keep:][None]
    chunk_v_sample = v_s.reshape(1, dec_batch, dec_seq, d_a)
    pool_prompt = pool_p[:, pool_p.shape[1] - _POOL_CTX:][None]
    pool_sample = jnp.concatenate([state_pool[0], p_s.reshape(dec_batch, dec_seq, d)], axis=1)[:, -_POOL_CTX:][None]
    return (y_p.reshape(batch, seq, d), y_s.reshape(dec_batch, dec_seq, d), conv_prompt, conv_sample,
            chunk_v_sample, pool_prompt, pool_sample)
```

```python
import collections
import functools

import jax
import jax.numpy as jnp
from jax import lax
from jax.experimental import pallas as pl
from jax.experimental.pallas import tpu as pltpu

_CHUNK = 128
_NH_A = 4
_HD_A = 128
_CONV_W = 3
_POOL_WINDOWS = (2, 4, 8, 16)
_POOL_CTX = max(_POOL_WINDOWS) - 1
_PAST_LEN = 16384
_EPS = 1e-6

_SUBLANES = 8
_BF16_ROWS = 16
_VMEM_LIMIT = 56 * 1024 * 1024

_TM = 1024
_SUB = 512
_FFN_CHUNKS = ((0, 1536), (1536, 1280))

_BF = jnp.bfloat16
_F32 = jnp.float32


def _rms(x, g):
    return x * lax.rsqrt(jnp.mean(x * x, axis=-1, keepdims=True) + _EPS) * g


def _dot(a, b):
    return jnp.dot(a, b, preferred_element_type=_F32)


def _resident(shape):
    nd = len(shape)
    return pl.BlockSpec(shape, lambda i: (0,) * nd, pipeline_mode=pl.Buffered(1))


_Cast = collections.namedtuple("_Cast", "array lead")


def _cast_specs(cast, grid):
    lead = tuple(cast.lead)
    rows, cols = cast.array.shape[len(lead):]
    steps = max(s for s in (1, 2, 4, 8, 16, 32) if s <= grid and rows % (s * _BF16_ROWS) == 0)
    blk = rows // steps
    step = lambda i: jnp.minimum(i, steps - 1)
    in_spec = pl.BlockSpec((None,) * len(lead) + (blk, cols), lambda i: lead + (step(i), 0))
    out_spec = pl.BlockSpec((blk, cols), lambda i: (step(i), 0))
    return in_spec, out_spec, jax.ShapeDtypeStruct((rows, cols), _BF)


def _call(body, *, name, grid, in_specs, args, out_specs, out_shape, scratch=(), casts=(), aliases=None):
    n_in, n_out, n_cast = len(in_specs), len(out_specs), len(casts)
    specs = [_cast_specs(c, grid) for c in casts]

    def wrapped(*refs):
        ins, refs = refs[:n_in], refs[n_in:]
        cast_in, refs = refs[:n_cast], refs[n_cast:]
        outs, refs = refs[:n_out], refs[n_out:]
        cast_out, scratch_refs = refs[:n_cast], refs[n_cast:]
        for src, dst in zip(cast_in, cast_out):
            dst[...] = src[...].astype(_BF)
        body(*ins, *outs, *scratch_refs)

    res = pl.pallas_call(
        wrapped,
        grid=(grid,),
        in_specs=list(in_specs) + [s[0] for s in specs],
        out_specs=list(out_specs) + [s[1] for s in specs],
        out_shape=list(out_shape) + [s[2] for s in specs],
        scratch_shapes=list(scratch),
        input_output_aliases=aliases or {},
        compiler_params=pltpu.CompilerParams(dimension_semantics=("arbitrary",), vmem_limit_bytes=_VMEM_LIMIT),
        name=name,
    )(*args, *[c.array for c in casts])
    return res[:n_out], res[n_out:]


def _ffn_body(*refs, split_in, final, n_prompt_tiles):
    refs = list(refs)
    i = pl.program_id(0)
    x_refs = [refs.pop(0) for _ in range(2 if split_in else 1)]
    g_ref, wg_ref, wu_ref, wd_ref = refs[:4]
    refs = refs[4:]
    is_prompt = i < n_prompt_tiles
    for r0 in range(0, _TM, _SUB):
        rows = slice(r0, r0 + _SUB)
        if split_in:
            x = jnp.where(is_prompt, x_refs[0][rows, :], x_refs[1][rows, :])
        else:
            x = x_refs[0][rows, :]
        h = _rms(x, g_ref[...]).astype(_BF)
        acc = None
        for c0, cn in _FFN_CHUNKS:
            a = _dot(h, wg_ref[:, c0:c0 + cn])
            b = _dot(h, wu_ref[:, c0:c0 + cn])
            act = (a * jax.nn.sigmoid(a) * b).astype(_BF)
            d = _dot(act, wd_ref[c0:c0 + cn, :])
            acc = d if acc is None else acc + d
        y = x + 0.5 * acc
        if final:
            fg_ref, yp_ref, ys_ref = refs
            y = _rms(y, fg_ref[...])

            @pl.when(is_prompt)
            def _():
                yp_ref[rows, :] = y

            @pl.when(jnp.logical_not(is_prompt))
            def _():
                ys_ref[rows, :] = y
        else:
            (o_ref,) = refs
            o_ref[rows, :] = y


def _ffn(xs, g, wg, wu, wd, *, n_prompt, n_sample, final_g=None, casts=()):
    d, dff = wg.shape
    npt, nst = n_prompt // _TM, n_sample // _TM
    split_in = len(xs) == 2
    final = final_g is not None
    row = lambda i: (i, 0)
    prow = lambda i: (jnp.minimum(i, npt - 1), 0)
    srow = lambda i: (jnp.maximum(i - npt, 0), 0)
    tile = (_TM, d)
    if split_in:
        in_specs = [pl.BlockSpec(tile, prow), pl.BlockSpec(tile, srow, pipeline_mode=pl.Buffered(1))]
    else:
        in_specs = [pl.BlockSpec(tile, row)]
    in_specs += [_resident((1, d)), _resident((d, dff)), _resident((d, dff)), _resident((dff, d))]
    args = list(xs) + [g.reshape(1, d), wg, wu, wd]
    if final:
        in_specs.append(_resident((1, d)))
        args.append(final_g.reshape(1, d))
        out_specs = [pl.BlockSpec(tile, prow), pl.BlockSpec(tile, srow)]
        out_shape = [jax.ShapeDtypeStruct((n_prompt, d), _F32), jax.ShapeDtypeStruct((n_sample, d), _F32)]
    else:
        out_specs = [pl.BlockSpec(tile, row)]
        out_shape = [jax.ShapeDtypeStruct((n_prompt + n_sample, d), _F32)]
    return _call(
        functools.partial(_ffn_body, split_in=split_in, final=final, n_prompt_tiles=npt),
        name="ffn_final" if final else ("ffn_first" if split_in else "ffn"),
        grid=npt + nst, in_specs=in_specs, args=args, out_specs=out_specs, out_shape=out_shape, casts=casts)


def _gelu(x):
    return 0.5 * x * (1.0 + lax.erf(x * (2.0 ** -0.5)))


def _even_body(*refs, sample, chunk_len, tiles_per_seq, sub):
    if sample:
        (x_ref, g_ref, win_ref, lng_ref, lnb_ref, sw_ref, sb_ref, cw_ref, wout_ref, b1_ref, b2_ref,
         o_ref, xg_ref, v_ref, yab_ref) = refs
    else:
        (x_ref, g_ref, win_ref, lng_ref, lnb_ref, sw_ref, sb_ref, cw_ref, wout_ref,
         o_ref, cs_ref, yab_ref, carry_ref) = refs

        @pl.when((pl.program_id(0) % tiles_per_seq) == 0)
        def _():
            carry_ref[...] = jnp.zeros_like(carry_ref)

    tm = x_ref.shape[0]
    d_a = _NH_A * _HD_A
    d_b = cw_ref.shape[1]
    o = 2 * d_a
    tt = lax.broadcasted_iota(jnp.int32, (_CHUNK, _CHUNK), 0)
    ss = lax.broadcasted_iota(jnp.int32, (_CHUNK, _CHUNK), 1)
    keep = (ss <= tt) & ((tt // chunk_len) == (ss // chunk_len))
    mix_w = [jnp.where(keep, sw_ref[hd], 0.0).astype(_BF) for hd in range(_NH_A)]
    cw = cw_ref[...]
    prev = None if sample else carry_ref[...]
    for r0 in range(0, tm, sub):
        rows = slice(r0, r0 + sub)
        x = x_ref[rows, :]
        h = _rms(x, g_ref[...]).astype(_BF)
        z = _dot(h, win_ref[...])
        u = _gelu(z[:, 0:d_a])
        vv = _gelu(z[:, d_a:2 * d_a])
        mu = jnp.mean(vv, axis=-1, keepdims=True)
        vc = vv - mu
        var = jnp.mean(vc * vc, axis=-1, keepdims=True)
        v = vc * lax.rsqrt(var + _EPS) * lng_ref[...] + lnb_ref[...]
        if sample:
            v_ref[rows, :] = v
        gate_b = z[:, o:o + d_b]
        gate_c = z[:, o + d_b:o + 2 * d_b]
        x_in = z[:, o + 2 * d_b:o + 3 * d_b]

        vb = v.astype(_BF)
        for hd in range(_NH_A):
            cols = slice(hd * _HD_A, (hd + 1) * _HD_A)
            for c0 in range(0, sub, _CHUNK):
                mixed = _dot(mix_w[hd], vb[c0:c0 + _CHUNK, cols]) + sb_ref[hd]
                yab_ref[r0 + c0:r0 + c0 + _CHUNK, cols] = (u[c0:c0 + _CHUNK, cols] * mixed).astype(_BF)

        xg = gate_c * x_in
        if sample:
            t8 = lax.broadcasted_iota(jnp.int32, (sub, 1), 0) % chunk_len
            back1 = jnp.where(t8 >= 1, pltpu.roll(xg, 1, axis=0), b1_ref[rows, :])
            back2 = jnp.where(t8 >= 2, pltpu.roll(xg, 2, axis=0), b2_ref[rows, :])
            xg_ref[rows, :] = xg
        else:
            ext = jnp.concatenate([prev, xg], axis=0)
            back1 = pltpu.roll(ext, 1, axis=0)[_SUBLANES:]
            back2 = pltpu.roll(ext, 2, axis=0)[_SUBLANES:]
            prev = xg[sub - _SUBLANES:]
        conv = back2 * cw[0:1] + back1 * cw[1:2] + xg * cw[2:3]
        yab_ref[rows, d_a:d_a + d_b] = (gate_b * conv).astype(_BF)
        o_ref[rows, :] = x + _dot(yab_ref[rows, :], wout_ref[...])
    if not sample:
        carry_ref[...] = prev
        cs_ref[0] = prev


def _even_mixer(xc, g, w_in, ln_g, ln_b, sgu_w, sgu_b, conv_w, w_out, *, row0, n_rows, seq_len,
                chunk_len, conv_state=None, casts=()):
    sample = conv_state is not None
    t, d = xc.shape
    d_in = w_in.shape[1]
    d_a = _NH_A * _HD_A
    d_b = conv_w.shape[1]
    tile0 = row0 // _TM
    n_tiles = n_rows // _TM
    row = lambda i: (tile0 + i, 0)
    reps = _CHUNK // chunk_len
    sw = jnp.tile(sgu_w[:, :chunk_len, :chunk_len], (1, reps, reps))
    sb = jnp.broadcast_to(jnp.tile(sgu_b[:, :chunk_len], (1, reps))[:, :, None], (_NH_A, _CHUNK, _HD_A))
    in_specs = [pl.BlockSpec((_TM, d), row), _resident((1, d)), _resident((d, d_in)), _resident((1, d_a)),
                _resident((1, d_a)), _resident(sw.shape), _resident(sb.shape), _resident(conv_w.shape),
                _resident(w_out.shape)]
    args = [xc, g.reshape(1, d), w_in, ln_g.reshape(1, d_a), ln_b.reshape(1, d_a), sw, sb, conv_w, w_out]
    scratch = [pltpu.VMEM((_TM, d_a + d_b), _BF)]
    if sample:
        n_seq = n_rows // chunk_len
        pad = lambda a: jnp.pad(a, ((0, 0), (0, chunk_len - a.shape[1]), (0, 0))).reshape(n_rows, d_b)
        b1 = pad(conv_state[:, 1:2])
        b2 = pad(conv_state)
        local = lambda i: (i, 0)
        in_specs += [pl.BlockSpec((_TM, d_b), local), pl.BlockSpec((_TM, d_b), local)]
        args += [b1, b2]
        out_specs = [pl.BlockSpec((_TM, d), row), pl.BlockSpec((_TM, d_b), local), pl.BlockSpec((_TM, d_a), local)]
        out_shape = [jax.ShapeDtypeStruct((t, d), _F32), jax.ShapeDtypeStruct((n_rows, d_b), _F32),
                     jax.ShapeDtypeStruct((n_rows, d_a), _F32)]
        tiles_per_seq = 1
    else:
        n_seq = n_rows // seq_len
        tiles_per_seq = seq_len // _TM
        out_specs = [pl.BlockSpec((_TM, d), row),
                     pl.BlockSpec((1, _SUBLANES, d_b), lambda i: (i // tiles_per_seq, 0, 0))]
        out_shape = [jax.ShapeDtypeStruct((t, d), _F32), jax.ShapeDtypeStruct((n_seq, _SUBLANES, d_b), _F32)]
        scratch.append(pltpu.VMEM((_SUBLANES, d_b), _F32))
    return _call(
        functools.partial(_even_body, sample=sample, chunk_len=chunk_len, tiles_per_seq=tiles_per_seq, sub=_SUB),
        name="even_sample" if sample else "even_prompt",
        grid=n_tiles, in_specs=in_specs, args=args, out_specs=out_specs, out_shape=out_shape, scratch=scratch,
        casts=casts, aliases={0: 0})


def _window_sums(ext):
    gc = ext.shape[1] // len(_POOL_WINDOWS)
    sums = []
    s = ext
    width = 1
    for w in _POOL_WINDOWS:
        while width < w:
            s = s + pltpu.roll(s, width, axis=0)
            width *= 2
        sums.append(s[:, 0:gc])
        s = s[:, gc:]
    return sums


def _odd_body(*refs, sample, seq_len, tiles_per_seq, sub):
    i = pl.program_id(0)
    if sample:
        (x_ref, g_ref, win_ref, pw_ref, psc_ref, wout_ref, ctx_ref, o_ref, p_ref, y_ref, ext_ref) = refs
    else:
        (x_ref, g_ref, win_ref, pw_ref, psc_ref, wout_ref, o_ref, ps_ref, y_ref, carry_ref) = refs

        @pl.when((i % tiles_per_seq) == 0)
        def _():
            carry_ref[...] = jnp.zeros_like(carry_ref)

    tm, d = x_ref.shape
    ctx_rows = 2 * _SUBLANES
    gc = d // len(_POOL_WINDOWS)
    prev = None if sample else carry_ref[...]
    for r0 in range(0, tm, sub):
        rows = slice(r0, r0 + sub)
        x = x_ref[rows, :]
        h = _rms(x, g_ref[...]).astype(_BF)
        p = _dot(h, win_ref[...])
        t_in_tile = r0 + lax.broadcasted_iota(jnp.int32, (sub, 1), 0)
        if sample:
            nb = sub // seq_len
            b0 = r0 // seq_len
            ext_ref[b0:b0 + nb, 0:ctx_rows, :] = ctx_ref[b0:b0 + nb]
            ext_ref[b0:b0 + nb, ctx_rows:ctx_rows + seq_len, :] = p.reshape(nb, seq_len, d)
            ext = ext_ref[b0:b0 + nb].reshape(nb * (ctx_rows + seq_len), d)
            pick = lambda s: s.reshape(nb, ctx_rows + seq_len, s.shape[1])[:, ctx_rows:, :].reshape(sub, s.shape[1])
            pos = _PAST_LEN + t_in_tile % seq_len
            p_ref[rows, :] = p
        else:
            ext = jnp.concatenate([prev, p], axis=0)
            pick = lambda s: s[ctx_rows:]
            pos = (i % tiles_per_seq) * tm + t_in_tile
            prev = p[sub - ctx_rows:]
        for grp, (w, s) in enumerate(zip(_POOL_WINDOWS, _window_sums(ext))):
            cnt = jnp.minimum(pos + 1, w).astype(_F32)
            cols = slice(grp * gc, (grp + 1) * gc)
            dlt = pick(s) / cnt - p[:, cols]
            yg = _dot(dlt.astype(_BF), pw_ref[grp])
            y_ref[rows, cols] = (yg * psc_ref[:, cols]).astype(_BF)
        o_ref[rows, :] = x + _dot(y_ref[rows, :], wout_ref[...])
    if not sample:
        carry_ref[...] = prev
        ps_ref[0] = prev


def _odd_mixer(xc, g, w_in, pool_w, pool_scale, w_out, *, row0, n_rows, seq_len, pool_state=None):
    sample = pool_state is not None
    t, d = xc.shape
    ctx_rows = 2 * _SUBLANES
    tm = _TM // 4 if sample else _TM
    tile0 = row0 // tm
    n_tiles = n_rows // tm
    row = lambda i: (tile0 + i, 0)
    in_specs = [pl.BlockSpec((tm, d), row), _resident((1, d)), _resident((d, d)), _resident(pool_w.shape),
                _resident((1, d)), _resident((d, d))]
    args = [xc, g.reshape(1, d), w_in, pool_w, pool_scale.reshape(1, d), w_out]
    scratch = [pltpu.VMEM((tm, d), _BF)]
    if sample:
        nb = tm // seq_len
        ctx = jnp.pad(pool_state, ((0, 0), (ctx_rows - _POOL_CTX, 0), (0, 0)))
        in_specs.append(pl.BlockSpec((nb, ctx_rows, d), lambda i: (i, 0, 0)))
        args.append(ctx)
        out_specs = [pl.BlockSpec((tm, d), row), pl.BlockSpec((tm, d), lambda i: (i, 0))]
        out_shape = [jax.ShapeDtypeStruct((t, d), _F32), jax.ShapeDtypeStruct((n_rows, d), _F32)]
        scratch.append(pltpu.VMEM((nb, ctx_rows + seq_len, d), _F32))
        tiles_per_seq = 1
    else:
        n_seq = n_rows // seq_len
        tiles_per_seq = seq_len // tm
        out_specs = [pl.BlockSpec((tm, d), row),
                     pl.BlockSpec((1, ctx_rows, d), lambda i: (i // tiles_per_seq, 0, 0))]
        out_shape = [jax.ShapeDtypeStruct((t, d), _F32), jax.ShapeDtypeStruct((n_seq, ctx_rows, d), _F32)]
        scratch.append(pltpu.VMEM((ctx_rows, d), _F32))
    outs, _ = _call(
        functools.partial(_odd_body, sample=sample, seq_len=seq_len, tiles_per_seq=tiles_per_seq, sub=min(tm, _SUB)),
        name="odd_sample" if sample else "odd_prompt",
        grid=n_tiles, in_specs=in_specs, args=args, out_specs=out_specs, out_shape=out_shape, scratch=scratch,
        aliases={0: 0})
    return outs


def kernel(x_prompt, x_sample, state_conv, state_pool, norm_g, final_norm_g, ffn_w_gate, ffn_w_up, ffn_w_down,
           e_w_in, e_ln_g, e_ln_b, e_sgu_w, e_sgu_b, e_conv_w, e_w_out, o_w_in, o_pool_w, o_pool_scale, o_w_out):
    batch, seq, d = x_prompt.shape
    dec_batch, dec_seq, _ = x_sample.shape
    n_prompt, n_sample = batch * seq, dec_batch * dec_seq
    d_a = _NH_A * _HD_A
    d_b = e_conv_w.shape[-1]
    n_groups, gc = o_pool_w.shape[1:3]
    ffn = functools.partial(_ffn, n_prompt=n_prompt, n_sample=n_sample)
    ffn_casts = lambda layer, k: [_Cast(w, (layer, k)) for w in (ffn_w_gate, ffn_w_up, ffn_w_down)]

    w00 = [w[0, 0].astype(_BF) for w in (ffn_w_gate, ffn_w_up, ffn_w_down)]
    even_casts = [_Cast(e_w_in, (0,)), _Cast(e_w_out, (0,))]
    odd_casts = [_Cast(o_w_in, (0,)), _Cast(o_w_out, (0,)), _Cast(o_pool_w.reshape(-1, n_groups * gc, gc), (0,))]
    (xc,), (e_win, e_wout, *w01) = ffn(
        (x_prompt.reshape(n_prompt, d), x_sample.reshape(n_sample, d)), norm_g[0, 0], *w00,
        casts=even_casts + ffn_casts(0, 1))
    even = functools.partial(_even_mixer, g=norm_g[0, 1], w_in=e_win, ln_g=e_ln_g[0], ln_b=e_ln_b[0],
                             sgu_w=e_sgu_w[0], sgu_b=e_sgu_b[0], conv_w=e_conv_w[0], w_out=e_wout)
    (xc, conv_p), _ = even(xc, row0=0, n_rows=n_prompt, seq_len=seq, chunk_len=_CHUNK)
    (xc, xg_s, v_s), _ = even(xc, row0=n_prompt, n_rows=n_sample, seq_len=dec_seq, chunk_len=dec_seq,
                              conv_state=state_conv[0])
    (xc,), (o_win, o_wout, o_pw, *w10) = ffn((xc,), norm_g[0, 2], *w01, casts=odd_casts + ffn_casts(1, 0))

    (xc,), w11 = ffn((xc,), norm_g[1, 0], *w10, casts=ffn_casts(1, 1))
    odd = functools.partial(_odd_mixer, g=norm_g[1, 1], w_in=o_win, pool_w=o_pw.reshape(n_groups, gc, gc),
                            pool_scale=o_pool_scale[0], w_out=o_wout)
    xc, pool_p = odd(xc, row0=0, n_rows=n_prompt, seq_len=seq)
    xc, p_s = odd(xc, row0=n_prompt, n_rows=n_sample, seq_len=dec_seq, pool_state=state_pool[0])
    (y_p, y_s), _ = ffn((xc,), norm_g[1, 2], *w11, final_g=final_norm_g)

    keep = _CONV_W - 1
    conv_prompt = conv_p[:, _SUBLANES - keep:][None]
    conv_sample = xg_s.reshape(dec_batch, dec_seq, d_b)[:, dec_seq - keep:][None]
    chunk_v_sample = v_s.reshape(1, dec_batch, dec_seq, d_a)
    pool_prompt = pool_p[:, pool_p.shape[1] - _POOL_CTX:][None]
    pool_sample = jnp.concatenate([state_pool[0], p_s.reshape(dec_batch, dec_seq, d)], axis=1)[:, -_POOL_CTX:][None]
    return (y_p.reshape(batch, seq, d), y_s.reshape(dec_batch, dec_seq, d), conv_prompt, conv_sample,
            chunk_v_sample, pool_prompt, pool_sample)
```

```python
import collections
import functools

import jax
import jax.numpy as jnp
from jax import lax
from jax.experimental import pallas as pl
from jax.experimental.pallas import tpu as pltpu

_CHUNK = 128
_NH_A = 4
_HD_A = 128
_CONV_W = 3
_POOL_WINDOWS = (2, 4, 8, 16)
_POOL_CTX = max(_POOL_WINDOWS) - 1
_PAST_LEN = 16384
_EPS = 1e-6

_SUBLANES = 8
_BF16_ROWS = 16
_VMEM_LIMIT = 60 * 1024 * 1024

_TM = 1024
_SUB = 512
_FFN_CHUNKS = ((0, 1536), (1536, 1280))

_BF = jnp.bfloat16
_F32 = jnp.float32


def _rms(x, g):
    return x * lax.rsqrt(jnp.mean(x * x, axis=-1, keepdims=True) + _EPS) * g


def _dot(a, b):
    return jnp.dot(a, b, preferred_element_type=_F32)


def _resident(shape):
    nd = len(shape)
    return pl.BlockSpec(shape, lambda i: (0,) * nd, pipeline_mode=pl.Buffered(1))


_Cast = collections.namedtuple("_Cast", "array lead")


def _cast_specs(cast, grid):
    lead = tuple(cast.lead)
    rows, cols = cast.array.shape[len(lead):]
    steps = max(s for s in (1, 2, 4, 8, 16, 32) if s <= grid and rows % (s * _BF16_ROWS) == 0)
    blk = rows // steps
    step = lambda i: jnp.minimum(i, steps - 1)
    in_spec = pl.BlockSpec((None,) * len(lead) + (blk, cols), lambda i: lead + (step(i), 0))
    out_spec = pl.BlockSpec((blk, cols), lambda i: (step(i), 0))
    return in_spec, out_spec, jax.ShapeDtypeStruct((rows, cols), _BF)


def _call(body, *, name, grid, in_specs, args, out_specs, out_shape, scratch=(), casts=(), aliases=None):
    n_in, n_out, n_cast = len(in_specs), len(out_specs), len(casts)
    specs = [_cast_specs(c, grid) for c in casts]

    def wrapped(*refs):
        ins, refs = refs[:n_in], refs[n_in:]
        cast_in, refs = refs[:n_cast], refs[n_cast:]
        outs, refs = refs[:n_out], refs[n_out:]
        cast_out, scratch_refs = refs[:n_cast], refs[n_cast:]
        for src, dst in zip(cast_in, cast_out):
            dst[...] = src[...].astype(_BF)
        body(*ins, *outs, *scratch_refs)

    res = pl.pallas_call(
        wrapped,
        grid=(grid,),
        in_specs=list(in_specs) + [s[0] for s in specs],
        out_specs=list(out_specs) + [s[1] for s in specs],
        out_shape=list(out_shape) + [s[2] for s in specs],
        scratch_shapes=list(scratch),
        input_output_aliases=aliases or {},
        compiler_params=pltpu.CompilerParams(dimension_semantics=("arbitrary",), vmem_limit_bytes=_VMEM_LIMIT),
        name=name,
    )(*args, *[c.array for c in casts])
    return res[:n_out], res[n_out:]


def _ffn_body(*refs, split_in, final, n_prompt_tiles):
    refs = list(refs)
    i = pl.program_id(0)
    x_refs = [refs.pop(0) for _ in range(2 if split_in else 1)]
    g_ref, wg_ref, wu_ref, wd_ref = refs[:4]
    refs = refs[4:]
    is_prompt = i < n_prompt_tiles
    for r0 in range(0, _TM, _SUB):
        rows = slice(r0, r0 + _SUB)
        if split_in:
            x = jnp.where(is_prompt, x_refs[0][rows, :], x_refs[1][rows, :])
        else:
            x = x_refs[0][rows, :]
        h = _rms(x, g_ref[...]).astype(_BF)
        acc = None
        for c0, cn in _FFN_CHUNKS:
            a = _dot(h, wg_ref[:, c0:c0 + cn])
            b = _dot(h, wu_ref[:, c0:c0 + cn])
            act = (a * jax.nn.sigmoid(a) * b).astype(_BF)
            d = _dot(act, wd_ref[c0:c0 + cn, :])
            acc = d if acc is None else acc + d
        y = x + 0.5 * acc
        if final:
            fg_ref, yp_ref, ys_ref = refs
            y = _rms(y, fg_ref[...])

            @pl.when(is_prompt)
            def _():
                yp_ref[rows, :] = y

            @pl.when(jnp.logical_not(is_prompt))
            def _():
                ys_ref[rows, :] = y
        else:
            (o_ref,) = refs
            o_ref[rows, :] = y


def _ffn(xs, g, wg, wu, wd, *, n_prompt, n_sample, final_g=None, casts=()):
    d, dff = wg.shape
    npt, nst = n_prompt // _TM, n_sample // _TM
    split_in = len(xs) == 2
    final = final_g is not None
    row = lambda i: (i, 0)
    prow = lambda i: (jnp.minimum(i, npt - 1), 0)
    srow = lambda i: (jnp.maximum(i - npt, 0), 0)
    tile = (_TM, d)
    in_specs = [pl.BlockSpec(tile, prow), pl.BlockSpec(tile, srow)] if split_in else [pl.BlockSpec(tile, row)]
    in_specs += [_resident((1, d)), _resident((d, dff)), _resident((d, dff)), _resident((dff, d))]
    args = list(xs) + [g.reshape(1, d), wg, wu, wd]
    if final:
        in_specs.append(_resident((1, d)))
        args.append(final_g.reshape(1, d))
        out_specs = [pl.BlockSpec(tile, prow), pl.BlockSpec(tile, srow)]
        out_shape = [jax.ShapeDtypeStruct((n_prompt, d), _F32), jax.ShapeDtypeStruct((n_sample, d), _F32)]
    else:
        out_specs = [pl.BlockSpec(tile, row)]
        out_shape = [jax.ShapeDtypeStruct((n_prompt + n_sample, d), _F32)]
    return _call(
        functools.partial(_ffn_body, split_in=split_in, final=final, n_prompt_tiles=npt),
        name="ffn_final" if final else ("ffn_first" if split_in else "ffn"),
        grid=npt + nst, in_specs=in_specs, args=args, out_specs=out_specs, out_shape=out_shape, casts=casts)


def _gelu(x):
    return 0.5 * x * (1.0 + lax.erf(x * (2.0 ** -0.5)))


def _even_body(*refs, sample, chunk_len, tiles_per_seq, sub):
    if sample:
        (x_ref, g_ref, win_ref, lng_ref, lnb_ref, sw_ref, sb_ref, cw_ref, wout_ref, b1_ref, b2_ref,
         o_ref, xg_ref, v_ref, yab_ref) = refs
    else:
        (x_ref, g_ref, win_ref, lng_ref, lnb_ref, sw_ref, sb_ref, cw_ref, wout_ref,
         o_ref, cs_ref, yab_ref, carry_ref) = refs

        @pl.when((pl.program_id(0) % tiles_per_seq) == 0)
        def _():
            carry_ref[...] = jnp.zeros_like(carry_ref)

    tm = x_ref.shape[0]
    d_a = _NH_A * _HD_A
    d_b = cw_ref.shape[1]
    o = 2 * d_a
    tt = lax.broadcasted_iota(jnp.int32, (_CHUNK, _CHUNK), 0)
    ss = lax.broadcasted_iota(jnp.int32, (_CHUNK, _CHUNK), 1)
    keep = (ss <= tt) & ((tt // chunk_len) == (ss // chunk_len))
    mix_w = [jnp.where(keep, sw_ref[hd], 0.0).astype(_BF) for hd in range(_NH_A)]
    cw = cw_ref[...]
    prev = None if sample else carry_ref[...]
    for r0 in range(0, tm, sub):
        rows = slice(r0, r0 + sub)
        x = x_ref[rows, :]
        h = _rms(x, g_ref[...]).astype(_BF)
        z = _dot(h, win_ref[...])
        u = _gelu(z[:, 0:d_a])
        vv = _gelu(z[:, d_a:2 * d_a])
        mu = jnp.mean(vv, axis=-1, keepdims=True)
        vc = vv - mu
        var = jnp.mean(vc * vc, axis=-1, keepdims=True)
        v = vc * lax.rsqrt(var + _EPS) * lng_ref[...] + lnb_ref[...]
        if sample:
            v_ref[rows, :] = v
        gate_b = z[:, o:o + d_b]
        gate_c = z[:, o + d_b:o + 2 * d_b]
        x_in = z[:, o + 2 * d_b:o + 3 * d_b]

        vb = v.astype(_BF)
        for hd in range(_NH_A):
            cols = slice(hd * _HD_A, (hd + 1) * _HD_A)
            for c0 in range(0, sub, _CHUNK):
                mixed = _dot(mix_w[hd], vb[c0:c0 + _CHUNK, cols]) + sb_ref[hd]
                yab_ref[r0 + c0:r0 + c0 + _CHUNK, cols] = (u[c0:c0 + _CHUNK, cols] * mixed).astype(_BF)

        xg = gate_c * x_in
        if sample:
            t8 = lax.broadcasted_iota(jnp.int32, (sub, 1), 0) % chunk_len
            back1 = jnp.where(t8 >= 1, pltpu.roll(xg, 1, axis=0), b1_ref[rows, :])
            back2 = jnp.where(t8 >= 2, pltpu.roll(xg, 2, axis=0), b2_ref[rows, :])
            xg_ref[rows, :] = xg
        else:
            ext = jnp.concatenate([prev, xg], axis=0)
            back1 = pltpu.roll(ext, 1, axis=0)[_SUBLANES:]
            back2 = pltpu.roll(ext, 2, axis=0)[_SUBLANES:]
            prev = xg[sub - _SUBLANES:]
        conv = back2 * cw[0:1] + back1 * cw[1:2] + xg * cw[2:3]
        yab_ref[rows, d_a:d_a + d_b] = (gate_b * conv).astype(_BF)
        o_ref[rows, :] = x + _dot(yab_ref[rows, :], wout_ref[...])
    if not sample:
        carry_ref[...] = prev
        cs_ref[0] = prev


def _even_mixer(xc, g, w_in, ln_g, ln_b, sgu_w, sgu_b, conv_w, w_out, *, row0, n_rows, seq_len,
                chunk_len, conv_state=None, casts=()):
    sample = conv_state is not None
    t, d = xc.shape
    d_in = w_in.shape[1]
    d_a = _NH_A * _HD_A
    d_b = conv_w.shape[1]
    tile0 = row0 // _TM
    n_tiles = n_rows // _TM
    row = lambda i: (tile0 + i, 0)
    reps = _CHUNK // chunk_len
    sw = jnp.tile(sgu_w[:, :chunk_len, :chunk_len], (1, reps, reps))
    sb = jnp.broadcast_to(jnp.tile(sgu_b[:, :chunk_len], (1, reps))[:, :, None], (_NH_A, _CHUNK, _HD_A))
    in_specs = [pl.BlockSpec((_TM, d), row), _resident((1, d)), _resident((d, d_in)), _resident((1, d_a)),
                _resident((1, d_a)), _resident(sw.shape), _resident(sb.shape), _resident(conv_w.shape),
                _resident(w_out.shape)]
    args = [xc, g.reshape(1, d), w_in, ln_g.reshape(1, d_a), ln_b.reshape(1, d_a), sw, sb, conv_w, w_out]
    scratch = [pltpu.VMEM((_TM, d_a + d_b), _BF)]
    if sample:
        n_seq = n_rows // chunk_len
        pad = lambda a: jnp.pad(a, ((0, 0), (0, chunk_len - a.shape[1]), (0, 0))).reshape(n_rows, d_b)
        b1 = pad(conv_state[:, 1:2])
        b2 = pad(conv_state)
        local = lambda i: (i, 0)
        in_specs += [pl.BlockSpec((_TM, d_b), local), pl.BlockSpec((_TM, d_b), local)]
        args += [b1, b2]
        out_specs = [pl.BlockSpec((_TM, d), row), pl.BlockSpec((_TM, d_b), local), pl.BlockSpec((_TM, d_a), local)]
        out_shape = [jax.ShapeDtypeStruct((t, d), _F32), jax.ShapeDtypeStruct((n_rows, d_b), _F32),
                     jax.ShapeDtypeStruct((n_rows, d_a), _F32)]
        tiles_per_seq = 1
    else:
        n_seq = n_rows // seq_len
        tiles_per_seq = seq_len // _TM
        out_specs = [pl.BlockSpec((_TM, d), row),
                     pl.BlockSpec((1, _SUBLANES, d_b), lambda i: (i // tiles_per_seq, 0, 0))]
        out_shape = [jax.ShapeDtypeStruct((t, d), _F32), jax.ShapeDtypeStruct((n_seq, _SUBLANES, d_b), _F32)]
        scratch.append(pltpu.VMEM((_SUBLANES, d_b), _F32))
    return _call(
        functools.partial(_even_body, sample=sample, chunk_len=chunk_len, tiles_per_seq=tiles_per_seq, sub=_SUB),
        name="even_sample" if sample else "even_prompt",
        grid=n_tiles, in_specs=in_specs, args=args, out_specs=out_specs, out_shape=out_shape, scratch=scratch,
        casts=casts, aliases={0: 0})


def _window_sums(ext):
    gc = ext.shape[1] // len(_POOL_WINDOWS)
    sums = []
    s = ext
    width = 1
    for w in _POOL_WINDOWS:
        while width < w:
            s = s + pltpu.roll(s, width, axis=0)
            width *= 2
        sums.append(s[:, 0:gc])
        s = s[:, gc:]
    return sums


def _odd_body(*refs, sample, seq_len, tiles_per_seq, sub):
    i = pl.program_id(0)
    if sample:
        (x_ref, g_ref, win_ref, pw_ref, psc_ref, wout_ref, ctx_ref, o_ref, p_ref, y_ref, ext_ref) = refs
    else:
        (x_ref, g_ref, win_ref, pw_ref, psc_ref, wout_ref, o_ref, ps_ref, y_ref, carry_ref) = refs

        @pl.when((i % tiles_per_seq) == 0)
        def _():
            carry_ref[...] = jnp.zeros_like(carry_ref)

    tm, d = x_ref.shape
    ctx_rows = 2 * _SUBLANES
    gc = d // len(_POOL_WINDOWS)
    prev = None if sample else carry_ref[...]
    for r0 in range(0, tm, sub):
        rows = slice(r0, r0 + sub)
        x = x_ref[rows, :]
        h = _rms(x, g_ref[...]).astype(_BF)
        p = _dot(h, win_ref[...])
        t_in_tile = r0 + lax.broadcasted_iota(jnp.int32, (sub, 1), 0)
        if sample:
            nb = sub // seq_len
            b0 = r0 // seq_len
            ext_ref[b0:b0 + nb, 0:ctx_rows, :] = ctx_ref[b0:b0 + nb]
            ext_ref[b0:b0 + nb, ctx_rows:ctx_rows + seq_len, :] = p.reshape(nb, seq_len, d)
            ext = ext_ref[b0:b0 + nb].reshape(nb * (ctx_rows + seq_len), d)
            pick = lambda s: s.reshape(nb, ctx_rows + seq_len, s.shape[1])[:, ctx_rows:, :].reshape(sub, s.shape[1])
            pos = _PAST_LEN + t_in_tile % seq_len
            p_ref[rows, :] = p
        else:
            ext = jnp.concatenate([prev, p], axis=0)
            pick = lambda s: s[ctx_rows:]
            pos = (i % tiles_per_seq) * tm + t_in_tile
            prev = p[sub - ctx_rows:]
        for grp, (w, s) in enumerate(zip(_POOL_WINDOWS, _window_sums(ext))):
            cnt = jnp.minimum(pos + 1, w).astype(_F32)
            cols = slice(grp * gc, (grp + 1) * gc)
            dlt = pick(s) / cnt - p[:, cols]
            yg = _dot(dlt.astype(_BF), pw_ref[grp])
            y_ref[rows, cols] = (yg * psc_ref[:, cols]).astype(_BF)
        o_ref[rows, :] = x + _dot(y_ref[rows, :], wout_ref[...])
    if not sample:
        carry_ref[...] = prev
        ps_ref[0] = prev


def _odd_mixer(xc, g, w_in, pool_w, pool_scale, w_out, *, row0, n_rows, seq_len, pool_state=None):
    sample = pool_state is not None
    t, d = xc.shape
    ctx_rows = 2 * _SUBLANES
    tm = _TM // 4 if sample else _TM
    tile0 = row0 // tm
    n_tiles = n_rows // tm
    row = lambda i: (tile0 + i, 0)
    in_specs = [pl.BlockSpec((tm, d), row), _resident((1, d)), _resident((d, d)), _resident(pool_w.shape),
                _resident((1, d)), _resident((d, d))]
    args = [xc, g.reshape(1, d), w_in, pool_w, pool_scale.reshape(1, d), w_out]
    scratch = [pltpu.VMEM((tm, d), _BF)]
    if sample:
        nb = tm // seq_len
        ctx = jnp.pad(pool_state, ((0, 0), (ctx_rows - _POOL_CTX, 0), (0, 0)))
        in_specs.append(pl.BlockSpec((nb, ctx_rows, d), lambda i: (i, 0, 0)))
        args.append(ctx)
        out_specs = [pl.BlockSpec((tm, d), row), pl.BlockSpec((tm, d), lambda i: (i, 0))]
        out_shape = [jax.ShapeDtypeStruct((t, d), _F32), jax.ShapeDtypeStruct((n_rows, d), _F32)]
        scratch.append(pltpu.VMEM((nb, ctx_rows + seq_len, d), _F32))
        tiles_per_seq = 1
    else:
        n_seq = n_rows // seq_len
        tiles_per_seq = seq_len // tm
        out_specs = [pl.BlockSpec((tm, d), row),
                     pl.BlockSpec((1, ctx_rows, d), lambda i: (i // tiles_per_seq, 0, 0))]
        out_shape = [jax.ShapeDtypeStruct((t, d), _F32), jax.ShapeDtypeStruct((n_seq, ctx_rows, d), _F32)]
        scratch.append(pltpu.VMEM((ctx_rows, d), _F32))
    outs, _ = _call(
        functools.partial(_odd_body, sample=sample, seq_len=seq_len, tiles_per_seq=tiles_per_seq, sub=min(tm, _SUB)),
        name="odd_sample" if sample else "odd_prompt",
        grid=n_tiles, in_specs=in_specs, args=args, out_specs=out_specs, out_shape=out_shape, scratch=scratch,
        aliases={0: 0})
    return outs


def kernel(x_prompt, x_sample, state_conv, state_pool, norm_g, final_norm_g, ffn_w_gate, ffn_w_up, ffn_w_down,
           e_w_in, e_ln_g, e_ln_b, e_sgu_w, e_sgu_b, e_conv_w, e_w_out, o_w_in, o_pool_w, o_pool_scale, o_w_out):
    batch, seq, d = x_prompt.shape
    dec_batch, dec_seq, _ = x_sample.shape
    n_prompt, n_sample = batch * seq, dec_batch * dec_seq
    d_a = _NH_A * _HD_A
    d_b = e_conv_w.shape[-1]
    n_groups, gc = o_pool_w.shape[1:3]
    ffn = functools.partial(_ffn, n_prompt=n_prompt, n_sample=n_sample)
    ffn_casts = lambda layer, k: [_Cast(w, (layer, k)) for w in (ffn_w_gate, ffn_w_up, ffn_w_down)]

    w00 = [w[0, 0].astype(_BF) for w in (ffn_w_gate, ffn_w_up, ffn_w_down)]
    even_casts = [_Cast(e_w_in, (0,)), _Cast(e_w_out, (0,))]
    odd_casts = [_Cast(o_w_in, (0,)), _Cast(o_w_out, (0,)), _Cast(o_pool_w.reshape(-1, n_groups * gc, gc), (0,))]
    (xc,), (e_win, e_wout, *w01) = ffn(
        (x_prompt.reshape(n_prompt, d), x_sample.reshape(n_sample, d)), norm_g[0, 0], *w00,
        casts=even_casts + ffn_casts(0, 1))
    even = functools.partial(_even_mixer, g=norm_g[0, 1], w_in=e_win, ln_g=e_ln_g[0], ln_b=e_ln_b[0],
                             sgu_w=e_sgu_w[0], sgu_b=e_sgu_b[0], conv_w=e_conv_w[0], w_out=e_wout)
    (xc, conv_p), _ = even(xc, row0=0, n_rows=n_prompt, seq_len=seq, chunk_len=_CHUNK)
    (xc, xg_s, v_s), _ = even(xc, row0=n_prompt, n_rows=n_sample, seq_len=dec_seq, chunk_len=dec_seq,
                              conv_state=state_conv[0])
    (xc,), (o_win, o_wout, o_pw, *w10) = ffn((xc,), norm_g[0, 2], *w01, casts=odd_casts + ffn_casts(1, 0))

    (xc,), w11 = ffn((xc,), norm_g[1, 0], *w10, casts=ffn_casts(1, 1))
    odd = functools.partial(_odd_mixer, g=norm_g[1, 1], w_in=o_win, pool_w=o_pw.reshape(n_groups, gc, gc),
                            pool_scale=o_pool_scale[0], w_out=o_wout)
    xc, pool_p = odd(xc, row0=0, n_rows=n_prompt, seq_len=seq)
    xc, p_s = odd(xc, row0=n_prompt, n_rows=n_sample, seq_len=dec_seq, pool_state=state_pool[0])
    (y_p, y_s), _ = ffn((xc,), norm_g[1, 2], *w11, final_g=final_norm_g)

    keep = _CONV_W - 1
    conv_prompt = conv_p[:, _SUBLANES - keep:][None]
    conv_sample = xg_s.reshape(dec_batch, dec_seq, d_b)[:, dec_seq - keep:][None]
    chunk_v_sample = v_s.reshape(1, dec_batch, dec_seq, d_a)
    pool_prompt = pool_p[:, pool_p.shape[1] - _POOL_CTX:][None]
    pool_sample = jnp.concatenate([state_pool[0], p_s.reshape(dec_batch, dec_seq, d)], axis=1)[:, -_POOL_CTX:][None]
    return (y_p.reshape(batch, seq, d), y_s.reshape(dec_batch, dec_seq, d), conv_prompt, conv_sample,
            chunk_v_sample, pool_prompt, pool_sample)
```

```python
import collections
import functools

import jax
import jax.numpy as jnp
from jax import lax
from jax.experimental import pallas as pl
from jax.experimental.pallas import tpu as pltpu

_CHUNK = 128
_NH_A = 4
_HD_A = 128
_CONV_W = 3
_POOL_WINDOWS = (2, 4, 8, 16)
_POOL_CTX = max(_POOL_WINDOWS) - 1
_PAST_LEN = 16384
_EPS = 1e-6

_SUBLANES = 8
_BF16_ROWS = 16
_VMEM_LIMIT = 60 * 1024 * 1024

_TM = 1024
_SUB = 512
_FFN_WIDTHS = (1024, 1024, 768)

_BF = jnp.bfloat16
_F32 = jnp.float32


def _rms(x, g):
    return x * lax.rsqrt(jnp.mean(x * x, axis=-1, keepdims=True) + _EPS) * g


def _dot(a, b):
    return jnp.dot(a, b, preferred_element_type=_F32)


def _resident(shape):
    nd = len(shape)
    return pl.BlockSpec(shape, lambda i: (0,) * nd, pipeline_mode=pl.Buffered(1))


_Cast = collections.namedtuple("_Cast", "array lead widths", defaults=(None,))


def _cast_specs(cast, grid):
    lead = tuple(cast.lead)
    rows, cols = cast.array.shape[len(lead):]
    widths = cast.widths or (cols,)
    assert sum(widths) == cols
    steps = max(s for s in (1, 2, 4, 8, 16, 32) if s <= grid and rows % (s * _BF16_ROWS) == 0)
    blk = rows // steps
    step = lambda i: jnp.minimum(i, steps - 1)
    in_spec = pl.BlockSpec((None,) * len(lead) + (blk, cols), lambda i: lead + (step(i), 0))
    out_specs = [pl.BlockSpec((blk, w), lambda i: (step(i), 0)) for w in widths]
    return in_spec, out_specs, [jax.ShapeDtypeStruct((rows, w), _BF) for w in widths]


def _call(body, *, name, grid, in_specs, args, out_specs, out_shape, scratch=(), casts=(), aliases=None):
    n_in, n_out, n_cast = len(in_specs), len(out_specs), len(casts)
    specs = [_cast_specs(c, grid) for c in casts]
    n_split = [len(s[1]) for s in specs]

    def wrapped(*refs):
        ins, refs = refs[:n_in], refs[n_in:]
        cast_in, refs = refs[:n_cast], refs[n_cast:]
        outs, refs = refs[:n_out], refs[n_out:]
        for src, n in zip(cast_in, n_split):
            dsts, refs = refs[:n], refs[n:]
            c0 = 0
            for dst in dsts:
                dst[...] = src[:, c0:c0 + dst.shape[1]].astype(_BF)
                c0 += dst.shape[1]
        body(*ins, *outs, *refs)

    res = pl.pallas_call(
        wrapped,
        grid=(grid,),
        in_specs=list(in_specs) + [s[0] for s in specs],
        out_specs=list(out_specs) + [o for s in specs for o in s[1]],
        out_shape=list(out_shape) + [o for s in specs for o in s[2]],
        scratch_shapes=list(scratch),
        input_output_aliases=aliases or {},
        compiler_params=pltpu.CompilerParams(dimension_semantics=("arbitrary",), vmem_limit_bytes=_VMEM_LIMIT),
        name=name,
    )(*args, *[c.array for c in casts])
    main, rest = res[:n_out], res[n_out:]
    cast_out = []
    for n in n_split:
        cast_out.append(rest[:n])
        rest = rest[n:]
    return main, cast_out


def _ffn_body(*refs, split_in, final, n_prompt_tiles, n_chunks):
    refs = list(refs)
    i = pl.program_id(0)
    x_refs = [refs.pop(0) for _ in range(2 if split_in else 1)]
    g_ref = refs.pop(0)
    wg_refs = [refs.pop(0) for _ in range(n_chunks)]
    wu_refs = [refs.pop(0) for _ in range(n_chunks)]
    wd_ref = refs.pop(0)
    is_prompt = i < n_prompt_tiles
    for r0 in range(0, _TM, _SUB):
        rows = slice(r0, r0 + _SUB)
        if split_in:
            x = jnp.where(is_prompt, x_refs[0][rows, :], x_refs[1][rows, :])
        else:
            x = x_refs[0][rows, :]
        h = _rms(x, g_ref[...]).astype(_BF)
        acc = None
        c0 = 0
        for wg_ref, wu_ref in zip(wg_refs, wu_refs):
            cn = wg_ref.shape[1]
            a = _dot(h, wg_ref[...])
            b = _dot(h, wu_ref[...])
            act = (a * jax.nn.sigmoid(a) * b).astype(_BF)
            d = _dot(act, wd_ref[c0:c0 + cn, :])
            acc = d if acc is None else acc + d
            c0 += cn
        y = x + 0.5 * acc
        if final:
            fg_ref, yp_ref, ys_ref = refs
            y = _rms(y, fg_ref[...])

            @pl.when(is_prompt)
            def _():
                yp_ref[rows, :] = y

            @pl.when(jnp.logical_not(is_prompt))
            def _():
                ys_ref[rows, :] = y
        else:
            (o_ref,) = refs
            o_ref[rows, :] = y


def _ffn(xs, g, wg, wu, wd, *, n_prompt, n_sample, final_g=None, casts=()):
    dff, d = wd.shape
    npt, nst = n_prompt // _TM, n_sample // _TM
    split_in = len(xs) == 2
    final = final_g is not None
    row = lambda i: (i, 0)
    prow = lambda i: (jnp.minimum(i, npt - 1), 0)
    srow = lambda i: (jnp.maximum(i - npt, 0), 0)
    tile = (_TM, d)
    in_specs = [pl.BlockSpec(tile, prow), pl.BlockSpec(tile, srow)] if split_in else [pl.BlockSpec(tile, row)]
    in_specs += [_resident((1, d))] + [_resident(w.shape) for w in (*wg, *wu)] + [_resident((dff, d))]
    args = list(xs) + [g.reshape(1, d), *wg, *wu, wd]
    if final:
        in_specs.append(_resident((1, d)))
        args.append(final_g.reshape(1, d))
        out_specs = [pl.BlockSpec(tile, prow), pl.BlockSpec(tile, srow)]
        out_shape = [jax.ShapeDtypeStruct((n_prompt, d), _F32), jax.ShapeDtypeStruct((n_sample, d), _F32)]
    else:
        out_specs = [pl.BlockSpec(tile, row)]
        out_shape = [jax.ShapeDtypeStruct((n_prompt + n_sample, d), _F32)]
    return _call(
        functools.partial(_ffn_body, split_in=split_in, final=final, n_prompt_tiles=npt, n_chunks=len(wg)),
        name="ffn_final" if final else ("ffn_first" if split_in else "ffn"),
        grid=npt + nst, in_specs=in_specs, args=args, out_specs=out_specs, out_shape=out_shape, casts=casts)


def _gelu(x):
    return 0.5 * x * (1.0 + lax.erf(x * (2.0 ** -0.5)))


def _even_body(*refs, sample, chunk_len, tiles_per_seq, sub):
    if sample:
        (x_ref, g_ref, win_ref, lng_ref, lnb_ref, sw_ref, sb_ref, cw_ref, wout_ref, b1_ref, b2_ref,
         o_ref, xg_ref, v_ref, yab_ref) = refs
    else:
        (x_ref, g_ref, win_ref, lng_ref, lnb_ref, sw_ref, sb_ref, cw_ref, wout_ref,
         o_ref, cs_ref, yab_ref, carry_ref) = refs

        @pl.when((pl.program_id(0) % tiles_per_seq) == 0)
        def _():
            carry_ref[...] = jnp.zeros_like(carry_ref)

    tm = x_ref.shape[0]
    d_a = _NH_A * _HD_A
    d_b = cw_ref.shape[1]
    o = 2 * d_a
    tt = lax.broadcasted_iota(jnp.int32, (_CHUNK, _CHUNK), 0)
    ss = lax.broadcasted_iota(jnp.int32, (_CHUNK, _CHUNK), 1)
    keep = (ss <= tt) & ((tt // chunk_len) == (ss // chunk_len))
    mix_w = [jnp.where(keep, sw_ref[hd], 0.0).astype(_BF) for hd in range(_NH_A)]
    cw = cw_ref[...]
    prev = None if sample else carry_ref[...]
    for r0 in range(0, tm, sub):
        rows = slice(r0, r0 + sub)
        x = x_ref[rows, :]
        h = _rms(x, g_ref[...]).astype(_BF)
        z = _dot(h, win_ref[...])
        u = _gelu(z[:, 0:d_a])
        vv = _gelu(z[:, d_a:2 * d_a])
        mu = jnp.mean(vv, axis=-1, keepdims=True)
        vc = vv - mu
        var = jnp.mean(vc * vc, axis=-1, keepdims=True)
        v = vc * lax.rsqrt(var + _EPS) * lng_ref[...] + lnb_ref[...]
        if sample:
            v_ref[rows, :] = v
        gate_b = z[:, o:o + d_b]
        gate_c = z[:, o + d_b:o + 2 * d_b]
        x_in = z[:, o + 2 * d_b:o + 3 * d_b]

        vb = v.astype(_BF)
        for hd in range(_NH_A):
            cols = slice(hd * _HD_A, (hd + 1) * _HD_A)
            for c0 in range(0, sub, _CHUNK):
                mixed = _dot(mix_w[hd], vb[c0:c0 + _CHUNK, cols]) + sb_ref[hd]
                yab_ref[r0 + c0:r0 + c0 + _CHUNK, cols] = (u[c0:c0 + _CHUNK, cols] * mixed).astype(_BF)

        xg = gate_c * x_in
        if sample:
            t8 = lax.broadcasted_iota(jnp.int32, (sub, 1), 0) % chunk_len
            back1 = jnp.where(t8 >= 1, pltpu.roll(xg, 1, axis=0), b1_ref[rows, :])
            back2 = jnp.where(t8 >= 2, pltpu.roll(xg, 2, axis=0), b2_ref[rows, :])
            xg_ref[rows, :] = xg
        else:
            ext = jnp.concatenate([prev, xg], axis=0)
            back1 = pltpu.roll(ext, 1, axis=0)[_SUBLANES:]
            back2 = pltpu.roll(ext, 2, axis=0)[_SUBLANES:]
            prev = xg[sub - _SUBLANES:]
        conv = back2 * cw[0:1] + back1 * cw[1:2] + xg * cw[2:3]
        yab_ref[rows, d_a:d_a + d_b] = (gate_b * conv).astype(_BF)
        o_ref[rows, :] = x + _dot(yab_ref[rows, :], wout_ref[...])
    if not sample:
        carry_ref[...] = prev
        cs_ref[0] = prev


def _even_mixer(xc, g, w_in, ln_g, ln_b, sgu_w, sgu_b, conv_w, w_out, *, row0, n_rows, seq_len,
                chunk_len, conv_state=None, casts=()):
    sample = conv_state is not None
    t, d = xc.shape
    d_in = w_in.shape[1]
    d_a = _NH_A * _HD_A
    d_b = conv_w.shape[1]
    tile0 = row0 // _TM
    n_tiles = n_rows // _TM
    row = lambda i: (tile0 + i, 0)
    reps = _CHUNK // chunk_len
    sw = jnp.tile(sgu_w[:, :chunk_len, :chunk_len], (1, reps, reps))
    sb = jnp.broadcast_to(jnp.tile(sgu_b[:, :chunk_len], (1, reps))[:, :, None], (_NH_A, _CHUNK, _HD_A))
    in_specs = [pl.BlockSpec((_TM, d), row), _resident((1, d)), _resident((d, d_in)), _resident((1, d_a)),
                _resident((1, d_a)), _resident(sw.shape), _resident(sb.shape), _resident(conv_w.shape),
                _resident(w_out.shape)]
    args = [xc, g.reshape(1, d), w_in, ln_g.reshape(1, d_a), ln_b.reshape(1, d_a), sw, sb, conv_w, w_out]
    scratch = [pltpu.VMEM((_TM, d_a + d_b), _BF)]
    if sample:
        n_seq = n_rows // chunk_len
        pad = lambda a: jnp.pad(a, ((0, 0), (0, chunk_len - a.shape[1]), (0, 0))).reshape(n_rows, d_b)
        b1 = pad(conv_state[:, 1:2])
        b2 = pad(conv_state)
        local = lambda i: (i, 0)
        in_specs += [pl.BlockSpec((_TM, d_b), local), pl.BlockSpec((_TM, d_b), local)]
        args += [b1, b2]
        out_specs = [pl.BlockSpec((_TM, d), row), pl.BlockSpec((_TM, d_b), local), pl.BlockSpec((_TM, d_a), local)]
        out_shape = [jax.ShapeDtypeStruct((t, d), _F32), jax.ShapeDtypeStruct((n_rows, d_b), _F32),
                     jax.ShapeDtypeStruct((n_rows, d_a), _F32)]
        tiles_per_seq = 1
    else:
        n_seq = n_rows // seq_len
        tiles_per_seq = seq_len // _TM
        out_specs = [pl.BlockSpec((_TM, d), row),
                     pl.BlockSpec((1, _SUBLANES, d_b), lambda i: (i // tiles_per_seq, 0, 0))]
        out_shape = [jax.ShapeDtypeStruct((t, d), _F32), jax.ShapeDtypeStruct((n_seq, _SUBLANES, d_b), _F32)]
        scratch.append(pltpu.VMEM((_SUBLANES, d_b), _F32))
    return _call(
        functools.partial(_even_body, sample=sample, chunk_len=chunk_len, tiles_per_seq=tiles_per_seq, sub=_SUB),
        name="even_sample" if sample else "even_prompt",
        grid=n_tiles, in_specs=in_specs, args=args, out_specs=out_specs, out_shape=out_shape, scratch=scratch,
        casts=casts, aliases={0: 0})


def _window_sums(ext):
    gc = ext.shape[1] // len(_POOL_WINDOWS)
    sums = []
    s = ext
    width = 1
    for w in _POOL_WINDOWS:
        while width < w:
            s = s + pltpu.roll(s, width, axis=0)
            width *= 2
        sums.append(s[:, 0:gc])
        s = s[:, gc:]
    return sums


def _odd_body(*refs, sample, seq_len, tiles_per_seq, sub):
    i = pl.program_id(0)
    if sample:
        (x_ref, g_ref, win_ref, pw_ref, psc_ref, wout_ref, ctx_ref, o_ref, p_ref, y_ref, ext_ref) = refs
    else:
        (x_ref, g_ref, win_ref, pw_ref, psc_ref, wout_ref, o_ref, ps_ref, y_ref, carry_ref) = refs

        @pl.when((i % tiles_per_seq) == 0)
        def _():
            carry_ref[...] = jnp.zeros_like(carry_ref)

    tm, d = x_ref.shape
    ctx_rows = 2 * _SUBLANES
    gc = d // len(_POOL_WINDOWS)
    prev = None if sample else carry_ref[...]
    for r0 in range(0, tm, sub):
        rows = slice(r0, r0 + sub)
        x = x_ref[rows, :]
        h = _rms(x, g_ref[...]).astype(_BF)
        p = _dot(h, win_ref[...])
        t_in_tile = r0 + lax.broadcasted_iota(jnp.int32, (sub, 1), 0)
        if sample:
            nb = sub // seq_len
            b0 = r0 // seq_len
            ext_ref[b0:b0 + nb, 0:ctx_rows, :] = ctx_ref[b0:b0 + nb]
            ext_ref[b0:b0 + nb, ctx_rows:ctx_rows + seq_len, :] = p.reshape(nb, seq_len, d)
            ext = ext_ref[b0:b0 + nb].reshape(nb * (ctx_rows + seq_len), d)
            pick = lambda s: s.reshape(nb, ctx_rows + seq_len, s.shape[1])[:, ctx_rows:, :].reshape(sub, s.shape[1])
            pos = _PAST_LEN + t_in_tile % seq_len
            p_ref[rows, :] = p
        else:
            ext = jnp.concatenate([prev, p], axis=0)
            pick = lambda s: s[ctx_rows:]
            pos = (i % tiles_per_seq) * tm + t_in_tile
            prev = p[sub - ctx_rows:]
        for grp, (w, s) in enumerate(zip(_POOL_WINDOWS, _window_sums(ext))):
            cnt = jnp.minimum(pos + 1, w).astype(_F32)
            cols = slice(grp * gc, (grp + 1) * gc)
            dlt = pick(s) / cnt - p[:, cols]
            yg = _dot(dlt.astype(_BF), pw_ref[grp])
            y_ref[rows, cols] = (yg * psc_ref[:, cols]).astype(_BF)
        o_ref[rows, :] = x + _dot(y_ref[rows, :], wout_ref[...])
    if not sample:
        carry_ref[...] = prev
        ps_ref[0] = prev


def _odd_mixer(xc, g, w_in, pool_w, pool_scale, w_out, *, row0, n_rows, seq_len, pool_state=None):
    sample = pool_state is not None
    t, d = xc.shape
    ctx_rows = 2 * _SUBLANES
    tm = _TM // 4 if sample else _TM
    tile0 = row0 // tm
    n_tiles = n_rows // tm
    row = lambda i: (tile0 + i, 0)
    in_specs = [pl.BlockSpec((tm, d), row), _resident((1, d)), _resident((d, d)), _resident(pool_w.shape),
                _resident((1, d)), _resident((d, d))]
    args = [xc, g.reshape(1, d), w_in, pool_w, pool_scale.reshape(1, d), w_out]
    scratch = [pltpu.VMEM((tm, d), _BF)]
    if sample:
        nb = tm // seq_len
        ctx = jnp.pad(pool_state, ((0, 0), (ctx_rows - _POOL_CTX, 0), (0, 0)))
        in_specs.append(pl.BlockSpec((nb, ctx_rows, d), lambda i: (i, 0, 0)))
        args.append(ctx)
        out_specs = [pl.BlockSpec((tm, d), row), pl.BlockSpec((tm, d), lambda i: (i, 0))]
        out_shape = [jax.ShapeDtypeStruct((t, d), _F32), jax.ShapeDtypeStruct((n_rows, d), _F32)]
        scratch.append(pltpu.VMEM((nb, ctx_rows + seq_len, d), _F32))
        tiles_per_seq = 1
    else:
        n_seq = n_rows // seq_len
        tiles_per_seq = seq_len // tm
        out_specs = [pl.BlockSpec((tm, d), row),
                     pl.BlockSpec((1, ctx_rows, d), lambda i: (i // tiles_per_seq, 0, 0))]
        out_shape = [jax.ShapeDtypeStruct((t, d), _F32), jax.ShapeDtypeStruct((n_seq, ctx_rows, d), _F32)]
        scratch.append(pltpu.VMEM((ctx_rows, d), _F32))
    outs, _ = _call(
        functools.partial(_odd_body, sample=sample, seq_len=seq_len, tiles_per_seq=tiles_per_seq, sub=min(tm, _SUB)),
        name="odd_sample" if sample else "odd_prompt",
        grid=n_tiles, in_specs=in_specs, args=args, out_specs=out_specs, out_shape=out_shape, scratch=scratch,
        aliases={0: 0})
    return outs


def kernel(x_prompt, x_sample, state_conv, state_pool, norm_g, final_norm_g, ffn_w_gate, ffn_w_up, ffn_w_down,
           e_w_in, e_ln_g, e_ln_b, e_sgu_w, e_sgu_b, e_conv_w, e_w_out, o_w_in, o_pool_w, o_pool_scale, o_w_out):
    batch, seq, d = x_prompt.shape
    dec_batch, dec_seq, _ = x_sample.shape
    n_prompt, n_sample = batch * seq, dec_batch * dec_seq
    d_a = _NH_A * _HD_A
    d_b = e_conv_w.shape[-1]
    n_groups, gc = o_pool_w.shape[1:3]
    ffn = functools.partial(_ffn, n_prompt=n_prompt, n_sample=n_sample)
    ffn_casts = lambda layer, k: [_Cast(ffn_w_gate, (layer, k), _FFN_WIDTHS), _Cast(ffn_w_up, (layer, k), _FFN_WIDTHS),
                                  _Cast(ffn_w_down, (layer, k))]
    unpack = lambda wg, wu, wd: (wg, wu, wd[0])

    cols = [(sum(_FFN_WIDTHS[:j]), sum(_FFN_WIDTHS[:j + 1])) for j in range(len(_FFN_WIDTHS))]
    w00 = ([ffn_w_gate[0, 0, :, a:b].astype(_BF) for a, b in cols],
           [ffn_w_up[0, 0, :, a:b].astype(_BF) for a, b in cols], ffn_w_down[0, 0].astype(_BF))
    even_casts = [_Cast(e_w_in, (0,)), _Cast(e_w_out, (0,))]
    odd_casts = [_Cast(o_w_in, (0,)), _Cast(o_w_out, (0,)), _Cast(o_pool_w.reshape(-1, n_groups * gc, gc), (0,))]
    (xc,), ((e_win,), (e_wout,), *w01) = ffn(
        (x_prompt.reshape(n_prompt, d), x_sample.reshape(n_sample, d)), norm_g[0, 0], *w00,
        casts=even_casts + ffn_casts(0, 1))
    even = functools.partial(_even_mixer, g=norm_g[0, 1], w_in=e_win, ln_g=e_ln_g[0], ln_b=e_ln_b[0],
                             sgu_w=e_sgu_w[0], sgu_b=e_sgu_b[0], conv_w=e_conv_w[0], w_out=e_wout)
    (xc, conv_p), _ = even(xc, row0=0, n_rows=n_prompt, seq_len=seq, chunk_len=_CHUNK)
    (xc, xg_s, v_s), _ = even(xc, row0=n_prompt, n_rows=n_sample, seq_len=dec_seq, chunk_len=dec_seq,
                              conv_state=state_conv[0])
    (xc,), ((o_win,), (o_wout,), (o_pw,), *w10) = ffn((xc,), norm_g[0, 2], *unpack(*w01),
                                                       casts=odd_casts + ffn_casts(1, 0))

    (xc,), w11 = ffn((xc,), norm_g[1, 0], *unpack(*w10), casts=ffn_casts(1, 1))
    odd = functools.partial(_odd_mixer, g=norm_g[1, 1], w_in=o_win, pool_w=o_pw.reshape(n_groups, gc, gc),
                            pool_scale=o_pool_scale[0], w_out=o_wout)
    xc, pool_p = odd(xc, row0=0, n_rows=n_prompt, seq_len=seq)
    xc, p_s = odd(xc, row0=n_prompt, n_rows=n_sample, seq_len=dec_seq, pool_state=state_pool[0])
    (y_p, y_s), _ = ffn((xc,), norm_g[1, 2], *unpack(*w11), final_g=final_norm_g)

    keep = _CONV_W - 1
    conv_prompt = conv_p[:, _SUBLANES - keep:][None]
    conv_sample = xg_s.reshape(dec_batch, dec_seq, d_b)[:, dec_seq - keep:][None]
    chunk_v_sample = v_s.reshape(1, dec_batch, dec_seq, d_a)
    pool_prompt = pool_p[:, pool_p.shape[1] - _POOL_CTX:][None]
    pool_sample = jnp.concatenate([state_pool[0], p_s.reshape(dec_batch, dec_seq, d)], axis=1)[:, -_POOL_CTX:][None]
    return (y_p.reshape(batch, seq, d), y_s.reshape(dec_batch, dec_seq, d), conv_prompt, conv_sample,
            chunk_v_sample, pool_prompt, pool_sample)
```

```python
import collections
import functools

import jax
import jax.numpy as jnp
from jax import lax
from jax.experimental import pallas as pl
from jax.experimental.pallas import tpu as pltpu

_CHUNK = 128
_NH_A = 4
_HD_A = 128
_CONV_W = 3
_POOL_WINDOWS = (2, 4, 8, 16)
_POOL_CTX = max(_POOL_WINDOWS) - 1
_PAST_LEN = 16384
_EPS = 1e-6

_SUBLANES = 8
_BF16_ROWS = 16
_VMEM_LIMIT = 56 * 1024 * 1024

_TM = 1024
_SUB = 512
_FFN_CHUNKS = ((0, 1536), (1536, 1280))

_BF = jnp.bfloat16
_F32 = jnp.float32


def _rms(x, g):
    return x * lax.rsqrt(jnp.mean(x * x, axis=-1, keepdims=True) + _EPS) * g


def _dot(a, b):
    return jnp.dot(a, b, preferred_element_type=_F32)


def _resident(shape):
    nd = len(shape)
    return pl.BlockSpec(shape, lambda i: (0,) * nd, pipeline_mode=pl.Buffered(1))


_Cast = collections.namedtuple("_Cast", "array lead")


def _cast_specs(cast, grid):
    lead = tuple(cast.lead)
    rows, cols = cast.array.shape[len(lead):]
    steps = max(s for s in (1, 2, 4, 8, 16, 32) if s <= grid and rows % (s * _BF16_ROWS) == 0)
    blk = rows // steps
    step = lambda i: jnp.minimum(i, steps - 1)
    in_spec = pl.BlockSpec((None,) * len(lead) + (blk, cols), lambda i: lead + (step(i), 0))
    out_spec = pl.BlockSpec((blk, cols), lambda i: (step(i), 0))
    return in_spec, out_spec, jax.ShapeDtypeStruct((rows, cols), _BF)


def _call(body, *, name, grid, in_specs, args, out_specs, out_shape, scratch=(), casts=(), aliases=None):
    n_in, n_out, n_cast = len(in_specs), len(out_specs), len(casts)
    specs = [_cast_specs(c, grid) for c in casts]

    def wrapped(*refs):
        ins, refs = refs[:n_in], refs[n_in:]
        cast_in, refs = refs[:n_cast], refs[n_cast:]
        outs, refs = refs[:n_out], refs[n_out:]
        cast_out, scratch_refs = refs[:n_cast], refs[n_cast:]
        for src, dst in zip(cast_in, cast_out):
            dst[...] = src[...].astype(_BF)
        body(*ins, *outs, *scratch_refs)

    res = pl.pallas_call(
        wrapped,
        grid=(grid,),
        in_specs=list(in_specs) + [s[0] for s in specs],
        out_specs=list(out_specs) + [s[1] for s in specs],
        out_shape=list(out_shape) + [s[2] for s in specs],
        scratch_shapes=list(scratch),
        input_output_aliases=aliases or {},
        compiler_params=pltpu.CompilerParams(dimension_semantics=("arbitrary",), vmem_limit_bytes=_VMEM_LIMIT),
        name=name,
    )(*args, *[c.array for c in casts])
    return res[:n_out], res[n_out:]


def _ffn_body(*refs, split_in, final, n_prompt_tiles):
    refs = list(refs)
    i = pl.program_id(0)
    x_refs = [refs.pop(0) for _ in range(2 if split_in else 1)]
    g_ref, wg_ref, wu_ref, wd_ref = refs[:4]
    refs = refs[4:]
    is_prompt = i < n_prompt_tiles
    for r0 in range(0, _TM, _SUB):
        rows = slice(r0, r0 + _SUB)
        if split_in:
            x = jnp.where(is_prompt, x_refs[0][rows, :], x_refs[1][rows, :])
        else:
            x = x_refs[0][rows, :]
        h = _rms(x, g_ref[...]).astype(_BF)
        acc = None
        for c0, cn in _FFN_CHUNKS:
            a = _dot(h, wg_ref[:, c0:c0 + cn])
            b = _dot(h, wu_ref[:, c0:c0 + cn])
            act = (a * jax.nn.sigmoid(a) * b).astype(_BF)
            d = _dot(act, wd_ref[c0:c0 + cn, :])
            acc = d if acc is None else acc + d
        y = x + 0.5 * acc
        if final:
            fg_ref, yp_ref, ys_ref = refs
            y = _rms(y, fg_ref[...])
            to_prompt = jnp.broadcast_to(is_prompt, y.shape)
            pltpu.store(yp_ref.at[rows, :], y, mask=to_prompt)
            pltpu.store(ys_ref.at[rows, :], y, mask=jnp.logical_not(to_prompt))
        else:
            (o_ref,) = refs
            o_ref[rows, :] = y


def _ffn(xs, g, wg, wu, wd, *, n_prompt, n_sample, final_g=None, casts=()):
    d, dff = wg.shape
    npt, nst = n_prompt // _TM, n_sample // _TM
    split_in = len(xs) == 2
    final = final_g is not None
    row = lambda i: (i, 0)
    prow = lambda i: (jnp.minimum(i, npt - 1), 0)
    srow = lambda i: (jnp.maximum(i - npt, 0), 0)
    tile = (_TM, d)
    if split_in:
        in_specs = [pl.BlockSpec(tile, prow), pl.BlockSpec(tile, srow, pipeline_mode=pl.Buffered(1))]
    else:
        in_specs = [pl.BlockSpec(tile, row)]
    in_specs += [_resident((1, d)), _resident((d, dff)), _resident((d, dff)), _resident((dff, d))]
    args = list(xs) + [g.reshape(1, d), wg, wu, wd]
    if final:
        in_specs.append(_resident((1, d)))
        args.append(final_g.reshape(1, d))
        out_specs = [pl.BlockSpec(tile, prow), pl.BlockSpec(tile, srow)]
        out_shape = [jax.ShapeDtypeStruct((n_prompt, d), _F32), jax.ShapeDtypeStruct((n_sample, d), _F32)]
    else:
        out_specs = [pl.BlockSpec(tile, row)]
        out_shape = [jax.ShapeDtypeStruct((n_prompt + n_sample, d), _F32)]
    return _call(
        functools.partial(_ffn_body, split_in=split_in, final=final, n_prompt_tiles=npt),
        name="ffn_final" if final else ("ffn_first" if split_in else "ffn"),
        grid=npt + nst, in_specs=in_specs, args=args, out_specs=out_specs, out_shape=out_shape, casts=casts)


def _gelu(x):
    return 0.5 * x * (1.0 + lax.erf(x * (2.0 ** -0.5)))


def _even_body(*refs, sample, chunk_len, tiles_per_seq, sub):
    if sample:
        (x_ref, g_ref, win_ref, lng_ref, lnb_ref, sw_ref, sb_ref, cw_ref, wout_ref, b1_ref, b2_ref,
         o_ref, xg_ref, v_ref, yab_ref) = refs
    else:
        (x_ref, g_ref, win_ref, lng_ref, lnb_ref, sw_ref, sb_ref, cw_ref, wout_ref,
         o_ref, cs_ref, yab_ref, carry_ref) = refs

        @pl.when((pl.program_id(0) % tiles_per_seq) == 0)
        def _():
            carry_ref[...] = jnp.zeros_like(carry_ref)

    tm = x_ref.shape[0]
    d_a = _NH_A * _HD_A
    d_b = cw_ref.shape[1]
    o = 2 * d_a
    tt = lax.broadcasted_iota(jnp.int32, (_CHUNK, _CHUNK), 0)
    ss = lax.broadcasted_iota(jnp.int32, (_CHUNK, _CHUNK), 1)
    keep = (ss <= tt) & ((tt // chunk_len) == (ss // chunk_len))
    mix_w = [jnp.where(keep, sw_ref[hd], 0.0).astype(_BF) for hd in range(_NH_A)]
    cw = cw_ref[...]
    prev = None if sample else carry_ref[...]
    for r0 in range(0, tm, sub):
        rows = slice(r0, r0 + sub)
        x = x_ref[rows, :]
        h = _rms(x, g_ref[...]).astype(_BF)
        z = _dot(h, win_ref[...])
        u = _gelu(z[:, 0:d_a])
        vv = _gelu(z[:, d_a:2 * d_a])
        mu = jnp.mean(vv, axis=-1, keepdims=True)
        vc = vv - mu
        var = jnp.mean(vc * vc, axis=-1, keepdims=True)
        v = vc * lax.rsqrt(var + _EPS) * lng_ref[...] + lnb_ref[...]
        if sample:
            v_ref[rows, :] = v
        gate_b = z[:, o:o + d_b]
        gate_c = z[:, o + d_b:o + 2 * d_b]
        x_in = z[:, o + 2 * d_b:o + 3 * d_b]

        vb = v.astype(_BF)
        for hd in range(_NH_A):
            cols = slice(hd * _HD_A, (hd + 1) * _HD_A)
            for c0 in range(0, sub, _CHUNK):
                mixed = _dot(mix_w[hd], vb[c0:c0 + _CHUNK, cols]) + sb_ref[hd]
                yab_ref[r0 + c0:r0 + c0 + _CHUNK, cols] = (u[c0:c0 + _CHUNK, cols] * mixed).astype(_BF)

        xg = gate_c * x_in
        if sample:
            t8 = lax.broadcasted_iota(jnp.int32, (sub, 1), 0) % chunk_len
            back1 = jnp.where(t8 >= 1, pltpu.roll(xg, 1, axis=0), b1_ref[rows, :])
            back2 = jnp.where(t8 >= 2, pltpu.roll(xg, 2, axis=0), b2_ref[rows, :])
            xg_ref[rows, :] = xg
        else:
            ext = jnp.concatenate([prev, xg], axis=0)
            back1 = pltpu.roll(ext, 1, axis=0)[_SUBLANES:]
            back2 = pltpu.roll(ext, 2, axis=0)[_SUBLANES:]
            prev = xg[sub - _SUBLANES:]
        conv = back2 * cw[0:1] + back1 * cw[1:2] + xg * cw[2:3]
        yab_ref[rows, d_a:d_a + d_b] = (gate_b * conv).astype(_BF)
        o_ref[rows, :] = x + _dot(yab_ref[rows, :], wout_ref[...])
    if not sample:
        carry_ref[...] = prev
        cs_ref[0] = prev


def _even_mixer(xc, g, w_in, ln_g, ln_b, sgu_w, sgu_b, conv_w, w_out, *, row0, n_rows, seq_len,
                chunk_len, conv_state=None, casts=()):
    sample = conv_state is not None
    t, d = xc.shape
    d_in = w_in.shape[1]
    d_a = _NH_A * _HD_A
    d_b = conv_w.shape[1]
    tile0 = row0 // _TM
    n_tiles = n_rows // _TM
    row = lambda i: (tile0 + i, 0)
    reps = _CHUNK // chunk_len
    sw = jnp.tile(sgu_w[:, :chunk_len, :chunk_len], (1, reps, reps))
    sb = jnp.broadcast_to(jnp.tile(sgu_b[:, :chunk_len], (1, reps))[:, :, None], (_NH_A, _CHUNK, _HD_A))
    in_specs = [pl.BlockSpec((_TM, d), row), _resident((1, d)), _resident((d, d_in)), _resident((1, d_a)),
                _resident((1, d_a)), _resident(sw.shape), _resident(sb.shape), _resident(conv_w.shape),
                _resident(w_out.shape)]
    args = [xc, g.reshape(1, d), w_in, ln_g.reshape(1, d_a), ln_b.reshape(1, d_a), sw, sb, conv_w, w_out]
    scratch = [pltpu.VMEM((_TM, d_a + d_b), _BF)]
    if sample:
        n_seq = n_rows // chunk_len
        pad = lambda a: jnp.pad(a, ((0, 0), (0, chunk_len - a.shape[1]), (0, 0))).reshape(n_rows, d_b)
        b1 = pad(conv_state[:, 1:2])
        b2 = pad(conv_state)
        local = lambda i: (i, 0)
        in_specs += [pl.BlockSpec((_TM, d_b), local), pl.BlockSpec((_TM, d_b), local)]
        args += [b1, b2]
        out_specs = [pl.BlockSpec((_TM, d), row), pl.BlockSpec((_TM, d_b), local), pl.BlockSpec((_TM, d_a), local)]
        out_shape = [jax.ShapeDtypeStruct((t, d), _F32), jax.ShapeDtypeStruct((n_rows, d_b), _F32),
                     jax.ShapeDtypeStruct((n_rows, d_a), _F32)]
        tiles_per_seq = 1
    else:
        n_seq = n_rows // seq_len
        tiles_per_seq = seq_len // _TM
        out_specs = [pl.BlockSpec((_TM, d), row),
                     pl.BlockSpec((1, _SUBLANES, d_b), lambda i: (i // tiles_per_seq, 0, 0))]
        out_shape = [jax.ShapeDtypeStruct((t, d), _F32), jax.ShapeDtypeStruct((n_seq, _SUBLANES, d_b), _F32)]
        scratch.append(pltpu.VMEM((_SUBLANES, d_b), _F32))
    return _call(
        functools.partial(_even_body, sample=sample, chunk_len=chunk_len, tiles_per_seq=tiles_per_seq, sub=_SUB),
        name="even_sample" if sample else "even_prompt",
        grid=n_tiles, in_specs=in_specs, args=args, out_specs=out_specs, out_shape=out_shape, scratch=scratch,
        casts=casts, aliases={0: 0})


def _window_sums(ext):
    gc = ext.shape[1] // len(_POOL_WINDOWS)
    sums = []
    s = ext
    width = 1
    for w in _POOL_WINDOWS:
        while width < w:
            s = s + pltpu.roll(s, width, axis=0)
            width *= 2
        sums.append(s[:, 0:gc])
        s = s[:, gc:]
    return sums


def _odd_body(*refs, sample, seq_len, tiles_per_seq, sub):
    i = pl.program_id(0)
    if sample:
        (x_ref, g_ref, win_ref, pw_ref, psc_ref, wout_ref, ctx_ref, o_ref, p_ref, y_ref, ext_ref) = refs
    else:
        (x_ref, g_ref, win_ref, pw_ref, psc_ref, wout_ref, o_ref, ps_ref, y_ref, carry_ref) = refs

        @pl.when((i % tiles_per_seq) == 0)
        def _():
            carry_ref[...] = jnp.zeros_like(carry_ref)

    tm, d = x_ref.shape
    ctx_rows = 2 * _SUBLANES
    gc = d // len(_POOL_WINDOWS)
    prev = None if sample else carry_ref[...]
    for r0 in range(0, tm, sub):
        rows = slice(r0, r0 + sub)
        x = x_ref[rows, :]
        h = _rms(x, g_ref[...]).astype(_BF)
        p = _dot(h, win_ref[...])
        t_in_tile = r0 + lax.broadcasted_iota(jnp.int32, (sub, 1), 0)
        if sample:
            nb = sub // seq_len
            b0 = r0 // seq_len
            ext_ref[b0:b0 + nb, 0:ctx_rows, :] = ctx_ref[b0:b0 + nb]
            ext_ref[b0:b0 + nb, ctx_rows:ctx_rows + seq_len, :] = p.reshape(nb, seq_len, d)
            ext = ext_ref[b0:b0 + nb].reshape(nb * (ctx_rows + seq_len), d)
            pick = lambda s: s.reshape(nb, ctx_rows + seq_len, s.shape[1])[:, ctx_rows:, :].reshape(sub, s.shape[1])
            pos = _PAST_LEN + t_in_tile % seq_len
            p_ref[rows, :] = p
        else:
            ext = jnp.concatenate([prev, p], axis=0)
            pick = lambda s: s[ctx_rows:]
            pos = (i % tiles_per_seq) * tm + t_in_tile
            prev = p[sub - ctx_rows:]
        for grp, (w, s) in enumerate(zip(_POOL_WINDOWS, _window_sums(ext))):
            cnt = jnp.minimum(pos + 1, w).astype(_F32)
            cols = slice(grp * gc, (grp + 1) * gc)
            dlt = pick(s) / cnt - p[:, cols]
            yg = _dot(dlt.astype(_BF), pw_ref[grp])
            y_ref[rows, cols] = (yg * psc_ref[:, cols]).astype(_BF)
        o_ref[rows, :] = x + _dot(y_ref[rows, :], wout_ref[...])
    if not sample:
        carry_ref[...] = prev
        ps_ref[0] = prev


def _odd_mixer(xc, g, w_in, pool_w, pool_scale, w_out, *, row0, n_rows, seq_len, pool_state=None):
    sample = pool_state is not None
    t, d = xc.shape
    ctx_rows = 2 * _SUBLANES
    tm = _TM // 4 if sample else _TM
    tile0 = row0 // tm
    n_tiles = n_rows // tm
    row = lambda i: (tile0 + i, 0)
    in_specs = [pl.BlockSpec((tm, d), row), _resident((1, d)), _resident((d, d)), _resident(pool_w.shape),
                _resident((1, d)), _resident((d, d))]
    args = [xc, g.reshape(1, d), w_in, pool_w, pool_scale.reshape(1, d), w_out]
    scratch = [pltpu.VMEM((tm, d), _BF)]
    if sample:
        nb = tm // seq_len
        ctx = jnp.pad(pool_state, ((0, 0), (ctx_rows - _POOL_CTX, 0), (0, 0)))
        in_specs.append(pl.BlockSpec((nb, ctx_rows, d), lambda i: (i, 0, 0)))
        args.append(ctx)
        out_specs = [pl.BlockSpec((tm, d), row), pl.BlockSpec((tm, d), lambda i: (i, 0))]
        out_shape = [jax.ShapeDtypeStruct((t, d), _F32), jax.ShapeDtypeStruct((n_rows, d), _F32)]
        scratch.append(pltpu.VMEM((nb, ctx_rows + seq_len, d), _F32))
        tiles_per_seq = 1
    else:
        n_seq = n_rows // seq_len
        tiles_per_seq = seq_len // tm
        out_specs = [pl.BlockSpec((tm, d), row),
                     pl.BlockSpec((1, ctx_rows, d), lambda i: (i // tiles_per_seq, 0, 0))]
        out_shape = [jax.ShapeDtypeStruct((t, d), _F32), jax.ShapeDtypeStruct((n_seq, ctx_rows, d), _F32)]
        scratch.append(pltpu.VMEM((ctx_rows, d), _F32))
    outs, _ = _call(
        functools.partial(_odd_body, sample=sample, seq_len=seq_len, tiles_per_seq=tiles_per_seq, sub=min(tm, _SUB)),
        name="odd_sample" if sample else "odd_prompt",
        grid=n_tiles, in_specs=in_specs, args=args, out_specs=out_specs, out_shape=out_shape, scratch=scratch,
        aliases={0: 0})
    return outs


def kernel(x_prompt, x_sample, state_conv, state_pool, norm_g, final_norm_g, ffn_w_gate, ffn_w_up, ffn_w_down,
           e_w_in, e_ln_g, e_ln_b, e_sgu_w, e_sgu_b, e_conv_w, e_w_out, o_w_in, o_pool_w, o_pool_scale, o_w_out):
    batch, seq, d = x_prompt.shape
    dec_batch, dec_seq, _ = x_sample.shape
    n_prompt, n_sample = batch * seq, dec_batch * dec_seq
    d_a = _NH_A * _HD_A
    d_b = e_conv_w.shape[-1]
    n_groups, gc = o_pool_w.shape[1:3]
    ffn = functools.partial(_ffn, n_prompt=n_prompt, n_sample=n_sample)
    ffn_casts = lambda layer, k: [_Cast(w, (layer, k)) for w in (ffn_w_gate, ffn_w_up, ffn_w_down)]

    w00 = [w[0, 0].astype(_BF) for w in (ffn_w_gate, ffn_w_up, ffn_w_down)]
    even_casts = [_Cast(e_w_in, (0,)), _Cast(e_w_out, (0,))]
    odd_casts = [_Cast(o_w_in, (0,)), _Cast(o_w_out, (0,)), _Cast(o_pool_w.reshape(-1, n_groups * gc, gc), (0,))]
    (xc,), (e_win, e_wout, *w01) = ffn(
        (x_prompt.reshape(n_prompt, d), x_sample.reshape(n_sample, d)), norm_g[0, 0], *w00,
        casts=even_casts + ffn_casts(0, 1))
    even = functools.partial(_even_mixer, g=norm_g[0, 1], w_in=e_win, ln_g=e_ln_g[0], ln_b=e_ln_b[0],
                             sgu_w=e_sgu_w[0], sgu_b=e_sgu_b[0], conv_w=e_conv_w[0], w_out=e_wout)
    (xc, conv_p), _ = even(xc, row0=0, n_rows=n_prompt, seq_len=seq, chunk_len=_CHUNK)
    (xc, xg_s, v_s), _ = even(xc, row0=n_prompt, n_rows=n_sample, seq_len=dec_seq, chunk_len=dec_seq,
                              conv_state=state_conv[0])
    (xc,), (o_win, o_wout, o_pw, *w10) = ffn((xc,), norm_g[0, 2], *w01, casts=odd_casts + ffn_casts(1, 0))

    (xc,), w11 = ffn((xc,), norm_g[1, 0], *w10, casts=ffn_casts(1, 1))
    odd = functools.partial(_odd_mixer, g=norm_g[1, 1], w_in=o_win, pool_w=o_pw.reshape(n_groups, gc, gc),
                            pool_scale=o_pool_scale[0], w_out=o_wout)
    xc, pool_p = odd(xc, row0=0, n_rows=n_prompt, seq_len=seq)
    xc, p_s = odd(xc, row0=n_prompt, n_rows=n_sample, seq_len=dec_seq, pool_state=state_pool[0])
    (y_p, y_s), _ = ffn((xc,), norm_g[1, 2], *w11, final_g=final_norm_g)

    keep = _CONV_W - 1
    conv_prompt = conv_p[:, _SUBLANES - keep:][None]
    conv_sample = xg_s.reshape(dec_batch, dec_seq, d_b)[:, dec_seq - keep:][None]
    chunk_v_sample = v_s.reshape(1, dec_batch, dec_seq, d_a)
    pool_prompt = pool_p[:, pool_p.shape[1] - _POOL_CTX:][None]
    pool_sample = jnp.concatenate([state_pool[0], p_s.reshape(dec_batch, dec_seq, d)], axis=1)[:, -_POOL_CTX:][None]
    return (y_p.reshape(batch, seq, d), y_s.reshape(dec_batch, dec_seq, d), conv_prompt, conv_sample,
            chunk_v_sample, pool_prompt, pool_sample)
```

```python
import collections
import functools

import jax
import jax.numpy as jnp
from jax import lax
from jax.experimental import pallas as pl
from jax.experimental.pallas import tpu as pltpu

_CHUNK = 128
_NH_A = 4
_HD_A = 128
_CONV_W = 3
_POOL_WINDOWS = (2, 4, 8, 16)
_POOL_CTX = max(_POOL_WINDOWS) - 1
_PAST_LEN = 16384
_EPS = 1e-6

_SUBLANES = 8
_BF16_ROWS = 16
_VMEM_LIMIT = 56 * 1024 * 1024

_TM = 1024
_SUB = 512
_FFN_CHUNKS = ((0, 1536), (1536, 1280))

_BF = jnp.bfloat16
_F32 = jnp.float32


def _rms(x, g):
    return x * lax.rsqrt(jnp.mean(x * x, axis=-1, keepdims=True) + _EPS) * g


def _dot(a, b):
    return jnp.dot(a, b, preferred_element_type=_F32)


def _resident(shape):
    nd = len(shape)
    return pl.BlockSpec(shape, lambda i: (0,) * nd, pipeline_mode=pl.Buffered(1))


_Cast = collections.namedtuple("_Cast", "array lead")


def _cast_specs(cast, grid):
    lead = tuple(cast.lead)
    rows, cols = cast.array.shape[len(lead):]
    steps = max(s for s in (1, 2, 4, 8, 16, 32) if s <= grid and rows % (s * _BF16_ROWS) == 0)
    blk = rows // steps
    step = lambda i: jnp.minimum(i, steps - 1)
    in_spec = pl.BlockSpec((None,) * len(lead) + (blk, cols), lambda i: lead + (step(i), 0))
    out_spec = pl.BlockSpec((blk, cols), lambda i: (step(i), 0))
    return in_spec, out_spec, jax.ShapeDtypeStruct((rows, cols), _BF)


def _call(body, *, name, grid, in_specs, args, out_specs, out_shape, scratch=(), casts=(), aliases=None):
    n_in, n_out, n_cast = len(in_specs), len(out_specs), len(casts)
    specs = [_cast_specs(c, grid) for c in casts]

    def wrapped(*refs):
        ins, refs = refs[:n_in], refs[n_in:]
        cast_in, refs = refs[:n_cast], refs[n_cast:]
        outs, refs = refs[:n_out], refs[n_out:]
        cast_out, scratch_refs = refs[:n_cast], refs[n_cast:]
        for src, dst in zip(cast_in, cast_out):
            dst[...] = src[...].astype(_BF)
        body(*ins, *outs, *scratch_refs)

    res = pl.pallas_call(
        wrapped,
        grid=(grid,),
        in_specs=list(in_specs) + [s[0] for s in specs],
        out_specs=list(out_specs) + [s[1] for s in specs],
        out_shape=list(out_shape) + [s[2] for s in specs],
        scratch_shapes=list(scratch),
        input_output_aliases=aliases or {},
        compiler_params=pltpu.CompilerParams(dimension_semantics=("arbitrary",), vmem_limit_bytes=_VMEM_LIMIT),
        name=name,
    )(*args, *[c.array for c in casts])
    return res[:n_out], res[n_out:]


def _ffn_body(*refs, split_in, final, n_prompt_tiles):
    refs = list(refs)
    i = pl.program_id(0)
    x_refs = [refs.pop(0) for _ in range(2 if split_in else 1)]
    g_ref, wg_ref, wu_ref, wd_ref = refs[:4]
    refs = refs[4:]
    is_prompt = i < n_prompt_tiles
    for r0 in range(0, _TM, _SUB):
        rows = slice(r0, r0 + _SUB)
        if split_in:
            x = jnp.where(is_prompt, x_refs[0][rows, :], x_refs[1][rows, :])
        else:
            x = x_refs[0][rows, :]
        h = _rms(x, g_ref[...]).astype(_BF)
        acc = None
        for c0, cn in _FFN_CHUNKS:
            a = _dot(h, wg_ref[:, c0:c0 + cn])
            b = _dot(h, wu_ref[:, c0:c0 + cn])
            act = (a * jax.nn.sigmoid(a) * b).astype(_BF)
            d = _dot(act, wd_ref[c0:c0 + cn, :])
            acc = d if acc is None else acc + d
        y = x + 0.5 * acc
        if final:
            fg_ref, yp_ref, ys_ref = refs
            y = _rms(y, fg_ref[...])
            to_prompt = jnp.broadcast_to(is_prompt, y.shape)
            pltpu.store(yp_ref.at[rows, :], y, mask=to_prompt)
            pltpu.store(ys_ref.at[rows, :], y, mask=jnp.logical_not(to_prompt))
        else:
            (o_ref,) = refs
            o_ref[rows, :] = y


def _ffn(xs, g, wg, wu, wd, *, n_prompt, n_sample, final_g=None, casts=()):
    d, dff = wg.shape
    npt, nst = n_prompt // _TM, n_sample // _TM
    split_in = len(xs) == 2
    final = final_g is not None
    row = lambda i: (i, 0)
    prow = lambda i: (jnp.minimum(i, npt - 1), 0)
    srow = lambda i: (jnp.maximum(i - npt, 0), 0)
    tile = (_TM, d)
    if split_in:
        in_specs = [pl.BlockSpec(tile, prow), pl.BlockSpec(tile, srow, pipeline_mode=pl.Buffered(1))]
    else:
        in_specs = [pl.BlockSpec(tile, row)]
    in_specs += [_resident((1, d)), _resident((d, dff)), _resident((d, dff)), _resident((dff, d))]
    args = list(xs) + [g.reshape(1, d), wg, wu, wd]
    if final:
        in_specs.append(_resident((1, d)))
        args.append(final_g.reshape(1, d))
        out_specs = [pl.BlockSpec(tile, prow), pl.BlockSpec(tile, srow)]
        out_shape = [jax.ShapeDtypeStruct((n_prompt, d), _F32), jax.ShapeDtypeStruct((n_sample, d), _F32)]
    else:
        out_specs = [pl.BlockSpec(tile, row)]
        out_shape = [jax.ShapeDtypeStruct((n_prompt + n_sample, d), _F32)]
    return _call(
        functools.partial(_ffn_body, split_in=split_in, final=final, n_prompt_tiles=npt),
        name="ffn_final" if final else ("ffn_first" if split_in else "ffn"),
        grid=npt + nst, in_specs=in_specs, args=args, out_specs=out_specs, out_shape=out_shape, casts=casts)


def _gelu(x):
    return 0.5 * x * (1.0 + lax.erf(x * (2.0 ** -0.5)))


def _even_body(*refs, sample, chunk_len, tiles_per_seq, sub):
    if sample:
        (x_ref, g_ref, win_ref, lng_ref, lnb_ref, sw_ref, sb_ref, cw_ref, wout_ref, b1_ref, b2_ref,
         o_ref, xg_ref, v_ref, yab_ref) = refs
    else:
        (x_ref, g_ref, win_ref, lng_ref, lnb_ref, sw_ref, sb_ref, cw_ref, wout_ref,
         o_ref, cs_ref, yab_ref, carry_ref) = refs

        @pl.when((pl.program_id(0) % tiles_per_seq) == 0)
        def _():
            carry_ref[...] = jnp.zeros_like(carry_ref)

    tm = x_ref.shape[0]
    d_a = _NH_A * _HD_A
    d_b = cw_ref.shape[1]
    o = 2 * d_a
    tt = lax.broadcasted_iota(jnp.int32, (_CHUNK, _CHUNK), 0)
    ss = lax.broadcasted_iota(jnp.int32, (_CHUNK, _CHUNK), 1)
    keep = (ss <= tt) & ((tt // chunk_len) == (ss // chunk_len))
    mix_w = [jnp.where(keep, sw_ref[hd], 0.0).astype(_BF) for hd in range(_NH_A)]
    cw = cw_ref[...]
    prev = None if sample else carry_ref[...]
    for r0 in range(0, tm, sub):
        rows = slice(r0, r0 + sub)
        x = x_ref[rows, :]
        h = _rms(x, g_ref[...]).astype(_BF)
        z = _dot(h, win_ref[...])
        u = _gelu(z[:, 0:d_a])
        vv = _gelu(z[:, d_a:2 * d_a])
        mu = jnp.mean(vv, axis=-1, keepdims=True)
        vc = vv - mu
        var = jnp.mean(vc * vc, axis=-1, keepdims=True)
        v = vc * lax.rsqrt(var + _EPS) * lng_ref[...] + lnb_ref[...]
        if sample:
            v_ref[rows, :] = v
        gate_b = z[:, o:o + d_b]
        gate_c = z[:, o + d_b:o + 2 * d_b]
        x_in = z[:, o + 2 * d_b:o + 3 * d_b]

        vb = v.astype(_BF)
        for hd in range(_NH_A):
            cols = slice(hd * _HD_A, (hd + 1) * _HD_A)
            for c0 in range(0, sub, _CHUNK):
                mixed = _dot(mix_w[hd], vb[c0:c0 + _CHUNK, cols]) + sb_ref[hd]
                yab_ref[r0 + c0:r0 + c0 + _CHUNK, cols] = (u[c0:c0 + _CHUNK, cols] * mixed).astype(_BF)

        xg = gate_c * x_in
        if sample:
            t8 = lax.broadcasted_iota(jnp.int32, (sub, 1), 0) % chunk_len
            back1 = jnp.where(t8 >= 1, pltpu.roll(xg, 1, axis=0), b1_ref[rows, :])
            back2 = jnp.where(t8 >= 2, pltpu.roll(xg, 2, axis=0), b2_ref[rows, :])
            xg_ref[rows, :] = xg
        else:
            ext = jnp.concatenate([prev, xg], axis=0)
            back1 = pltpu.roll(ext, 1, axis=0)[_SUBLANES:]
            back2 = pltpu.roll(ext, 2, axis=0)[_SUBLANES:]
            prev = xg[sub - _SUBLANES:]
        conv = back2 * cw[0:1] + back1 * cw[1:2] + xg * cw[2:3]
        yab_ref[rows, d_a:d_a + d_b] = (gate_b * conv).astype(_BF)
        o_ref[rows, :] = x + _dot(yab_ref[rows, :], wout_ref[...])
    if not sample:
        carry_ref[...] = prev
        cs_ref[0] = prev


def _even_mixer(xc, g, w_in, ln_g, ln_b, sgu_w, sgu_b, conv_w, w_out, *, row0, n_rows, seq_len,
                chunk_len, conv_state=None, casts=()):
    sample = conv_state is not None
    t, d = xc.shape
    d_in = w_in.shape[1]
    d_a = _NH_A * _HD_A
    d_b = conv_w.shape[1]
    tile0 = row0 // _TM
    n_tiles = n_rows // _TM
    row = lambda i: (tile0 + i, 0)
    reps = _CHUNK // chunk_len
    sw = jnp.tile(sgu_w[:, :chunk_len, :chunk_len], (1, reps, reps))
    sb = jnp.broadcast_to(jnp.tile(sgu_b[:, :chunk_len], (1, reps))[:, :, None], (_NH_A, _CHUNK, _HD_A))
    in_specs = [pl.BlockSpec((_TM, d), row), _resident((1, d)), _resident((d, d_in)), _resident((1, d_a)),
                _resident((1, d_a)), _resident(sw.shape), _resident(sb.shape), _resident(conv_w.shape),
                _resident(w_out.shape)]
    args = [xc, g.reshape(1, d), w_in, ln_g.reshape(1, d_a), ln_b.reshape(1, d_a), sw, sb, conv_w, w_out]
    scratch = [pltpu.VMEM((_TM, d_a + d_b), _BF)]
    if sample:
        n_seq = n_rows // chunk_len
        pad = lambda a: jnp.pad(a, ((0, 0), (0, chunk_len - a.shape[1]), (0, 0))).reshape(n_rows, d_b)
        b1 = pad(conv_state[:, 1:2])
        b2 = pad(conv_state)
        local = lambda i: (i, 0)
        in_specs += [pl.BlockSpec((_TM, d_b), local), pl.BlockSpec((_TM, d_b), local)]
        args += [b1, b2]
        out_specs = [pl.BlockSpec((_TM, d), row), pl.BlockSpec((_TM, d_b), local), pl.BlockSpec((_TM, d_a), local)]
        out_shape = [jax.ShapeDtypeStruct((t, d), _F32), jax.ShapeDtypeStruct((n_rows, d_b), _F32),
                     jax.ShapeDtypeStruct((n_rows, d_a), _F32)]
        tiles_per_seq = 1
    else:
        n_seq = n_rows // seq_len
        tiles_per_seq = seq_len // _TM
        out_specs = [pl.BlockSpec((_TM, d), row),
                     pl.BlockSpec((1, _SUBLANES, d_b), lambda i: (i // tiles_per_seq, 0, 0))]
        out_shape = [jax.ShapeDtypeStruct((t, d), _F32), jax.ShapeDtypeStruct((n_seq, _SUBLANES, d_b), _F32)]
        scratch.append(pltpu.VMEM((_SUBLANES, d_b), _F32))
    return _call(
        functools.partial(_even_body, sample=sample, chunk_len=chunk_len, tiles_per_seq=tiles_per_seq, sub=_SUB),
        name="even_sample" if sample else "even_prompt",
        grid=n_tiles, in_specs=in_specs, args=args, out_specs=out_specs, out_shape=out_shape, scratch=scratch,
        casts=casts, aliases={0: 0})


def _window_sums(ext):
    gc = ext.shape[1] // len(_POOL_WINDOWS)
    sums = []
    s = ext
    width = 1
    for w in _POOL_WINDOWS:
        while width < w:
            s = s + pltpu.roll(s, width, axis=0)
            width *= 2
        sums.append(s[:, 0:gc])
        s = s[:, gc:]
    return sums


def _odd_body(*refs, sample, seq_len, tiles_per_seq, sub):
    i = pl.program_id(0)
    if sample:
        (x_ref, g_ref, win_ref, pw_ref, psc_ref, wout_ref, ctx_ref, o_ref, ps_ref, y_ref, ext_ref) = refs
    else:
        (x_ref, g_ref, win_ref, pw_ref, psc_ref, wout_ref, o_ref, ps_ref, y_ref, carry_ref) = refs

        @pl.when((i % tiles_per_seq) == 0)
        def _():
            carry_ref[...] = jnp.zeros_like(carry_ref)

    tm, d = x_ref.shape
    ctx_rows = 2 * _SUBLANES
    gc = d // len(_POOL_WINDOWS)
    prev = None if sample else carry_ref[...]
    for r0 in range(0, tm, sub):
        rows = slice(r0, r0 + sub)
        x = x_ref[rows, :]
        h = _rms(x, g_ref[...]).astype(_BF)
        p = _dot(h, win_ref[...])
        t_in_tile = r0 + lax.broadcasted_iota(jnp.int32, (sub, 1), 0)
        if sample:
            nb = sub // seq_len
            b0 = r0 // seq_len
            lead = ctx_rows - _POOL_CTX
            ext_ref[b0:b0 + nb, 0:lead, :] = jnp.zeros((nb, lead, d), _F32)
            ext_ref[b0:b0 + nb, lead:ctx_rows, :] = ctx_ref[b0:b0 + nb]
            ext_ref[b0:b0 + nb, ctx_rows:ctx_rows + seq_len, :] = p.reshape(nb, seq_len, d)
            ext = ext_ref[b0:b0 + nb].reshape(nb * (ctx_rows + seq_len), d)
            pick = lambda s: s.reshape(nb, ctx_rows + seq_len, s.shape[1])[:, ctx_rows:, :].reshape(sub, s.shape[1])
            pos = _PAST_LEN + t_in_tile % seq_len
            ps_ref[b0:b0 + nb] = ext_ref[b0:b0 + nb, ctx_rows + seq_len - _POOL_CTX:, :]
        else:
            ext = jnp.concatenate([prev, p], axis=0)
            pick = lambda s: s[ctx_rows:]
            pos = (i % tiles_per_seq) * tm + t_in_tile
            prev = p[sub - ctx_rows:]
        for grp, (w, s) in enumerate(zip(_POOL_WINDOWS, _window_sums(ext))):
            cnt = jnp.minimum(pos + 1, w).astype(_F32)
            cols = slice(grp * gc, (grp + 1) * gc)
            dlt = pick(s) / cnt - p[:, cols]
            yg = _dot(dlt.astype(_BF), pw_ref[grp])
            y_ref[rows, cols] = (yg * psc_ref[:, cols]).astype(_BF)
        o_ref[rows, :] = x + _dot(y_ref[rows, :], wout_ref[...])
    if not sample:
        carry_ref[...] = prev
        ps_ref[0] = prev


def _odd_mixer(xc, g, w_in, pool_w, pool_scale, w_out, *, row0, n_rows, seq_len, pool_state=None):
    sample = pool_state is not None
    t, d = xc.shape
    ctx_rows = 2 * _SUBLANES
    tm = _TM // 4 if sample else _TM
    tile0 = row0 // tm
    n_tiles = n_rows // tm
    row = lambda i: (tile0 + i, 0)
    in_specs = [pl.BlockSpec((tm, d), row), _resident((1, d)), _resident((d, d)), _resident(pool_w.shape),
                _resident((1, d)), _resident((d, d))]
    args = [xc, g.reshape(1, d), w_in, pool_w, pool_scale.reshape(1, d), w_out]
    scratch = [pltpu.VMEM((tm, d), _BF)]
    if sample:
        nb = tm // seq_len
        in_specs.append(pl.BlockSpec((nb, _POOL_CTX, d), lambda i: (i, 0, 0)))
        args.append(pool_state)
        out_specs = [pl.BlockSpec((tm, d), row), pl.BlockSpec((nb, _POOL_CTX, d), lambda i: (i, 0, 0))]
        out_shape = [jax.ShapeDtypeStruct((t, d), _F32), jax.ShapeDtypeStruct(pool_state.shape, _F32)]
        scratch.append(pltpu.VMEM((nb, ctx_rows + seq_len, d), _F32))
        tiles_per_seq = 1
    else:
        n_seq = n_rows // seq_len
        tiles_per_seq = seq_len // tm
        out_specs = [pl.BlockSpec((tm, d), row),
                     pl.BlockSpec((1, ctx_rows, d), lambda i: (i // tiles_per_seq, 0, 0))]
        out_shape = [jax.ShapeDtypeStruct((t, d), _F32), jax.ShapeDtypeStruct((n_seq, ctx_rows, d), _F32)]
        scratch.append(pltpu.VMEM((ctx_rows, d), _F32))
    outs, _ = _call(
        functools.partial(_odd_body, sample=sample, seq_len=seq_len, tiles_per_seq=tiles_per_seq, sub=min(tm, _SUB)),
        name="odd_sample" if sample else "odd_prompt",
        grid=n_tiles, in_specs=in_specs, args=args, out_specs=out_specs, out_shape=out_shape, scratch=scratch,
        aliases={0: 0})
    return outs


def kernel(x_prompt, x_sample, state_conv, state_pool, norm_g, final_norm_g, ffn_w_gate, ffn_w_up, ffn_w_down,
           e_w_in, e_ln_g, e_ln_b, e_sgu_w, e_sgu_b, e_conv_w, e_w_out, o_w_in, o_pool_w, o_pool_scale, o_w_out):
    batch, seq, d = x_prompt.shape
    dec_batch, dec_seq, _ = x_sample.shape
    n_prompt, n_sample = batch * seq, dec_batch * dec_seq
    d_a = _NH_A * _HD_A
    d_b = e_conv_w.shape[-1]
    n_groups, gc = o_pool_w.shape[1:3]
    ffn = functools.partial(_ffn, n_prompt=n_prompt, n_sample=n_sample)
    ffn_casts = lambda layer, k: [_Cast(w, (layer, k)) for w in (ffn_w_gate, ffn_w_up, ffn_w_down)]

    w00 = [w[0, 0].astype(_BF) for w in (ffn_w_gate, ffn_w_up, ffn_w_down)]
    even_casts = [_Cast(e_w_in, (0,)), _Cast(e_w_out, (0,))]
    odd_casts = [_Cast(o_w_in, (0,)), _Cast(o_w_out, (0,)), _Cast(o_pool_w.reshape(-1, n_groups * gc, gc), (0,))]
    (xc,), (e_win, e_wout, *w01) = ffn(
        (x_prompt.reshape(n_prompt, d), x_sample.reshape(n_sample, d)), norm_g[0, 0], *w00,
        casts=even_casts + ffn_casts(0, 1))
    even = functools.partial(_even_mixer, g=norm_g[0, 1], w_in=e_win, ln_g=e_ln_g[0], ln_b=e_ln_b[0],
                             sgu_w=e_sgu_w[0], sgu_b=e_sgu_b[0], conv_w=e_conv_w[0], w_out=e_wout)
    (xc, conv_p), _ = even(xc, row0=0, n_rows=n_prompt, seq_len=seq, chunk_len=_CHUNK)
    (xc, xg_s, v_s), _ = even(xc, row0=n_prompt, n_rows=n_sample, seq_len=dec_seq, chunk_len=dec_seq,
                              conv_state=state_conv[0])
    (xc,), (o_win, o_wout, o_pw, *w10) = ffn((xc,), norm_g[0, 2], *w01, casts=odd_casts + ffn_casts(1, 0))

    (xc,), w11 = ffn((xc,), norm_g[1, 0], *w10, casts=ffn_casts(1, 1))
    odd = functools.partial(_odd_mixer, g=norm_g[1, 1], w_in=o_win, pool_w=o_pw.reshape(n_groups, gc, gc),
                            pool_scale=o_pool_scale[0], w_out=o_wout)
    xc, pool_p = odd(xc, row0=0, n_rows=n_prompt, seq_len=seq)
    xc, pool_s = odd(xc, row0=n_prompt, n_rows=n_sample, seq_len=dec_seq, pool_state=state_pool[0])
    (y_p, y_s), _ = ffn((xc,), norm_g[1, 2], *w11, final_g=final_norm_g)

    keep = _CONV_W - 1
    conv_prompt = conv_p[:, _SUBLANES - keep:][None]
    conv_sample = xg_s.reshape(dec_batch, dec_seq, d_b)[:, dec_seq - keep:][None]
    chunk_v_sample = v_s.reshape(1, dec_batch, dec_seq, d_a)
    pool_prompt = pool_p[:, pool_p.shape[1] - _POOL_CTX:][None]
    pool_sample = pool_s[None]
    return (y_p.reshape(batch, seq, d), y_s.reshape(dec_batch, dec_seq, d), conv_prompt, conv_sample,
            chunk_v_sample, pool_prompt, pool_sample)
```

```python
import collections
import functools

import jax
import jax.numpy as jnp
from jax import lax
from jax.experimental import pallas as pl
from jax.experimental.pallas import tpu as pltpu

_CHUNK = 128
_NH_A = 4
_HD_A = 128
_CONV_W = 3
_POOL_WINDOWS = (2, 4, 8, 16)
_POOL_CTX = max(_POOL_WINDOWS) - 1
_PAST_LEN = 16384
_EPS = 1e-6

_SUBLANES = 8
_BF16_ROWS = 16
_VMEM_LIMIT = 56 * 1024 * 1024

_TM = 1024
_SUB = 512
_FFN_CHUNKS = ((0, 1536), (1536, 1280))

_BF = jnp.bfloat16
_F32 = jnp.float32


def _rms(x, g):
    return x * lax.rsqrt(jnp.mean(x * x, axis=-1, keepdims=True) + _EPS) * g


def _dot(a, b):
    return jnp.dot(a, b, preferred_element_type=_F32)


def _resident(shape):
    nd = len(shape)
    return pl.BlockSpec(shape, lambda i: (0,) * nd, pipeline_mode=pl.Buffered(1))


_Cast = collections.namedtuple("_Cast", "array lead")


def _cast_specs(cast, grid):
    lead = tuple(cast.lead)
    rows, cols = cast.array.shape[len(lead):]
    steps = max(s for s in (1, 2, 4, 8, 16, 32) if s <= grid and rows % (s * _BF16_ROWS) == 0)
    blk = rows // steps
    step = lambda i: jnp.minimum(i, steps - 1)
    in_spec = pl.BlockSpec((None,) * len(lead) + (blk, cols), lambda i: lead + (step(i), 0))
    out_spec = pl.BlockSpec((blk, cols), lambda i: (step(i), 0))
    return in_spec, out_spec, jax.ShapeDtypeStruct((rows, cols), _BF)


def _call(body, *, name, grid, in_specs, args, out_specs, out_shape, scratch=(), casts=(), aliases=None):
    n_in, n_out, n_cast = len(in_specs), len(out_specs), len(casts)
    specs = [_cast_specs(c, grid) for c in casts]

    def wrapped(*refs):
        ins, refs = refs[:n_in], refs[n_in:]
        cast_in, refs = refs[:n_cast], refs[n_cast:]
        outs, refs = refs[:n_out], refs[n_out:]
        cast_out, scratch_refs = refs[:n_cast], refs[n_cast:]
        for src, dst in zip(cast_in, cast_out):
            dst[...] = src[...].astype(_BF)
        body(*ins, *outs, *scratch_refs)

    res = pl.pallas_call(
        wrapped,
        grid=(grid,),
        in_specs=list(in_specs) + [s[0] for s in specs],
        out_specs=list(out_specs) + [s[1] for s in specs],
        out_shape=list(out_shape) + [s[2] for s in specs],
        scratch_shapes=list(scratch),
        input_output_aliases=aliases or {},
        compiler_params=pltpu.CompilerParams(dimension_semantics=("arbitrary",), vmem_limit_bytes=_VMEM_LIMIT),
        name=name,
    )(*args, *[c.array for c in casts])
    return res[:n_out], res[n_out:]


def _ffn_body(*refs, split_in, final, n_prompt_tiles):
    refs = list(refs)
    i = pl.program_id(0)
    x_refs = [refs.pop(0) for _ in range(2 if split_in else 1)]
    g_ref, wg_ref, wu_ref, wd_ref = refs[:4]
    refs = refs[4:]
    is_prompt = i < n_prompt_tiles
    for r0 in range(0, _TM, _SUB):
        rows = slice(r0, r0 + _SUB)
        if split_in:
            x = jnp.where(is_prompt, x_refs[0][rows, :], x_refs[1][rows, :])
        else:
            x = x_refs[0][rows, :]
        h = _rms(x, g_ref[...]).astype(_BF)
        acc = None
        for c0, cn in _FFN_CHUNKS:
            a = _dot(h, wg_ref[:, c0:c0 + cn])
            b = _dot(h, wu_ref[:, c0:c0 + cn])
            act = (a * jax.nn.sigmoid(a) * b).astype(_BF)
            d = _dot(act, wd_ref[c0:c0 + cn, :])
            acc = d if acc is None else acc + d
        y = x + 0.5 * acc
        if final:
            fg_ref, yp_ref, ys_ref = refs
            y = _rms(y, fg_ref[...])
            to_prompt = jnp.broadcast_to(is_prompt, y.shape)
            pltpu.store(yp_ref.at[rows, :], y, mask=to_prompt)
            pltpu.store(ys_ref.at[rows, :], y, mask=jnp.logical_not(to_prompt))
        else:
            (o_ref,) = refs
            o_ref[rows, :] = y


def _ffn(xs, g, wg, wu, wd, *, n_prompt, n_sample, final_g=None, casts=()):
    d, dff = wg.shape
    npt, nst = n_prompt // _TM, n_sample // _TM
    split_in = len(xs) == 2
    final = final_g is not None
    row = lambda i: (i, 0)
    prow = lambda i: (jnp.minimum(i, npt - 1), 0)
    srow = lambda i: (jnp.maximum(i - npt, 0), 0)
    tile = (_TM, d)
    if split_in:
        in_specs = [pl.BlockSpec(tile, prow), pl.BlockSpec(tile, srow, pipeline_mode=pl.Buffered(1))]
    else:
        in_specs = [pl.BlockSpec(tile, row)]
    in_specs += [_resident((1, d)), _resident((d, dff)), _resident((d, dff)), _resident((dff, d))]
    args = list(xs) + [g.reshape(1, d), wg, wu, wd]
    if final:
        in_specs.append(_resident((1, d)))
        args.append(final_g.reshape(1, d))
        out_specs = [pl.BlockSpec(tile, prow), pl.BlockSpec(tile, srow)]
        out_shape = [jax.ShapeDtypeStruct((n_prompt, d), _F32), jax.ShapeDtypeStruct((n_sample, d), _F32)]
    else:
        out_specs = [pl.BlockSpec(tile, row)]
        out_shape = [jax.ShapeDtypeStruct((n_prompt + n_sample, d), _F32)]
    return _call(
        functools.partial(_ffn_body, split_in=split_in, final=final, n_prompt_tiles=npt),
        name="ffn_final" if final else ("ffn_first" if split_in else "ffn"),
        grid=npt + nst, in_specs=in_specs, args=args, out_specs=out_specs, out_shape=out_shape, casts=casts)


def _gelu(x):
    return 0.5 * x * (1.0 + lax.erf(x * (2.0 ** -0.5)))


def _even_body(*refs, sample, chunk_len, tiles_per_seq, sub):
    if sample:
        (x_ref, g_ref, win_ref, lng_ref, lnb_ref, sw_ref, sb_ref, cw_ref, wout_ref, b1_ref, b2_ref,
         o_ref, xg_ref, v_ref, yab_ref) = refs
    else:
        (x_ref, g_ref, win_ref, lng_ref, lnb_ref, sw_ref, sb_ref, cw_ref, wout_ref,
         o_ref, cs_ref, yab_ref, carry_ref) = refs

        @pl.when((pl.program_id(0) % tiles_per_seq) == 0)
        def _():
            carry_ref[...] = jnp.zeros_like(carry_ref)

    tm = x_ref.shape[0]
    d_a = _NH_A * _HD_A
    d_b = cw_ref.shape[1]
    o = 2 * d_a
    tt = lax.broadcasted_iota(jnp.int32, (_CHUNK, _CHUNK), 0)
    ss = lax.broadcasted_iota(jnp.int32, (_CHUNK, _CHUNK), 1)
    keep = (ss <= tt) & ((tt // chunk_len) == (ss // chunk_len))
    mix_w = [jnp.where(keep, sw_ref[hd], 0.0).astype(_BF) for hd in range(_NH_A)]
    cw = cw_ref[...]
    prev = None if sample else carry_ref[...]
    for r0 in range(0, tm, sub):
        rows = slice(r0, r0 + sub)
        x = x_ref[rows, :]
        h = _rms(x, g_ref[...]).astype(_BF)
        z = _dot(h, win_ref[...])
        u = _gelu(z[:, 0:d_a])
        vv = _gelu(z[:, d_a:2 * d_a])
        mu = jnp.mean(vv, axis=-1, keepdims=True)
        vc = vv - mu
        var = jnp.mean(vc * vc, axis=-1, keepdims=True)
        v = vc * lax.rsqrt(var + _EPS) * lng_ref[...] + lnb_ref[...]
        if sample:
            v_ref[rows, :] = v
        gate_b = z[:, o:o + d_b]
        gate_c = z[:, o + d_b:o + 2 * d_b]
        x_in = z[:, o + 2 * d_b:o + 3 * d_b]

        vb = v.astype(_BF)
        for hd in range(_NH_A):
            cols = slice(hd * _HD_A, (hd + 1) * _HD_A)
            for c0 in range(0, sub, _CHUNK):
                mixed = _dot(mix_w[hd], vb[c0:c0 + _CHUNK, cols]) + sb_ref[hd]
                yab_ref[r0 + c0:r0 + c0 + _CHUNK, cols] = (u[c0:c0 + _CHUNK, cols] * mixed).astype(_BF)

        xg = gate_c * x_in
        if sample:
            t8 = lax.broadcasted_iota(jnp.int32, (sub, 1), 0) % chunk_len
            back1 = jnp.where(t8 >= 1, pltpu.roll(xg, 1, axis=0), b1_ref[rows, :])
            back2 = jnp.where(t8 >= 2, pltpu.roll(xg, 2, axis=0), b2_ref[rows, :])
            xg_ref[rows, :] = xg
        else:
            ext = jnp.concatenate([prev, xg], axis=0)
            back1 = pltpu.roll(ext, 1, axis=0)[_SUBLANES:]
            back2 = pltpu.roll(ext, 2, axis=0)[_SUBLANES:]
            prev = xg[sub - _SUBLANES:]
        conv = back2 * cw[0:1] + back1 * cw[1:2] + xg * cw[2:3]
        yab_ref[rows, d_a:d_a + d_b] = (gate_b * conv).astype(_BF)
        o_ref[rows, :] = x + _dot(yab_ref[rows, :], wout_ref[...])
    if not sample:
        carry_ref[...] = prev
        cs_ref[0] = prev


def _even_mixer(xc, g, w_in, ln_g, ln_b, sgu_w, sgu_b, conv_w, w_out, *, row0, n_rows, seq_len,
                chunk_len, conv_state=None, casts=()):
    sample = conv_state is not None
    t, d = xc.shape
    d_in = w_in.shape[1]
    d_a = _NH_A * _HD_A
    d_b = conv_w.shape[1]
    tile0 = row0 // _TM
    n_tiles = n_rows // _TM
    row = lambda i: (tile0 + i, 0)
    reps = _CHUNK // chunk_len
    sw = jnp.tile(sgu_w[:, :chunk_len, :chunk_len], (1, reps, reps))
    sb = jnp.broadcast_to(jnp.tile(sgu_b[:, :chunk_len], (1, reps))[:, :, None], (_NH_A, _CHUNK, _HD_A))
    in_specs = [pl.BlockSpec((_TM, d), row), _resident((1, d)), _resident((d, d_in)), _resident((1, d_a)),
                _resident((1, d_a)), _resident(sw.shape), _resident(sb.shape), _resident(conv_w.shape),
                _resident(w_out.shape)]
    args = [xc, g.reshape(1, d), w_in, ln_g.reshape(1, d_a), ln_b.reshape(1, d_a), sw, sb, conv_w, w_out]
    scratch = [pltpu.VMEM((_TM, d_a + d_b), _BF)]
    if sample:
        n_seq = n_rows // chunk_len
        pad = lambda a: jnp.pad(a, ((0, 0), (0, chunk_len - a.shape[1]), (0, 0))).reshape(n_rows, d_b)
        b1 = pad(conv_state[:, 1:2])
        b2 = pad(conv_state)
        local = lambda i: (i, 0)
        in_specs += [pl.BlockSpec((_TM, d_b), local), pl.BlockSpec((_TM, d_b), local)]
        args += [b1, b2]
        out_specs = [pl.BlockSpec((_TM, d), row), pl.BlockSpec((_TM, d_b), local), pl.BlockSpec((_TM, d_a), local)]
        out_shape = [jax.ShapeDtypeStruct((t, d), _F32), jax.ShapeDtypeStruct((n_rows, d_b), _F32),
                     jax.ShapeDtypeStruct((n_rows, d_a), _F32)]
        tiles_per_seq = 1
    else:
        n_seq = n_rows // seq_len
        tiles_per_seq = seq_len // _TM
        out_specs = [pl.BlockSpec((_TM, d), row),
                     pl.BlockSpec((1, _SUBLANES, d_b), lambda i: (i // tiles_per_seq, 0, 0))]
        out_shape = [jax.ShapeDtypeStruct((t, d), _F32), jax.ShapeDtypeStruct((n_seq, _SUBLANES, d_b), _F32)]
        scratch.append(pltpu.VMEM((_SUBLANES, d_b), _F32))
    return _call(
        functools.partial(_even_body, sample=sample, chunk_len=chunk_len, tiles_per_seq=tiles_per_seq, sub=_SUB),
        name="even_sample" if sample else "even_prompt",
        grid=n_tiles, in_specs=in_specs, args=args, out_specs=out_specs, out_shape=out_shape, scratch=scratch,
        casts=casts, aliases={0: 0})


def _window_sums(ext):
    gc = ext.shape[1] // len(_POOL_WINDOWS)
    sums = []
    s = ext
    width = 1
    for w in _POOL_WINDOWS:
        while width < w:
            s = s + pltpu.roll(s, width, axis=0)
            width *= 2
        sums.append(s[:, 0:gc])
        s = s[:, gc:]
    return sums


def _odd_body(*refs, sample, seq_len, tiles_per_seq, sub):
    i = pl.program_id(0)
    if sample:
        (x_ref, g_ref, win_ref, pw_ref, psc_ref, wout_ref, ctx_ref, o_ref, ps_ref, y_ref, ext_ref) = refs
    else:
        (x_ref, g_ref, win_ref, pw_ref, psc_ref, wout_ref, o_ref, ps_ref, y_ref, carry_ref) = refs

        @pl.when((i % tiles_per_seq) == 0)
        def _():
            carry_ref[...] = jnp.zeros_like(carry_ref)

    tm, d = x_ref.shape
    ctx_rows = 2 * _SUBLANES
    gc = d // len(_POOL_WINDOWS)
    prev = None if sample else carry_ref[...]
    for r0 in range(0, tm, sub):
        rows = slice(r0, r0 + sub)
        x = x_ref[rows, :]
        h = _rms(x, g_ref[...]).astype(_BF)
        p = _dot(h, win_ref[...])
        t_in_tile = r0 + lax.broadcasted_iota(jnp.int32, (sub, 1), 0)
        if sample:
            nb = sub // seq_len
            b0 = r0 // seq_len
            lead = ctx_rows - _POOL_CTX
            ext_ref[b0:b0 + nb, 0:lead, :] = jnp.zeros((nb, lead, d), _F32)
            for k in range(_POOL_CTX):
                ext_ref[b0:b0 + nb, lead + k, :] = ctx_ref[k, b0:b0 + nb, :]
            ext_ref[b0:b0 + nb, ctx_rows:ctx_rows + seq_len, :] = p.reshape(nb, seq_len, d)
            ext = ext_ref[b0:b0 + nb].reshape(nb * (ctx_rows + seq_len), d)
            pick = lambda s: s.reshape(nb, ctx_rows + seq_len, s.shape[1])[:, ctx_rows:, :].reshape(sub, s.shape[1])
            pos = _PAST_LEN + t_in_tile % seq_len
            for k in range(_POOL_CTX):
                ps_ref[k, b0:b0 + nb, :] = ext_ref[b0:b0 + nb, ctx_rows + seq_len - _POOL_CTX + k, :]
        else:
            ext = jnp.concatenate([prev, p], axis=0)
            pick = lambda s: s[ctx_rows:]
            pos = (i % tiles_per_seq) * tm + t_in_tile
            prev = p[sub - ctx_rows:]
        for grp, (w, s) in enumerate(zip(_POOL_WINDOWS, _window_sums(ext))):
            cnt = jnp.minimum(pos + 1, w).astype(_F32)
            cols = slice(grp * gc, (grp + 1) * gc)
            dlt = pick(s) / cnt - p[:, cols]
            yg = _dot(dlt.astype(_BF), pw_ref[grp])
            y_ref[rows, cols] = (yg * psc_ref[:, cols]).astype(_BF)
        o_ref[rows, :] = x + _dot(y_ref[rows, :], wout_ref[...])
    if not sample:
        carry_ref[...] = prev
        ps_ref[0] = prev


def _odd_mixer(xc, g, w_in, pool_w, pool_scale, w_out, *, row0, n_rows, seq_len, pool_state=None):
    sample = pool_state is not None
    t, d = xc.shape
    ctx_rows = 2 * _SUBLANES
    tm = _TM // 4 if sample else _TM
    tile0 = row0 // tm
    n_tiles = n_rows // tm
    row = lambda i: (tile0 + i, 0)
    in_specs = [pl.BlockSpec((tm, d), row), _resident((1, d)), _resident((d, d)), _resident(pool_w.shape),
                _resident((1, d)), _resident((d, d))]
    args = [xc, g.reshape(1, d), w_in, pool_w, pool_scale.reshape(1, d), w_out]
    scratch = [pltpu.VMEM((tm, d), _BF)]
    if sample:
        nb = tm // seq_len
        ctx_spec = pl.BlockSpec((_POOL_CTX, nb, d), lambda i: (0, i, 0))
        in_specs.append(ctx_spec)
        args.append(pool_state)
        out_specs = [pl.BlockSpec((tm, d), row), ctx_spec]
        out_shape = [jax.ShapeDtypeStruct((t, d), _F32), jax.ShapeDtypeStruct(pool_state.shape, _F32)]
        scratch.append(pltpu.VMEM((nb, ctx_rows + seq_len, d), _F32))
        tiles_per_seq = 1
    else:
        n_seq = n_rows // seq_len
        tiles_per_seq = seq_len // tm
        out_specs = [pl.BlockSpec((tm, d), row),
                     pl.BlockSpec((1, ctx_rows, d), lambda i: (i // tiles_per_seq, 0, 0))]
        out_shape = [jax.ShapeDtypeStruct((t, d), _F32), jax.ShapeDtypeStruct((n_seq, ctx_rows, d), _F32)]
        scratch.append(pltpu.VMEM((ctx_rows, d), _F32))
    outs, _ = _call(
        functools.partial(_odd_body, sample=sample, seq_len=seq_len, tiles_per_seq=tiles_per_seq, sub=min(tm, _SUB)),
        name="odd_sample" if sample else "odd_prompt",
        grid=n_tiles, in_specs=in_specs, args=args, out_specs=out_specs, out_shape=out_shape, scratch=scratch,
        aliases={0: 0})
    return outs


def kernel(x_prompt, x_sample, state_conv, state_pool, norm_g, final_norm_g, ffn_w_gate, ffn_w_up, ffn_w_down,
           e_w_in, e_ln_g, e_ln_b, e_sgu_w, e_sgu_b, e_conv_w, e_w_out, o_w_in, o_pool_w, o_pool_scale, o_w_out):
    batch, seq, d = x_prompt.shape
    dec_batch, dec_seq, _ = x_sample.shape
    n_prompt, n_sample = batch * seq, dec_batch * dec_seq
    d_a = _NH_A * _HD_A
    d_b = e_conv_w.shape[-1]
    n_groups, gc = o_pool_w.shape[1:3]
    ffn = functools.partial(_ffn, n_prompt=n_prompt, n_sample=n_sample)
    ffn_casts = lambda layer, k: [_Cast(w, (layer, k)) for w in (ffn_w_gate, ffn_w_up, ffn_w_down)]

    w00 = [w[0, 0].astype(_BF) for w in (ffn_w_gate, ffn_w_up, ffn_w_down)]
    even_casts = [_Cast(e_w_in, (0,)), _Cast(e_w_out, (0,))]
    odd_casts = [_Cast(o_w_in, (0,)), _Cast(o_w_out, (0,)), _Cast(o_pool_w.reshape(-1, n_groups * gc, gc), (0,))]
    (xc,), (e_win, e_wout, *w01) = ffn(
        (x_prompt.reshape(n_prompt, d), x_sample.reshape(n_sample, d)), norm_g[0, 0], *w00,
        casts=even_casts + ffn_casts(0, 1))
    even = functools.partial(_even_mixer, g=norm_g[0, 1], w_in=e_win, ln_g=e_ln_g[0], ln_b=e_ln_b[0],
                             sgu_w=e_sgu_w[0], sgu_b=e_sgu_b[0], conv_w=e_conv_w[0], w_out=e_wout)
    (xc, conv_p), _ = even(xc, row0=0, n_rows=n_prompt, seq_len=seq, chunk_len=_CHUNK)
    (xc, xg_s, v_s), _ = even(xc, row0=n_prompt, n_rows=n_sample, seq_len=dec_seq, chunk_len=dec_seq,
                              conv_state=state_conv[0])
    (xc,), (o_win, o_wout, o_pw, *w10) = ffn((xc,), norm_g[0, 2], *w01, casts=odd_casts + ffn_casts(1, 0))

    (xc,), w11 = ffn((xc,), norm_g[1, 0], *w10, casts=ffn_casts(1, 1))
    odd = functools.partial(_odd_mixer, g=norm_g[1, 1], w_in=o_win, pool_w=o_pw.reshape(n_groups, gc, gc),
                            pool_scale=o_pool_scale[0], w_out=o_wout)
    xc, pool_p = odd(xc, row0=0, n_rows=n_prompt, seq_len=seq)
    xc, pool_s = odd(xc, row0=n_prompt, n_rows=n_sample, seq_len=dec_seq,
                     pool_state=jnp.transpose(state_pool[0], (1, 0, 2)))
    (y_p, y_s), _ = ffn((xc,), norm_g[1, 2], *w11, final_g=final_norm_g)

    keep = _CONV_W - 1
    conv_prompt = conv_p[:, _SUBLANES - keep:][None]
    conv_sample = xg_s.reshape(dec_batch, dec_seq, d_b)[:, dec_seq - keep:][None]
    chunk_v_sample = v_s.reshape(1, dec_batch, dec_seq, d_a)
    pool_prompt = pool_p[:, pool_p.shape[1] - _POOL_CTX:][None]
    pool_sample = jnp.transpose(pool_s, (1, 0, 2))[None]
    return (y_p.reshape(batch, seq, d), y_s.reshape(dec_batch, dec_seq, d), conv_prompt, conv_sample,
            chunk_v_sample, pool_prompt, pool_sample)
```

```python
import collections
import functools

import jax
import jax.numpy as jnp
from jax import lax
from jax.experimental import pallas as pl
from jax.experimental.pallas import tpu as pltpu

_CHUNK = 128
_NH_A = 4
_HD_A = 128
_CONV_W = 3
_POOL_WINDOWS = (2, 4, 8, 16)
_POOL_CTX = max(_POOL_WINDOWS) - 1
_PAST_LEN = 16384
_EPS = 1e-6

_SUBLANES = 8
_BF16_ROWS = 16
_VMEM_LIMIT = 56 * 1024 * 1024

_TM = 1024
_SUB = 512
_FFN_CHUNKS = ((0, 1536), (1536, 1280))
_CAST_CHUNKS = 16

_BF = jnp.bfloat16
_F32 = jnp.float32


def _rms(x, g):
    return x * lax.rsqrt(jnp.mean(x * x, axis=-1, keepdims=True) + _EPS) * g


def _dot(a, b):
    return jnp.dot(a, b, preferred_element_type=_F32)


def _resident(shape):
    nd = len(shape)
    return pl.BlockSpec(shape, lambda i: (0,) * nd, pipeline_mode=pl.Buffered(1))


_Cast = collections.namedtuple("_Cast", "array lead")


def _cast_specs(cast, grid):
    lead = tuple(cast.lead)
    rows, cols = cast.array.shape[len(lead):]
    steps = max(s for s in (1, 2, 4, 8, 16, 32) if s <= grid and rows % (s * _BF16_ROWS) == 0)
    blk = rows // steps
    step = lambda i: jnp.minimum(i, steps - 1)
    in_spec = pl.BlockSpec((None,) * len(lead) + (blk, cols), lambda i: lead + (step(i), 0))
    out_spec = pl.BlockSpec((blk, cols), lambda i: (step(i), 0))
    return in_spec, out_spec, jax.ShapeDtypeStruct((rows, cols), _BF)


def _call(body, *, name, grid, in_specs, args, out_specs, out_shape, scratch=(), casts=(), aliases=None):
    n_in, n_out, n_cast = len(in_specs), len(out_specs), len(casts)
    specs = [_cast_specs(c, grid) for c in casts]

    def wrapped(*refs):
        ins, refs = refs[:n_in], refs[n_in:]
        cast_in, refs = refs[:n_cast], refs[n_cast:]
        outs, refs = refs[:n_out], refs[n_out:]
        cast_out, scratch_refs = refs[:n_cast], refs[n_cast:]
        for src, dst in zip(cast_in, cast_out):
            dst[...] = src[...].astype(_BF)
        body(*ins, *outs, *scratch_refs)

    res = pl.pallas_call(
        wrapped,
        grid=(grid,),
        in_specs=list(in_specs) + [s[0] for s in specs],
        out_specs=list(out_specs) + [s[1] for s in specs],
        out_shape=list(out_shape) + [s[2] for s in specs],
        scratch_shapes=list(scratch),
        input_output_aliases=aliases or {},
        compiler_params=pltpu.CompilerParams(dimension_semantics=("arbitrary",), vmem_limit_bytes=_VMEM_LIMIT),
        name=name,
    )(*args, *[c.array for c in casts])
    return res[:n_out], res[n_out:]


def _load_cast(src, dst, stage, sem):
    rows = stage.shape[1]
    n = dst.shape[0] // rows
    copy = lambda c: pltpu.make_async_copy(src.at[pl.ds(c * rows, rows), :], stage.at[c % 2], sem.at[c % 2])
    copy(0).start()
    for c in range(n):
        if c + 1 < n:
            copy(c + 1).start()
        copy(c).wait()
        dst[c * rows:(c + 1) * rows, :] = stage[c % 2].astype(_BF)


def _ffn_body(*refs, split_in, final, n_prompt_tiles, f32_lead):
    refs = list(refs)
    i = pl.program_id(0)
    x_refs = [refs.pop(0) for _ in range(2 if split_in else 1)]
    g_ref, wg_ref, wu_ref, wd_ref = refs[:4]
    refs = refs[4:]
    if f32_lead is not None:
        *refs, wg_bf, wu_bf, wd_bf, stage_in, stage_out, sem = refs

        @pl.when(i == 0)
        def _():
            _load_cast(wg_ref.at[f32_lead], wg_bf, stage_in, sem)
            _load_cast(wu_ref.at[f32_lead], wu_bf, stage_in, sem)
            _load_cast(wd_ref.at[f32_lead], wd_bf, stage_out, sem)

        wg_ref, wu_ref, wd_ref = wg_bf, wu_bf, wd_bf
    is_prompt = i < n_prompt_tiles
    for r0 in range(0, _TM, _SUB):
        rows = slice(r0, r0 + _SUB)
        if split_in:
            x = jnp.where(is_prompt, x_refs[0][rows, :], x_refs[1][rows, :])
        else:
            x = x_refs[0][rows, :]
        h = _rms(x, g_ref[...]).astype(_BF)
        acc = None
        for c0, cn in _FFN_CHUNKS:
            a = _dot(h, wg_ref[:, c0:c0 + cn])
            b = _dot(h, wu_ref[:, c0:c0 + cn])
            act = (a * jax.nn.sigmoid(a) * b).astype(_BF)
            d = _dot(act, wd_ref[c0:c0 + cn, :])
            acc = d if acc is None else acc + d
        y = x + 0.5 * acc
        if final:
            fg_ref, yp_ref, ys_ref = refs
            y = _rms(y, fg_ref[...])
            to_prompt = jnp.broadcast_to(is_prompt, y.shape)
            pltpu.store(yp_ref.at[rows, :], y, mask=to_prompt)
            pltpu.store(ys_ref.at[rows, :], y, mask=jnp.logical_not(to_prompt))
        else:
            (o_ref,) = refs
            o_ref[rows, :] = y


def _ffn(xs, g, wg, wu, wd, *, n_prompt, n_sample, final_g=None, casts=(), f32_lead=None):
    d, dff = wg.shape[-2:]
    npt, nst = n_prompt // _TM, n_sample // _TM
    split_in = len(xs) == 2
    final = final_g is not None
    row = lambda i: (i, 0)
    prow = lambda i: (jnp.minimum(i, npt - 1), 0)
    srow = lambda i: (jnp.maximum(i - npt, 0), 0)
    tile = (_TM, d)
    if split_in:
        in_specs = [pl.BlockSpec(tile, prow), pl.BlockSpec(tile, srow, pipeline_mode=pl.Buffered(1))]
    else:
        in_specs = [pl.BlockSpec(tile, row)]
    scratch = []
    if f32_lead is None:
        in_specs += [_resident((1, d)), _resident((d, dff)), _resident((d, dff)), _resident((dff, d))]
    else:
        in_specs += [_resident((1, d))] + [pl.BlockSpec(memory_space=pl.ANY)] * 3
        scratch = [pltpu.VMEM((d, dff), _BF), pltpu.VMEM((d, dff), _BF), pltpu.VMEM((dff, d), _BF),
                   pltpu.VMEM((2, d // _CAST_CHUNKS, dff), _F32), pltpu.VMEM((2, dff // _CAST_CHUNKS, d), _F32),
                   pltpu.SemaphoreType.DMA((2,))]
    args = list(xs) + [g.reshape(1, d), wg, wu, wd]
    if final:
        in_specs.append(_resident((1, d)))
        args.append(final_g.reshape(1, d))
        out_specs = [pl.BlockSpec(tile, prow), pl.BlockSpec(tile, srow)]
        out_shape = [jax.ShapeDtypeStruct((n_prompt, d), _F32), jax.ShapeDtypeStruct((n_sample, d), _F32)]
    else:
        out_specs = [pl.BlockSpec(tile, row)]
        out_shape = [jax.ShapeDtypeStruct((n_prompt + n_sample, d), _F32)]
    return _call(
        functools.partial(_ffn_body, split_in=split_in, final=final, n_prompt_tiles=npt, f32_lead=f32_lead),
        name="ffn_final" if final else ("ffn_first" if split_in else "ffn"),
        grid=npt + nst, in_specs=in_specs, args=args, out_specs=out_specs, out_shape=out_shape, scratch=scratch,
        casts=casts)


def _gelu(x):
    return 0.5 * x * (1.0 + lax.erf(x * (2.0 ** -0.5)))


def _even_body(*refs, sample, chunk_len, tiles_per_seq, sub):
    if sample:
        (x_ref, g_ref, win_ref, lng_ref, lnb_ref, sw_ref, sb_ref, cw_ref, wout_ref, b1_ref, b2_ref,
         o_ref, xg_ref, v_ref, yab_ref) = refs
    else:
        (x_ref, g_ref, win_ref, lng_ref, lnb_ref, sw_ref, sb_ref, cw_ref, wout_ref,
         o_ref, cs_ref, yab_ref, carry_ref) = refs

        @pl.when((pl.program_id(0) % tiles_per_seq) == 0)
        def _():
            carry_ref[...] = jnp.zeros_like(carry_ref)

    tm = x_ref.shape[0]
    d_a = _NH_A * _HD_A
    d_b = cw_ref.shape[1]
    o = 2 * d_a
    tt = lax.broadcasted_iota(jnp.int32, (_CHUNK, _CHUNK), 0)
    ss = lax.broadcasted_iota(jnp.int32, (_CHUNK, _CHUNK), 1)
    keep = (ss <= tt) & ((tt // chunk_len) == (ss // chunk_len))
    mix_w = [jnp.where(keep, sw_ref[hd], 0.0).astype(_BF) for hd in range(_NH_A)]
    cw = cw_ref[...]
    prev = None if sample else carry_ref[...]
    for r0 in range(0, tm, sub):
        rows = slice(r0, r0 + sub)
        x = x_ref[rows, :]
        h = _rms(x, g_ref[...]).astype(_BF)
        z = _dot(h, win_ref[...])
        u = _gelu(z[:, 0:d_a])
        vv = _gelu(z[:, d_a:2 * d_a])
        mu = jnp.mean(vv, axis=-1, keepdims=True)
        vc = vv - mu
        var = jnp.mean(vc * vc, axis=-1, keepdims=True)
        v = vc * lax.rsqrt(var + _EPS) * lng_ref[...] + lnb_ref[...]
        if sample:
            v_ref[rows, :] = v
        gate_b = z[:, o:o + d_b]
        gate_c = z[:, o + d_b:o + 2 * d_b]
        x_in = z[:, o + 2 * d_b:o + 3 * d_b]

        vb = v.astype(_BF)
        for hd in range(_NH_A):
            cols = slice(hd * _HD_A, (hd + 1) * _HD_A)
            for c0 in range(0, sub, _CHUNK):
                mixed = _dot(mix_w[hd], vb[c0:c0 + _CHUNK, cols]) + sb_ref[hd]
                yab_ref[r0 + c0:r0 + c0 + _CHUNK, cols] = (u[c0:c0 + _CHUNK, cols] * mixed).astype(_BF)

        xg = gate_c * x_in
        if sample:
            t8 = lax.broadcasted_iota(jnp.int32, (sub, 1), 0) % chunk_len
            back1 = jnp.where(t8 >= 1, pltpu.roll(xg, 1, axis=0), b1_ref[rows, :])
            back2 = jnp.where(t8 >= 2, pltpu.roll(xg, 2, axis=0), b2_ref[rows, :])
            xg_ref[rows, :] = xg
        else:
            ext = jnp.concatenate([prev, xg], axis=0)
            back1 = pltpu.roll(ext, 1, axis=0)[_SUBLANES:]
            back2 = pltpu.roll(ext, 2, axis=0)[_SUBLANES:]
            prev = xg[sub - _SUBLANES:]
        conv = back2 * cw[0:1] + back1 * cw[1:2] + xg * cw[2:3]
        yab_ref[rows, d_a:d_a + d_b] = (gate_b * conv).astype(_BF)
        o_ref[rows, :] = x + _dot(yab_ref[rows, :], wout_ref[...])
    if not sample:
        carry_ref[...] = prev
        cs_ref[0] = prev


def _even_mixer(xc, g, w_in, ln_g, ln_b, sgu_w, sgu_b, conv_w, w_out, *, row0, n_rows, seq_len,
                chunk_len, conv_state=None, casts=()):
    sample = conv_state is not None
    t, d = xc.shape
    d_in = w_in.shape[1]
    d_a = _NH_A * _HD_A
    d_b = conv_w.shape[1]
    tile0 = row0 // _TM
    n_tiles = n_rows // _TM
    row = lambda i: (tile0 + i, 0)
    reps = _CHUNK // chunk_len
    sw = jnp.tile(sgu_w[:, :chunk_len, :chunk_len], (1, reps, reps))
    sb = jnp.broadcast_to(jnp.tile(sgu_b[:, :chunk_len], (1, reps))[:, :, None], (_NH_A, _CHUNK, _HD_A))
    in_specs = [pl.BlockSpec((_TM, d), row), _resident((1, d)), _resident((d, d_in)), _resident((1, d_a)),
                _resident((1, d_a)), _resident(sw.shape), _resident(sb.shape), _resident(conv_w.shape),
                _resident(w_out.shape)]
    args = [xc, g.reshape(1, d), w_in, ln_g.reshape(1, d_a), ln_b.reshape(1, d_a), sw, sb, conv_w, w_out]
    scratch = [pltpu.VMEM((_TM, d_a + d_b), _BF)]
    if sample:
        n_seq = n_rows // chunk_len
        pad = lambda a: jnp.pad(a, ((0, 0), (0, chunk_len - a.shape[1]), (0, 0))).reshape(n_rows, d_b)
        b1 = pad(conv_state[:, 1:2])
        b2 = pad(conv_state)
        local = lambda i: (i, 0)
        in_specs += [pl.BlockSpec((_TM, d_b), local), pl.BlockSpec((_TM, d_b), local)]
        args += [b1, b2]
        out_specs = [pl.BlockSpec((_TM, d), row), pl.BlockSpec((_TM, d_b), local), pl.BlockSpec((_TM, d_a), local)]
        out_shape = [jax.ShapeDtypeStruct((t, d), _F32), jax.ShapeDtypeStruct((n_rows, d_b), _F32),
                     jax.ShapeDtypeStruct((n_rows, d_a), _F32)]
        tiles_per_seq = 1
    else:
        n_seq = n_rows // seq_len
        tiles_per_seq = seq_len // _TM
        out_specs = [pl.BlockSpec((_TM, d), row),
                     pl.BlockSpec((1, _SUBLANES, d_b), lambda i: (i // tiles_per_seq, 0, 0))]
        out_shape = [jax.ShapeDtypeStruct((t, d), _F32), jax.ShapeDtypeStruct((n_seq, _SUBLANES, d_b), _F32)]
        scratch.append(pltpu.VMEM((_SUBLANES, d_b), _F32))
    return _call(
        functools.partial(_even_body, sample=sample, chunk_len=chunk_len, tiles_per_seq=tiles_per_seq, sub=_SUB),
        name="even_sample" if sample else "even_prompt",
        grid=n_tiles, in_specs=in_specs, args=args, out_specs=out_specs, out_shape=out_shape, scratch=scratch,
        casts=casts, aliases={0: 0})


def _window_sums(ext):
    gc = ext.shape[1] // len(_POOL_WINDOWS)
    sums = []
    s = ext
    width = 1
    for w in _POOL_WINDOWS:
        while width < w:
            s = s + pltpu.roll(s, width, axis=0)
            width *= 2
        sums.append(s[:, 0:gc])
        s = s[:, gc:]
    return sums


def _odd_body(*refs, sample, seq_len, tiles_per_seq, sub):
    i = pl.program_id(0)
    if sample:
        (x_ref, g_ref, win_ref, pw_ref, psc_ref, wout_ref, ctx_ref, o_ref, ps_ref, y_ref, ext_ref) = refs
    else:
        (x_ref, g_ref, win_ref, pw_ref, psc_ref, wout_ref, o_ref, ps_ref, y_ref, carry_ref) = refs

        @pl.when((i % tiles_per_seq) == 0)
        def _():
            carry_ref[...] = jnp.zeros_like(carry_ref)

    tm, d = x_ref.shape
    ctx_rows = 2 * _SUBLANES
    gc = d // len(_POOL_WINDOWS)
    prev = None if sample else carry_ref[...]
    for r0 in range(0, tm, sub):
        rows = slice(r0, r0 + sub)
        x = x_ref[rows, :]
        h = _rms(x, g_ref[...]).astype(_BF)
        p = _dot(h, win_ref[...])
        t_in_tile = r0 + lax.broadcasted_iota(jnp.int32, (sub, 1), 0)
        if sample:
            nb = sub // seq_len
            b0 = r0 // seq_len
            lead = ctx_rows - _POOL_CTX
            ext_ref[b0:b0 + nb, 0:lead, :] = jnp.zeros((nb, lead, d), _F32)
            for k in range(_POOL_CTX):
                ext_ref[b0:b0 + nb, lead + k, :] = ctx_ref[k, b0:b0 + nb, :]
            ext_ref[b0:b0 + nb, ctx_rows:ctx_rows + seq_len, :] = p.reshape(nb, seq_len, d)
            ext = ext_ref[b0:b0 + nb].reshape(nb * (ctx_rows + seq_len), d)
            pick = lambda s: s.reshape(nb, ctx_rows + seq_len, s.shape[1])[:, ctx_rows:, :].reshape(sub, s.shape[1])
            pos = _PAST_LEN + t_in_tile % seq_len
            for k in range(_POOL_CTX):
                ps_ref[k, b0:b0 + nb, :] = ext_ref[b0:b0 + nb, ctx_rows + seq_len - _POOL_CTX + k, :]
        else:
            ext = jnp.concatenate([prev, p], axis=0)
            pick = lambda s: s[ctx_rows:]
            pos = (i % tiles_per_seq) * tm + t_in_tile
            prev = p[sub - ctx_rows:]
        for grp, (w, s) in enumerate(zip(_POOL_WINDOWS, _window_sums(ext))):
            cnt = jnp.minimum(pos + 1, w).astype(_F32)
            cols = slice(grp * gc, (grp + 1) * gc)
            dlt = pick(s) / cnt - p[:, cols]
            yg = _dot(dlt.astype(_BF), pw_ref[grp])
            y_ref[rows, cols] = (yg * psc_ref[:, cols]).astype(_BF)
        o_ref[rows, :] = x + _dot(y_ref[rows, :], wout_ref[...])
    if not sample:
        carry_ref[...] = prev
        ps_ref[0] = prev


def _odd_mixer(xc, g, w_in, pool_w, pool_scale, w_out, *, row0, n_rows, seq_len, pool_state=None):
    sample = pool_state is not None
    t, d = xc.shape
    ctx_rows = 2 * _SUBLANES
    tm = _TM // 4 if sample else _TM
    tile0 = row0 // tm
    n_tiles = n_rows // tm
    row = lambda i: (tile0 + i, 0)
    in_specs = [pl.BlockSpec((tm, d), row), _resident((1, d)), _resident((d, d)), _resident(pool_w.shape),
                _resident((1, d)), _resident((d, d))]
    args = [xc, g.reshape(1, d), w_in, pool_w, pool_scale.reshape(1, d), w_out]
    scratch = [pltpu.VMEM((tm, d), _BF)]
    if sample:
        nb = tm // seq_len
        ctx_spec = pl.BlockSpec((_POOL_CTX, nb, d), lambda i: (0, i, 0))
        in_specs.append(ctx_spec)
        args.append(pool_state)
        out_specs = [pl.BlockSpec((tm, d), row), ctx_spec]
        out_shape = [jax.ShapeDtypeStruct((t, d), _F32), jax.ShapeDtypeStruct(pool_state.shape, _F32)]
        scratch.append(pltpu.VMEM((nb, ctx_rows + seq_len, d), _F32))
        tiles_per_seq = 1
    else:
        n_seq = n_rows // seq_len
        tiles_per_seq = seq_len // tm
        out_specs = [pl.BlockSpec((tm, d), row),
                     pl.BlockSpec((1, ctx_rows, d), lambda i: (i // tiles_per_seq, 0, 0))]
        out_shape = [jax.ShapeDtypeStruct((t, d), _F32), jax.ShapeDtypeStruct((n_seq, ctx_rows, d), _F32)]
        scratch.append(pltpu.VMEM((ctx_rows, d), _F32))
    outs, _ = _call(
        functools.partial(_odd_body, sample=sample, seq_len=seq_len, tiles_per_seq=tiles_per_seq, sub=min(tm, _SUB)),
        name="odd_sample" if sample else "odd_prompt",
        grid=n_tiles, in_specs=in_specs, args=args, out_specs=out_specs, out_shape=out_shape, scratch=scratch,
        aliases={0: 0})
    return outs


def kernel(x_prompt, x_sample, state_conv, state_pool, norm_g, final_norm_g, ffn_w_gate, ffn_w_up, ffn_w_down,
           e_w_in, e_ln_g, e_ln_b, e_sgu_w, e_sgu_b, e_conv_w, e_w_out, o_w_in, o_pool_w, o_pool_scale, o_w_out):
    batch, seq, d = x_prompt.shape
    dec_batch, dec_seq, _ = x_sample.shape
    n_prompt, n_sample = batch * seq, dec_batch * dec_seq
    d_a = _NH_A * _HD_A
    d_b = e_conv_w.shape[-1]
    n_groups, gc = o_pool_w.shape[1:3]
    ffn = functools.partial(_ffn, n_prompt=n_prompt, n_sample=n_sample)
    ffn_casts = lambda layer, k: [_Cast(w, (layer, k)) for w in (ffn_w_gate, ffn_w_up, ffn_w_down)]

    even_casts = [_Cast(e_w_in, (0,)), _Cast(e_w_out, (0,))]
    odd_casts = [_Cast(o_w_in, (0,)), _Cast(o_w_out, (0,)), _Cast(o_pool_w.reshape(-1, n_groups * gc, gc), (0,))]
    (xc,), (e_win, e_wout, *w01) = ffn(
        (x_prompt.reshape(n_prompt, d), x_sample.reshape(n_sample, d)), norm_g[0, 0],
        ffn_w_gate, ffn_w_up, ffn_w_down, f32_lead=(0, 0), casts=even_casts + ffn_casts(0, 1))
    even = functools.partial(_even_mixer, g=norm_g[0, 1], w_in=e_win, ln_g=e_ln_g[0], ln_b=e_ln_b[0],
                             sgu_w=e_sgu_w[0], sgu_b=e_sgu_b[0], conv_w=e_conv_w[0], w_out=e_wout)
    (xc, conv_p), _ = even(xc, row0=0, n_rows=n_prompt, seq_len=seq, chunk_len=_CHUNK)
    (xc, xg_s, v_s), _ = even(xc, row0=n_prompt, n_rows=n_sample, seq_len=dec_seq, chunk_len=dec_seq,
                              conv_state=state_conv[0])
    (xc,), (o_win, o_wout, o_pw, *w10) = ffn((xc,), norm_g[0, 2], *w01, casts=odd_casts + ffn_casts(1, 0))

    (xc,), w11 = ffn((xc,), norm_g[1, 0], *w10, casts=ffn_casts(1, 1))
    odd = functools.partial(_odd_mixer, g=norm_g[1, 1], w_in=o_win, pool_w=o_pw.reshape(n_groups, gc, gc),
                            pool_scale=o_pool_scale[0], w_out=o_wout)
    xc, pool_p = odd(xc, row0=0, n_rows=n_prompt, seq_len=seq)
    xc, pool_s = odd(xc, row0=n_prompt, n_rows=n_sample, seq_len=dec_seq,
                     pool_state=jnp.transpose(state_pool[0], (1, 0, 2)))
    (y_p, y_s), _ = ffn((xc,), norm_g[1, 2], *w11, final_g=final_norm_g)

    keep = _CONV_W - 1
    conv_prompt = conv_p[:, _SUBLANES - keep:][None]
    conv_sample = xg_s.reshape(dec_batch, dec_seq, d_b)[:, dec_seq - keep:][None]
    chunk_v_sample = v_s.reshape(1, dec_batch, dec_seq, d_a)
    pool_prompt = pool_p[:, pool_p.shape[1] - _POOL_CTX:][None]
    pool_sample = jnp.transpose(pool_s, (1, 0, 2))[None]
    return (y_p.reshape(batch, seq, d), y_s.reshape(dec_batch, dec_seq, d), conv_prompt, conv_sample,
            chunk_v_sample, pool_prompt, pool_sample)
```

```python
import collections
import functools

import jax
import jax.numpy as jnp
from jax import lax
from jax.experimental import pallas as pl
from jax.experimental.pallas import tpu as pltpu

_CHUNK = 128
_NH_A = 4
_HD_A = 128
_CONV_W = 3
_POOL_WINDOWS = (2, 4, 8, 16)
_POOL_CTX = max(_POOL_WINDOWS) - 1
_PAST_LEN = 16384
_EPS = 1e-6

_SUBLANES = 8
_BF16_ROWS = 16
_VMEM_LIMIT = 56 * 1024 * 1024

_TM = 1024
_SUB = 512
_FFN_CHUNKS = ((0, 1536), (1536, 1280))
_CTX_ROWS = 2 * _SUBLANES

_BF = jnp.bfloat16
_F32 = jnp.float32


def _rms(x, g):
    return x * lax.rsqrt(jnp.mean(x * x, axis=-1, keepdims=True) + _EPS) * g


def _dot(a, b):
    return jnp.dot(a, b, preferred_element_type=_F32)


def _resident(shape):
    nd = len(shape)
    return pl.BlockSpec(shape, lambda i: (0,) * nd, pipeline_mode=pl.Buffered(1))


_Cast = collections.namedtuple("_Cast", "array lead")


def _cast_specs(cast, grid):
    lead = tuple(cast.lead)
    rows, cols = cast.array.shape[len(lead):]
    steps = max(s for s in (1, 2, 4, 8, 16, 32) if s <= grid and rows % (s * _BF16_ROWS) == 0)
    blk = rows // steps
    step = lambda i: jnp.minimum(i, steps - 1)
    in_spec = pl.BlockSpec((None,) * len(lead) + (blk, cols), lambda i: lead + (step(i), 0))
    out_spec = pl.BlockSpec((blk, cols), lambda i: (step(i), 0))
    return in_spec, out_spec, jax.ShapeDtypeStruct((rows, cols), _BF)


def _call(body, *, name, grid, in_specs, args, out_specs, out_shape, scratch=(), casts=(), aliases=None):
    n_in, n_out, n_cast = len(in_specs), len(out_specs), len(casts)
    specs = [_cast_specs(c, grid) for c in casts]

    def wrapped(*refs):
        ins, refs = refs[:n_in], refs[n_in:]
        cast_in, refs = refs[:n_cast], refs[n_cast:]
        outs, refs = refs[:n_out], refs[n_out:]
        cast_out, scratch_refs = refs[:n_cast], refs[n_cast:]
        for src, dst in zip(cast_in, cast_out):
            dst[...] = src[...].astype(_BF)
        body(*ins, *outs, *scratch_refs)

    res = pl.pallas_call(
        wrapped,
        grid=(grid,),
        in_specs=list(in_specs) + [s[0] for s in specs],
        out_specs=list(out_specs) + [s[1] for s in specs],
        out_shape=list(out_shape) + [s[2] for s in specs],
        scratch_shapes=list(scratch),
        input_output_aliases=aliases or {},
        compiler_params=pltpu.CompilerParams(dimension_semantics=("arbitrary",), vmem_limit_bytes=_VMEM_LIMIT),
        name=name,
    )(*args, *[c.array for c in casts])
    return res[:n_out], res[n_out:]


def _ffn_body(*refs, split_in, final, n_prompt_tiles):
    refs = list(refs)
    i = pl.program_id(0)
    x_refs = [refs.pop(0) for _ in range(2 if split_in else 1)]
    g_ref, wg_ref, wu_ref, wd_ref = refs[:4]
    refs = refs[4:]
    is_prompt = i < n_prompt_tiles
    for r0 in range(0, _TM, _SUB):
        rows = slice(r0, r0 + _SUB)
        if split_in:
            x = jnp.where(is_prompt, x_refs[0][rows, :], x_refs[1][rows, :])
        else:
            x = x_refs[0][rows, :]
        h = _rms(x, g_ref[...]).astype(_BF)
        acc = None
        for c0, cn in _FFN_CHUNKS:
            a = _dot(h, wg_ref[:, c0:c0 + cn])
            b = _dot(h, wu_ref[:, c0:c0 + cn])
            act = (a * jax.nn.sigmoid(a) * b).astype(_BF)
            d = _dot(act, wd_ref[c0:c0 + cn, :])
            acc = d if acc is None else acc + d
        y = x + 0.5 * acc
        if final:
            fg_ref, yp_ref, ys_ref = refs
            y = _rms(y, fg_ref[...])
            to_prompt = jnp.broadcast_to(is_prompt, y.shape)
            pltpu.store(yp_ref.at[rows, :], y, mask=to_prompt)
            pltpu.store(ys_ref.at[rows, :], y, mask=jnp.logical_not(to_prompt))
        else:
            (o_ref,) = refs
            o_ref[rows, :] = y


def _ffn(xs, g, wg, wu, wd, *, n_prompt, n_sample, final_g=None, casts=()):
    d, dff = wg.shape
    npt, nst = n_prompt // _TM, n_sample // _TM
    split_in = len(xs) == 2
    final = final_g is not None
    row = lambda i: (i, 0)
    prow = lambda i: (jnp.minimum(i, npt - 1), 0)
    srow = lambda i: (jnp.maximum(i - npt, 0), 0)
    tile = (_TM, d)
    if split_in:
        in_specs = [pl.BlockSpec(tile, prow), pl.BlockSpec(tile, srow, pipeline_mode=pl.Buffered(1))]
    else:
        in_specs = [pl.BlockSpec(tile, row)]
    in_specs += [_resident((1, d)), _resident((d, dff)), _resident((d, dff)), _resident((dff, d))]
    args = list(xs) + [g.reshape(1, d), wg, wu, wd]
    if final:
        in_specs.append(_resident((1, d)))
        args.append(final_g.reshape(1, d))
        out_specs = [pl.BlockSpec(tile, prow), pl.BlockSpec(tile, srow)]
        out_shape = [jax.ShapeDtypeStruct((n_prompt, d), _F32), jax.ShapeDtypeStruct((n_sample, d), _F32)]
    else:
        out_specs = [pl.BlockSpec(tile, row)]
        out_shape = [jax.ShapeDtypeStruct((n_prompt + n_sample, d), _F32)]
    return _call(
        functools.partial(_ffn_body, split_in=split_in, final=final, n_prompt_tiles=npt),
        name="ffn_final" if final else ("ffn_first" if split_in else "ffn"),
        grid=npt + nst, in_specs=in_specs, args=args, out_specs=out_specs, out_shape=out_shape, casts=casts)


def _gelu(x):
    return 0.5 * x * (1.0 + lax.erf(x * (2.0 ** -0.5)))


def _even_tile(x_ref, g_ref, win_ref, lng_ref, lnb_ref, sw_ref, sb_ref, cw_ref, wout_ref, o_ref, yab_ref, *,
               chunk_len, sub, carry_ref=None, cs_ref=None, b1_ref=None, b2_ref=None, xg_ref=None, v_ref=None):
    sample = carry_ref is None
    tm = x_ref.shape[0]
    d_a = _NH_A * _HD_A
    d_b = cw_ref.shape[1]
    o = 2 * d_a
    tt = lax.broadcasted_iota(jnp.int32, (_CHUNK, _CHUNK), 0)
    ss = lax.broadcasted_iota(jnp.int32, (_CHUNK, _CHUNK), 1)
    keep = (ss <= tt) & ((tt // chunk_len) == (ss // chunk_len))
    mix_w = [jnp.where(keep, sw_ref[hd], 0.0).astype(_BF) for hd in range(_NH_A)]
    cw = cw_ref[...]
    prev = None if sample else carry_ref[...]
    for r0 in range(0, tm, sub):
        rows = slice(r0, r0 + sub)
        x = x_ref[rows, :]
        h = _rms(x, g_ref[...]).astype(_BF)
        z = _dot(h, win_ref[...])
        u = _gelu(z[:, 0:d_a])
        vv = _gelu(z[:, d_a:2 * d_a])
        mu = jnp.mean(vv, axis=-1, keepdims=True)
        vc = vv - mu
        var = jnp.mean(vc * vc, axis=-1, keepdims=True)
        v = vc * lax.rsqrt(var + _EPS) * lng_ref[...] + lnb_ref[...]
        if sample:
            v_ref[rows, :] = v
        gate_b = z[:, o:o + d_b]
        gate_c = z[:, o + d_b:o + 2 * d_b]
        x_in = z[:, o + 2 * d_b:o + 3 * d_b]

        vb = v.astype(_BF)
        for hd in range(_NH_A):
            cols = slice(hd * _HD_A, (hd + 1) * _HD_A)
            for c0 in range(0, sub, _CHUNK):
                mixed = _dot(mix_w[hd], vb[c0:c0 + _CHUNK, cols]) + sb_ref[hd]
                yab_ref[r0 + c0:r0 + c0 + _CHUNK, cols] = (u[c0:c0 + _CHUNK, cols] * mixed).astype(_BF)

        xg = gate_c * x_in
        if sample:
            t8 = lax.broadcasted_iota(jnp.int32, (sub, 1), 0) % chunk_len
            back1 = jnp.where(t8 >= 1, pltpu.roll(xg, 1, axis=0), b1_ref[rows, :])
            back2 = jnp.where(t8 >= 2, pltpu.roll(xg, 2, axis=0), b2_ref[rows, :])
            xg_ref[rows, :] = xg
        else:
            ext = jnp.concatenate([prev, xg], axis=0)
            back1 = pltpu.roll(ext, 1, axis=0)[_SUBLANES:]
            back2 = pltpu.roll(ext, 2, axis=0)[_SUBLANES:]
            prev = xg[sub - _SUBLANES:]
        conv = back2 * cw[0:1] + back1 * cw[1:2] + xg * cw[2:3]
        yab_ref[rows, d_a:d_a + d_b] = (gate_b * conv).astype(_BF)
        o_ref[rows, :] = x + _dot(yab_ref[rows, :], wout_ref[...])
    if not sample:
        carry_ref[...] = prev
        cs_ref[0] = prev


def _even_body(x_ref, g_ref, win_ref, lng_ref, lnb_ref, swp_ref, sbp_ref, sws_ref, sbs_ref, cw_ref, wout_ref,
               b1_ref, b2_ref, o_ref, cs_ref, xg_ref, v_ref, yab_ref, carry_ref, *,
               n_prompt_tiles, tiles_per_seq, chunk_s, sub):
    i = pl.program_id(0)
    shared = (x_ref, g_ref, win_ref, lng_ref, lnb_ref)

    @pl.when(i < n_prompt_tiles)
    def _():
        @pl.when((i % tiles_per_seq) == 0)
        def _():
            carry_ref[...] = jnp.zeros_like(carry_ref)

        _even_tile(*shared, swp_ref, sbp_ref, cw_ref, wout_ref, o_ref, yab_ref, chunk_len=_CHUNK, sub=sub,
                   carry_ref=carry_ref, cs_ref=cs_ref)

    @pl.when(i >= n_prompt_tiles)
    def _():
        _even_tile(*shared, sws_ref, sbs_ref, cw_ref, wout_ref, o_ref, yab_ref, chunk_len=chunk_s, sub=sub,
                   b1_ref=b1_ref, b2_ref=b2_ref, xg_ref=xg_ref, v_ref=v_ref)


def _sgu_tiles(sgu_w, sgu_b, chunk_len):
    reps = _CHUNK // chunk_len
    sw = jnp.tile(sgu_w[:, :chunk_len, :chunk_len], (1, reps, reps))
    sb = jnp.broadcast_to(jnp.tile(sgu_b[:, :chunk_len], (1, reps))[:, :, None], (_NH_A, _CHUNK, _HD_A))
    return sw, sb


def _even_mixer(xc, g, w_in, ln_g, ln_b, sgu_w, sgu_b, conv_w, w_out, conv_state, *, n_prompt, seq_len, chunk_s):
    t, d = xc.shape
    n_sample = t - n_prompt
    d_in = w_in.shape[1]
    d_a = _NH_A * _HD_A
    d_b = conv_w.shape[1]
    npt = n_prompt // _TM
    n_seq = n_prompt // seq_len
    tiles_per_seq = seq_len // _TM
    row = lambda i: (i, 0)
    srow = lambda i: (jnp.maximum(i - npt, 0), 0)
    swp, sbp = _sgu_tiles(sgu_w, sgu_b, _CHUNK)
    sws, sbs = _sgu_tiles(sgu_w, sgu_b, chunk_s)
    pad = lambda a: jnp.pad(a, ((0, 0), (0, chunk_s - a.shape[1]), (0, 0))).reshape(n_sample, d_b)
    b1 = pad(conv_state[:, 1:2])
    b2 = pad(conv_state)
    once = lambda shape, imap: pl.BlockSpec(shape, imap, pipeline_mode=pl.Buffered(1))
    in_specs = [pl.BlockSpec((_TM, d), row), _resident((1, d)), _resident((d, d_in)), _resident((1, d_a)),
                _resident((1, d_a)), _resident(swp.shape), _resident(sbp.shape), _resident(sws.shape),
                _resident(sbs.shape), _resident(conv_w.shape), _resident(w_out.shape),
                once((_TM, d_b), srow), once((_TM, d_b), srow)]
    args = [xc, g.reshape(1, d), w_in, ln_g.reshape(1, d_a), ln_b.reshape(1, d_a), swp, sbp, sws, sbs, conv_w, w_out,
            b1, b2]
    out_specs = [pl.BlockSpec((_TM, d), row),
                 pl.BlockSpec((1, _SUBLANES, d_b), lambda i: (jnp.minimum(i // tiles_per_seq, n_seq - 1), 0, 0)),
                 pl.BlockSpec((_TM, d_b), srow), pl.BlockSpec((_TM, d_a), srow)]
    out_shape = [jax.ShapeDtypeStruct((t, d), _F32), jax.ShapeDtypeStruct((n_seq, _SUBLANES, d_b), _F32),
                 jax.ShapeDtypeStruct((n_sample, d_b), _F32), jax.ShapeDtypeStruct((n_sample, d_a), _F32)]
    scratch = [pltpu.VMEM((_TM, d_a + d_b), _BF), pltpu.VMEM((_SUBLANES, d_b), _F32)]
    outs, _ = _call(
        functools.partial(_even_body, n_prompt_tiles=npt, tiles_per_seq=tiles_per_seq, chunk_s=chunk_s, sub=_SUB),
        name="even", grid=t // _TM, in_specs=in_specs, args=args, out_specs=out_specs, out_shape=out_shape,
        scratch=scratch, aliases={0: 0})
    return outs


def _window_sums(ext):
    gc = ext.shape[1] // len(_POOL_WINDOWS)
    sums = []
    s = ext
    width = 1
    for w in _POOL_WINDOWS:
        while width < w:
            s = s + pltpu.roll(s, width, axis=0)
            width *= 2
        sums.append(s[:, 0:gc])
        s = s[:, gc:]
    return sums


def _odd_tile(x_ref, g_ref, win_ref, pw_ref, psc_ref, wout_ref, o_ref, ps_ref, y_ref, *, sub, pos0=None,
              carry_ref=None, ctx_ref=None, ext_ref=None, seq_len=None):
    sample = carry_ref is None
    tm, d = x_ref.shape
    gc = d // len(_POOL_WINDOWS)
    prev = None if sample else carry_ref[...]
    for r0 in range(0, tm, sub):
        rows = slice(r0, r0 + sub)
        x = x_ref[rows, :]
        h = _rms(x, g_ref[...]).astype(_BF)
        p = _dot(h, win_ref[...])
        t_in_tile = r0 + lax.broadcasted_iota(jnp.int32, (sub, 1), 0)
        if sample:
            nb = sub // seq_len
            b0 = r0 // seq_len
            per_seq = _CTX_ROWS + seq_len
            lead = _CTX_ROWS - _POOL_CTX
            ext_ref[:, 0:lead, :] = jnp.zeros((nb, lead, d), _F32)
            for k in range(_POOL_CTX):
                ext_ref[:, lead + k, :] = ctx_ref[k, b0:b0 + nb, :]
            ext_ref[:, _CTX_ROWS:per_seq, :] = p.reshape(nb, seq_len, d)
            ext = ext_ref[...].reshape(nb * per_seq, d)
            pick = lambda s: s.reshape(nb, per_seq, s.shape[1])[:, _CTX_ROWS:, :].reshape(sub, s.shape[1])
            pos = _PAST_LEN + t_in_tile % seq_len
            for k in range(_POOL_CTX):
                ps_ref[k, b0:b0 + nb, :] = ext_ref[:, per_seq - _POOL_CTX + k, :]
        else:
            ext = jnp.concatenate([prev, p], axis=0)
            pick = lambda s: s[_CTX_ROWS:]
            pos = pos0 + t_in_tile
            prev = p[sub - _CTX_ROWS:]
        for grp, (w, s) in enumerate(zip(_POOL_WINDOWS, _window_sums(ext))):
            cnt = jnp.minimum(pos + 1, w).astype(_F32)
            cols = slice(grp * gc, (grp + 1) * gc)
            dlt = pick(s) / cnt - p[:, cols]
            yg = _dot(dlt.astype(_BF), pw_ref[grp])
            y_ref[rows, cols] = (yg * psc_ref[:, cols]).astype(_BF)
        o_ref[rows, :] = x + _dot(y_ref[rows, :], wout_ref[...])
    if not sample:
        carry_ref[...] = prev
        ps_ref[0] = prev


def _odd_body(x_ref, g_ref, win_ref, pw_ref, psc_ref, wout_ref, ctx_ref, o_ref, psp_ref, pss_ref,
              y_ref, carry_ref, ext_ref, *, n_prompt_tiles, tiles_per_seq, seq_s):
    i = pl.program_id(0)
    shared = (x_ref, g_ref, win_ref, pw_ref, psc_ref, wout_ref, o_ref)

    @pl.when(i < n_prompt_tiles)
    def _():
        @pl.when((i % tiles_per_seq) == 0)
        def _():
            carry_ref[...] = jnp.zeros_like(carry_ref)

        _odd_tile(*shared, psp_ref, y_ref, sub=_SUB, pos0=(i % tiles_per_seq) * x_ref.shape[0], carry_ref=carry_ref)

    @pl.when(i >= n_prompt_tiles)
    def _():
        _odd_tile(*shared, pss_ref, y_ref, sub=ext_ref.shape[0] * seq_s, ctx_ref=ctx_ref, ext_ref=ext_ref,
                  seq_len=seq_s)


def _odd_mixer(xc, g, w_in, pool_w, pool_scale, w_out, pool_state, *, n_prompt, seq_len, seq_s):
    t, d = xc.shape
    npt = n_prompt // _TM
    n_seq = n_prompt // seq_len
    tiles_per_seq = seq_len // _TM
    seqs_per_tile = _TM // seq_s
    row = lambda i: (i, 0)
    ctx_spec = lambda **kw: pl.BlockSpec((_POOL_CTX, seqs_per_tile, d), lambda i: (0, jnp.maximum(i - npt, 0), 0), **kw)
    in_specs = [pl.BlockSpec((_TM, d), row), _resident((1, d)), _resident((d, d)), _resident(pool_w.shape),
                _resident((1, d)), _resident((d, d)), ctx_spec(pipeline_mode=pl.Buffered(1))]
    args = [xc, g.reshape(1, d), w_in, pool_w, pool_scale.reshape(1, d), w_out, pool_state]
    out_specs = [pl.BlockSpec((_TM, d), row),
                 pl.BlockSpec((1, _CTX_ROWS, d), lambda i: (jnp.minimum(i // tiles_per_seq, n_seq - 1), 0, 0)),
                 ctx_spec()]
    out_shape = [jax.ShapeDtypeStruct((t, d), _F32), jax.ShapeDtypeStruct((n_seq, _CTX_ROWS, d), _F32),
                 jax.ShapeDtypeStruct(pool_state.shape, _F32)]
    seqs_per_sub = _SUB // 2 // seq_s
    scratch = [pltpu.VMEM((_TM, d), _BF), pltpu.VMEM((_CTX_ROWS, d), _F32),
               pltpu.VMEM((seqs_per_sub, _CTX_ROWS + seq_s, d), _F32)]
    outs, _ = _call(
        functools.partial(_odd_body, n_prompt_tiles=npt, tiles_per_seq=tiles_per_seq, seq_s=seq_s),
        name="odd", grid=t // _TM, in_specs=in_specs, args=args, out_specs=out_specs, out_shape=out_shape,
        scratch=scratch, aliases={0: 0})
    return outs


def kernel(x_prompt, x_sample, state_conv, state_pool, norm_g, final_norm_g, ffn_w_gate, ffn_w_up, ffn_w_down,
           e_w_in, e_ln_g, e_ln_b, e_sgu_w, e_sgu_b, e_conv_w, e_w_out, o_w_in, o_pool_w, o_pool_scale, o_w_out):
    batch, seq, d = x_prompt.shape
    dec_batch, dec_seq, _ = x_sample.shape
    n_prompt, n_sample = batch * seq, dec_batch * dec_seq
    d_a = _NH_A * _HD_A
    d_b = e_conv_w.shape[-1]
    n_groups, gc = o_pool_w.shape[1:3]
    ffn = functools.partial(_ffn, n_prompt=n_prompt, n_sample=n_sample)
    ffn_casts = lambda layer, k: [_Cast(w, (layer, k)) for w in (ffn_w_gate, ffn_w_up, ffn_w_down)]

    w00 = [w[0, 0].astype(_BF) for w in (ffn_w_gate, ffn_w_up, ffn_w_down)]
    even_casts = [_Cast(e_w_in, (0,)), _Cast(e_w_out, (0,))]
    odd_casts = [_Cast(o_w_in, (0,)), _Cast(o_w_out, (0,)), _Cast(o_pool_w.reshape(-1, n_groups * gc, gc), (0,))]
    (xc,), (e_win, e_wout, *w01) = ffn(
        (x_prompt.reshape(n_prompt, d), x_sample.reshape(n_sample, d)), norm_g[0, 0], *w00,
        casts=even_casts + ffn_casts(0, 1))
    xc, conv_p, xg_s, v_s = _even_mixer(xc, norm_g[0, 1], e_win, e_ln_g[0], e_ln_b[0], e_sgu_w[0], e_sgu_b[0],
                                         e_conv_w[0], e_wout, state_conv[0], n_prompt=n_prompt, seq_len=seq,
                                         chunk_s=dec_seq)
    (xc,), (o_win, o_wout, o_pw, *w10) = ffn((xc,), norm_g[0, 2], *w01, casts=odd_casts + ffn_casts(1, 0))

    (xc,), w11 = ffn((xc,), norm_g[1, 0], *w10, casts=ffn_casts(1, 1))
    xc, pool_p, pool_s = _odd_mixer(xc, norm_g[1, 1], o_win, o_pw.reshape(n_groups, gc, gc), o_pool_scale[0], o_wout,
                                    jnp.transpose(state_pool[0], (1, 0, 2)), n_prompt=n_prompt, seq_len=seq,
                                    seq_s=dec_seq)
    (y_p, y_s), _ = ffn((xc,), norm_g[1, 2], *w11, final_g=final_norm_g)

    keep = _CONV_W - 1
    conv_prompt = conv_p[:, _SUBLANES - keep:][None]
    conv_sample = xg_s.reshape(dec_batch, dec_seq, d_b)[:, dec_seq - keep:][None]
    chunk_v_sample = v_s.reshape(1, dec_batch, dec_seq, d_a)
    pool_prompt = pool_p[:, pool_p.shape[1] - _POOL_CTX:][None]
    pool_sample = jnp.transpose(pool_s, (1, 0, 2))[None]
    return (y_p.reshape(batch, seq, d), y_s.reshape(dec_batch, dec_seq, d), conv_prompt, conv_sample,
            chunk_v_sample, pool_prompt, pool_sample)
```

```python
import collections
import functools

import jax
import jax.numpy as jnp
from jax import lax
from jax.experimental import pallas as pl
from jax.experimental.pallas import tpu as pltpu

_CHUNK = 128
_NH_A = 4
_HD_A = 128
_CONV_W = 3
_POOL_WINDOWS = (2, 4, 8, 16)
_POOL_CTX = max(_POOL_WINDOWS) - 1
_PAST_LEN = 16384
_EPS = 1e-6

_SUBLANES = 8
_BF16_ROWS = 16
_VMEM_LIMIT = 56 * 1024 * 1024

_TM = 1024
_SUB = 512
_FFN_CHUNKS = ((0, 1536), (1536, 1280))
_CTX_ROWS = 2 * _SUBLANES

_BF = jnp.bfloat16
_F32 = jnp.float32


def _rms(x, g):
    return x * lax.rsqrt(jnp.mean(x * x, axis=-1, keepdims=True) + _EPS) * g


def _dot(a, b):
    return jnp.dot(a, b, preferred_element_type=_F32)


def _resident(shape):
    nd = len(shape)
    return pl.BlockSpec(shape, lambda i: (0,) * nd, pipeline_mode=pl.Buffered(1))


_Cast = collections.namedtuple("_Cast", "array lead")


def _cast_specs(cast, grid):
    lead = tuple(cast.lead)
    rows, cols = cast.array.shape[len(lead):]
    steps = max(s for s in (1, 2, 4, 8, 16, 32) if s <= grid and rows % (s * _BF16_ROWS) == 0)
    blk = rows // steps
    step = lambda i: jnp.minimum(i, steps - 1)
    in_spec = pl.BlockSpec((None,) * len(lead) + (blk, cols), lambda i: lead + (step(i), 0))
    out_spec = pl.BlockSpec((blk, cols), lambda i: (step(i), 0))
    return in_spec, out_spec, jax.ShapeDtypeStruct((rows, cols), _BF)


def _call(body, *, name, grid, in_specs, args, out_specs, out_shape, scratch=(), casts=(), aliases=None):
    n_in, n_out, n_cast = len(in_specs), len(out_specs), len(casts)
    specs = [_cast_specs(c, grid) for c in casts]

    def wrapped(*refs):
        ins, refs = refs[:n_in], refs[n_in:]
        cast_in, refs = refs[:n_cast], refs[n_cast:]
        outs, refs = refs[:n_out], refs[n_out:]
        cast_out, scratch_refs = refs[:n_cast], refs[n_cast:]
        for src, dst in zip(cast_in, cast_out):
            dst[...] = src[...].astype(_BF)
        body(*ins, *outs, *scratch_refs)

    res = pl.pallas_call(
        wrapped,
        grid=(grid,),
        in_specs=list(in_specs) + [s[0] for s in specs],
        out_specs=list(out_specs) + [s[1] for s in specs],
        out_shape=list(out_shape) + [s[2] for s in specs],
        scratch_shapes=list(scratch),
        input_output_aliases=aliases or {},
        compiler_params=pltpu.CompilerParams(dimension_semantics=("arbitrary",), vmem_limit_bytes=_VMEM_LIMIT),
        name=name,
    )(*args, *[c.array for c in casts])
    return res[:n_out], res[n_out:]


def _ffn_body(*refs, split_in, final, n_prompt_tiles):
    refs = list(refs)
    i = pl.program_id(0)
    x_refs = [refs.pop(0) for _ in range(2 if split_in else 1)]
    g_ref, wg_ref, wu_ref, wd_ref = refs[:4]
    refs = refs[4:]
    is_prompt = i < n_prompt_tiles
    for r0 in range(0, _TM, _SUB):
        rows = slice(r0, r0 + _SUB)
        if split_in:
            x = jnp.where(is_prompt, x_refs[0][rows, :], x_refs[1][rows, :])
        else:
            x = x_refs[0][rows, :]
        h = _rms(x, g_ref[...]).astype(_BF)
        acc = None
        for c0, cn in _FFN_CHUNKS:
            a = _dot(h, wg_ref[:, c0:c0 + cn])
            b = _dot(h, wu_ref[:, c0:c0 + cn])
            act = (a * jax.nn.sigmoid(a) * b).astype(_BF)
            d = _dot(act, wd_ref[c0:c0 + cn, :])
            acc = d if acc is None else acc + d
        y = x + 0.5 * acc
        if final:
            fg_ref, yp_ref, ys_ref = refs
            y = _rms(y, fg_ref[...])
            to_prompt = jnp.broadcast_to(is_prompt, y.shape)
            pltpu.store(yp_ref.at[rows, :], y, mask=to_prompt)
            pltpu.store(ys_ref.at[rows, :], y, mask=jnp.logical_not(to_prompt))
        else:
            (o_ref,) = refs
            o_ref[rows, :] = y


def _ffn(xs, g, wg, wu, wd, *, n_prompt, n_sample, final_g=None, casts=()):
    d, dff = wg.shape
    npt, nst = n_prompt // _TM, n_sample // _TM
    split_in = len(xs) == 2
    final = final_g is not None
    row = lambda i: (i, 0)
    prow = lambda i: (jnp.minimum(i, npt - 1), 0)
    srow = lambda i: (jnp.maximum(i - npt, 0), 0)
    tile = (_TM, d)
    if split_in:
        in_specs = [pl.BlockSpec(tile, prow), pl.BlockSpec(tile, srow, pipeline_mode=pl.Buffered(1))]
    else:
        in_specs = [pl.BlockSpec(tile, row)]
    in_specs += [_resident((1, d)), _resident((d, dff)), _resident((d, dff)), _resident((dff, d))]
    args = list(xs) + [g.reshape(1, d), wg, wu, wd]
    if final:
        in_specs.append(_resident((1, d)))
        args.append(final_g.reshape(1, d))
        out_specs = [pl.BlockSpec(tile, prow), pl.BlockSpec(tile, srow)]
        out_shape = [jax.ShapeDtypeStruct((n_prompt, d), _F32), jax.ShapeDtypeStruct((n_sample, d), _F32)]
    else:
        out_specs = [pl.BlockSpec(tile, row)]
        out_shape = [jax.ShapeDtypeStruct((n_prompt + n_sample, d), _F32)]
    return _call(
        functools.partial(_ffn_body, split_in=split_in, final=final, n_prompt_tiles=npt),
        name="ffn_final" if final else ("ffn_first" if split_in else "ffn"),
        grid=npt + nst, in_specs=in_specs, args=args, out_specs=out_specs, out_shape=out_shape, casts=casts)


def _gelu(x):
    return 0.5 * x * (1.0 + lax.erf(x * (2.0 ** -0.5)))


def _even_tile(x_ref, g_ref, win_ref, lng_ref, lnb_ref, sw_ref, sb_ref, cw_ref, wout_ref, o_ref, *,
               chunk_len, sub, carry_ref=None, cs_ref=None, b1_ref=None, b2_ref=None, xg_ref=None, v_ref=None):
    sample = carry_ref is None
    tm = x_ref.shape[0]
    d_a = _NH_A * _HD_A
    d_b = cw_ref.shape[1]
    o = 2 * d_a
    tt = lax.broadcasted_iota(jnp.int32, (_CHUNK, _CHUNK), 0)
    ss = lax.broadcasted_iota(jnp.int32, (_CHUNK, _CHUNK), 1)
    keep = (ss <= tt) & ((tt // chunk_len) == (ss // chunk_len))
    mix_w = [jnp.where(keep, sw_ref[hd], 0.0).astype(_BF) for hd in range(_NH_A)]
    cw = cw_ref[...]
    prev = None if sample else carry_ref[...]
    for r0 in range(0, tm, sub):
        rows = slice(r0, r0 + sub)
        x = x_ref[rows, :]
        h = _rms(x, g_ref[...]).astype(_BF)
        z = _dot(h, win_ref[...])
        u = _gelu(z[:, 0:d_a])
        vv = _gelu(z[:, d_a:2 * d_a])
        mu = jnp.mean(vv, axis=-1, keepdims=True)
        vc = vv - mu
        var = jnp.mean(vc * vc, axis=-1, keepdims=True)
        v = vc * lax.rsqrt(var + _EPS) * lng_ref[...] + lnb_ref[...]
        if sample:
            v_ref[rows, :] = v
        gate_b = z[:, o:o + d_b]
        gate_c = z[:, o + d_b:o + 2 * d_b]
        x_in = z[:, o + 2 * d_b:o + 3 * d_b]

        vb = v.astype(_BF)
        y_parts = []
        for hd in range(_NH_A):
            cols = slice(hd * _HD_A, (hd + 1) * _HD_A)
            chunks = []
            for c0 in range(0, sub, _CHUNK):
                mixed = _dot(mix_w[hd], vb[c0:c0 + _CHUNK, cols]) + sb_ref[hd]
                chunks.append((u[c0:c0 + _CHUNK, cols] * mixed).astype(_BF))
            y_parts.append(jnp.concatenate(chunks, axis=0))

        xg = gate_c * x_in
        if sample:
            t8 = lax.broadcasted_iota(jnp.int32, (sub, 1), 0) % chunk_len
            back1 = jnp.where(t8 >= 1, pltpu.roll(xg, 1, axis=0), b1_ref[rows, :])
            back2 = jnp.where(t8 >= 2, pltpu.roll(xg, 2, axis=0), b2_ref[rows, :])
            xg_ref[rows, :] = xg
        else:
            ext = jnp.concatenate([prev, xg], axis=0)
            back1 = pltpu.roll(ext, 1, axis=0)[_SUBLANES:]
            back2 = pltpu.roll(ext, 2, axis=0)[_SUBLANES:]
            prev = xg[sub - _SUBLANES:]
        conv = back2 * cw[0:1] + back1 * cw[1:2] + xg * cw[2:3]
        y_parts.append((gate_b * conv).astype(_BF))
        o_ref[rows, :] = x + _dot(jnp.concatenate(y_parts, axis=1), wout_ref[...])
    if not sample:
        carry_ref[...] = prev
        cs_ref[0] = prev


def _even_body(x_ref, g_ref, win_ref, lng_ref, lnb_ref, swp_ref, sbp_ref, sws_ref, sbs_ref, cw_ref, wout_ref,
               b1_ref, b2_ref, o_ref, cs_ref, xg_ref, v_ref, carry_ref, *,
               n_prompt_tiles, tiles_per_seq, chunk_s, sub):
    i = pl.program_id(0)
    shared = (x_ref, g_ref, win_ref, lng_ref, lnb_ref)

    @pl.when(i < n_prompt_tiles)
    def _():
        @pl.when((i % tiles_per_seq) == 0)
        def _():
            carry_ref[...] = jnp.zeros_like(carry_ref)

        _even_tile(*shared, swp_ref, sbp_ref, cw_ref, wout_ref, o_ref, chunk_len=_CHUNK, sub=sub,
                   carry_ref=carry_ref, cs_ref=cs_ref)

    @pl.when(i >= n_prompt_tiles)
    def _():
        _even_tile(*shared, sws_ref, sbs_ref, cw_ref, wout_ref, o_ref, chunk_len=chunk_s, sub=sub,
                   b1_ref=b1_ref, b2_ref=b2_ref, xg_ref=xg_ref, v_ref=v_ref)


def _sgu_tiles(sgu_w, sgu_b, chunk_len):
    reps = _CHUNK // chunk_len
    sw = jnp.tile(sgu_w[:, :chunk_len, :chunk_len], (1, reps, reps))
    sb = jnp.broadcast_to(jnp.tile(sgu_b[:, :chunk_len], (1, reps))[:, :, None], (_NH_A, _CHUNK, _HD_A))
    return sw, sb


def _even_mixer(xc, g, w_in, ln_g, ln_b, sgu_w, sgu_b, conv_w, w_out, conv_state, *, n_prompt, seq_len, chunk_s):
    t, d = xc.shape
    n_sample = t - n_prompt
    d_in = w_in.shape[1]
    d_a = _NH_A * _HD_A
    d_b = conv_w.shape[1]
    npt = n_prompt // _TM
    n_seq = n_prompt // seq_len
    tiles_per_seq = seq_len // _TM
    row = lambda i: (i, 0)
    srow = lambda i: (jnp.maximum(i - npt, 0), 0)
    swp, sbp = _sgu_tiles(sgu_w, sgu_b, _CHUNK)
    sws, sbs = _sgu_tiles(sgu_w, sgu_b, chunk_s)
    pad = lambda a: jnp.pad(a, ((0, 0), (0, chunk_s - a.shape[1]), (0, 0))).reshape(n_sample, d_b)
    b1 = pad(conv_state[:, 1:2])
    b2 = pad(conv_state)
    once = lambda shape, imap: pl.BlockSpec(shape, imap, pipeline_mode=pl.Buffered(1))
    in_specs = [pl.BlockSpec((_TM, d), row), _resident((1, d)), _resident((d, d_in)), _resident((1, d_a)),
                _resident((1, d_a)), _resident(swp.shape), _resident(sbp.shape), _resident(sws.shape),
                _resident(sbs.shape), _resident(conv_w.shape), _resident(w_out.shape),
                once((_TM, d_b), srow), once((_TM, d_b), srow)]
    args = [xc, g.reshape(1, d), w_in, ln_g.reshape(1, d_a), ln_b.reshape(1, d_a), swp, sbp, sws, sbs, conv_w, w_out,
            b1, b2]
    out_specs = [pl.BlockSpec((_TM, d), row),
                 pl.BlockSpec((1, _SUBLANES, d_b), lambda i: (jnp.minimum(i // tiles_per_seq, n_seq - 1), 0, 0)),
                 pl.BlockSpec((_TM, d_b), srow), pl.BlockSpec((_TM, d_a), srow)]
    out_shape = [jax.ShapeDtypeStruct((t, d), _F32), jax.ShapeDtypeStruct((n_seq, _SUBLANES, d_b), _F32),
                 jax.ShapeDtypeStruct((n_sample, d_b), _F32), jax.ShapeDtypeStruct((n_sample, d_a), _F32)]
    scratch = [pltpu.VMEM((_SUBLANES, d_b), _F32)]
    outs, _ = _call(
        functools.partial(_even_body, n_prompt_tiles=npt, tiles_per_seq=tiles_per_seq, chunk_s=chunk_s, sub=_SUB),
        name="even", grid=t // _TM, in_specs=in_specs, args=args, out_specs=out_specs, out_shape=out_shape,
        scratch=scratch, aliases={0: 0})
    return outs


def _window_sums(ext):
    gc = ext.shape[1] // len(_POOL_WINDOWS)
    sums = []
    s = ext
    width = 1
    for w in _POOL_WINDOWS:
        while width < w:
            s = s + pltpu.roll(s, width, axis=0)
            width *= 2
        sums.append(s[:, 0:gc])
        s = s[:, gc:]
    return sums


def _odd_tile(x_ref, g_ref, win_ref, pw_ref, psc_ref, wout_ref, o_ref, ps_ref, *, sub, pos0=None,
              carry_ref=None, ctx_ref=None, ext_ref=None, seq_len=None):
    sample = carry_ref is None
    tm, d = x_ref.shape
    gc = d // len(_POOL_WINDOWS)
    prev = None if sample else carry_ref[...]
    for r0 in range(0, tm, sub):
        rows = slice(r0, r0 + sub)
        x = x_ref[rows, :]
        h = _rms(x, g_ref[...]).astype(_BF)
        p = _dot(h, win_ref[...])
        t_in_tile = r0 + lax.broadcasted_iota(jnp.int32, (sub, 1), 0)
        if sample:
            nb = sub // seq_len
            b0 = r0 // seq_len
            per_seq = _CTX_ROWS + seq_len
            lead = _CTX_ROWS - _POOL_CTX
            ext_ref[:, 0:lead, :] = jnp.zeros((nb, lead, d), _F32)
            for k in range(_POOL_CTX):
                ext_ref[:, lead + k, :] = ctx_ref[k, b0:b0 + nb, :]
            ext_ref[:, _CTX_ROWS:per_seq, :] = p.reshape(nb, seq_len, d)
            ext = ext_ref[...].reshape(nb * per_seq, d)
            pick = lambda s: s.reshape(nb, per_seq, s.shape[1])[:, _CTX_ROWS:, :].reshape(sub, s.shape[1])
            pos = _PAST_LEN + t_in_tile % seq_len
            for k in range(_POOL_CTX):
                ps_ref[k, b0:b0 + nb, :] = ext_ref[:, per_seq - _POOL_CTX + k, :]
        else:
            ext = jnp.concatenate([prev, p], axis=0)
            pick = lambda s: s[_CTX_ROWS:]
            pos = pos0 + t_in_tile
            prev = p[sub - _CTX_ROWS:]
        ys = []
        for grp, (w, s) in enumerate(zip(_POOL_WINDOWS, _window_sums(ext))):
            cnt = jnp.minimum(pos + 1, w).astype(_F32)
            cols = slice(grp * gc, (grp + 1) * gc)
            dlt = pick(s) / cnt - p[:, cols]
            yg = _dot(dlt.astype(_BF), pw_ref[grp])
            ys.append((yg * psc_ref[:, cols]).astype(_BF))
        o_ref[rows, :] = x + _dot(jnp.concatenate(ys, axis=1), wout_ref[...])
    if not sample:
        carry_ref[...] = prev
        ps_ref[0] = prev


def _odd_body(x_ref, g_ref, win_ref, pw_ref, psc_ref, wout_ref, ctx_ref, o_ref, psp_ref, pss_ref,
              carry_ref, ext_ref, *, n_prompt_tiles, tiles_per_seq, seq_s):
    i = pl.program_id(0)
    shared = (x_ref, g_ref, win_ref, pw_ref, psc_ref, wout_ref, o_ref)

    @pl.when(i < n_prompt_tiles)
    def _():
        @pl.when((i % tiles_per_seq) == 0)
        def _():
            carry_ref[...] = jnp.zeros_like(carry_ref)

        _odd_tile(*shared, psp_ref, sub=_SUB, pos0=(i % tiles_per_seq) * x_ref.shape[0], carry_ref=carry_ref)

    @pl.when(i >= n_prompt_tiles)
    def _():
        _odd_tile(*shared, pss_ref, sub=ext_ref.shape[0] * seq_s, ctx_ref=ctx_ref, ext_ref=ext_ref,
                  seq_len=seq_s)


def _odd_mixer(xc, g, w_in, pool_w, pool_scale, w_out, pool_state, *, n_prompt, seq_len, seq_s):
    t, d = xc.shape
    npt = n_prompt // _TM
    n_seq = n_prompt // seq_len
    tiles_per_seq = seq_len // _TM
    seqs_per_tile = _TM // seq_s
    row = lambda i: (i, 0)
    ctx_spec = lambda **kw: pl.BlockSpec((_POOL_CTX, seqs_per_tile, d), lambda i: (0, jnp.maximum(i - npt, 0), 0), **kw)
    in_specs = [pl.BlockSpec((_TM, d), row), _resident((1, d)), _resident((d, d)), _resident(pool_w.shape),
                _resident((1, d)), _resident((d, d)), ctx_spec(pipeline_mode=pl.Buffered(1))]
    args = [xc, g.reshape(1, d), w_in, pool_w, pool_scale.reshape(1, d), w_out, pool_state]
    out_specs = [pl.BlockSpec((_TM, d), row),
                 pl.BlockSpec((1, _CTX_ROWS, d), lambda i: (jnp.minimum(i // tiles_per_seq, n_seq - 1), 0, 0)),
                 ctx_spec()]
    out_shape = [jax.ShapeDtypeStruct((t, d), _F32), jax.ShapeDtypeStruct((n_seq, _CTX_ROWS, d), _F32),
                 jax.ShapeDtypeStruct(pool_state.shape, _F32)]
    seqs_per_sub = _SUB // 2 // seq_s
    scratch = [pltpu.VMEM((_CTX_ROWS, d), _F32),
               pltpu.VMEM((seqs_per_sub, _CTX_ROWS + seq_s, d), _F32)]
    outs, _ = _call(
        functools.partial(_odd_body, n_prompt_tiles=npt, tiles_per_seq=tiles_per_seq, seq_s=seq_s),
        name="odd", grid=t // _TM, in_specs=in_specs, args=args, out_specs=out_specs, out_shape=out_shape,
        scratch=scratch, aliases={0: 0})
    return outs


def kernel(x_prompt, x_sample, state_conv, state_pool, norm_g, final_norm_g, ffn_w_gate, ffn_w_up, ffn_w_down,
           e_w_in, e_ln_g, e_ln_b, e_sgu_w, e_sgu_b, e_conv_w, e_w_out, o_w_in, o_pool_w, o_pool_scale, o_w_out):
    batch, seq, d = x_prompt.shape
    dec_batch, dec_seq, _ = x_sample.shape
    n_prompt, n_sample = batch * seq, dec_batch * dec_seq
    d_a = _NH_A * _HD_A
    d_b = e_conv_w.shape[-1]
    n_groups, gc = o_pool_w.shape[1:3]
    ffn = functools.partial(_ffn, n_prompt=n_prompt, n_sample=n_sample)
    ffn_casts = lambda layer, k: [_Cast(w, (layer, k)) for w in (ffn_w_gate, ffn_w_up, ffn_w_down)]

    w00 = [w[0, 0].astype(_BF) for w in (ffn_w_gate, ffn_w_up, ffn_w_down)]
    even_casts = [_Cast(e_w_in, (0,)), _Cast(e_w_out, (0,))]
    odd_casts = [_Cast(o_w_in, (0,)), _Cast(o_w_out, (0,)), _Cast(o_pool_w.reshape(-1, n_groups * gc, gc), (0,))]
    (xc,), (e_win, e_wout, *w01) = ffn(
        (x_prompt.reshape(n_prompt, d), x_sample.reshape(n_sample, d)), norm_g[0, 0], *w00,
        casts=even_casts + ffn_casts(0, 1))
    xc, conv_p, xg_s, v_s = _even_mixer(xc, norm_g[0, 1], e_win, e_ln_g[0], e_ln_b[0], e_sgu_w[0], e_sgu_b[0],
                                         e_conv_w[0], e_wout, state_conv[0], n_prompt=n_prompt, seq_len=seq,
                                         chunk_s=dec_seq)
    (xc,), (o_win, o_wout, o_pw, *w10) = ffn((xc,), norm_g[0, 2], *w01, casts=odd_casts + ffn_casts(1, 0))

    (xc,), w11 = ffn((xc,), norm_g[1, 0], *w10, casts=ffn_casts(1, 1))
    xc, pool_p, pool_s = _odd_mixer(xc, norm_g[1, 1], o_win, o_pw.reshape(n_groups, gc, gc), o_pool_scale[0], o_wout,
                                    jnp.transpose(state_pool[0], (1, 0, 2)), n_prompt=n_prompt, seq_len=seq,
                                    seq_s=dec_seq)
    (y_p, y_s), _ = ffn((xc,), norm_g[1, 2], *w11, final_g=final_norm_g)

    keep = _CONV_W - 1
    conv_prompt = conv_p[:, _SUBLANES - keep:][None]
    conv_sample = xg_s.reshape(dec_batch, dec_seq, d_b)[:, dec_seq - keep:][None]
    chunk_v_sample = v_s.reshape(1, dec_batch, dec_seq, d_a)
    pool_prompt = pool_p[:, pool_p.shape[1] - _POOL_CTX:][None]
    pool_sample = jnp.transpose(pool_s, (1, 0, 2))[None]
    return (y_p.reshape(batch, seq, d), y_s.reshape(dec_batch, dec_seq, d), conv_prompt, conv_sample,
            chunk_v_sample, pool_prompt, pool_sample)
```

```python
import collections
import functools

import jax
import jax.numpy as jnp
from jax import lax
from jax.experimental import pallas as pl
from jax.experimental.pallas import tpu as pltpu

_CHUNK = 128
_NH_A = 4
_HD_A = 128
_CONV_W = 3
_POOL_WINDOWS = (2, 4, 8, 16)
_POOL_CTX = max(_POOL_WINDOWS) - 1
_PAST_LEN = 16384
_EPS = 1e-6

_SUBLANES = 8
_BF16_ROWS = 16
_VMEM_LIMIT = 56 * 1024 * 1024

_TM = 1024
_SUB = 512
_FFN_CHUNKS = ((0, 1536), (1536, 1280))
_CTX_ROWS = 2 * _SUBLANES

_BF = jnp.bfloat16
_F32 = jnp.float32


def _rms(x, g):
    return x * lax.rsqrt(jnp.mean(x * x, axis=-1, keepdims=True) + _EPS) * g


def _dot(a, b):
    return jnp.dot(a, b, preferred_element_type=_F32)


def _resident(shape):
    nd = len(shape)
    return pl.BlockSpec(shape, lambda i: (0,) * nd, pipeline_mode=pl.Buffered(1))


_Cast = collections.namedtuple("_Cast", "array lead")


def _cast_specs(cast, grid):
    lead = tuple(cast.lead)
    rows, cols = cast.array.shape[len(lead):]
    steps = max(s for s in (1, 2, 4, 8, 16, 32) if s <= grid and rows % (s * _BF16_ROWS) == 0)
    blk = rows // steps
    step = lambda i: jnp.minimum(i, steps - 1)
    in_spec = pl.BlockSpec((None,) * len(lead) + (blk, cols), lambda i: lead + (step(i), 0))
    out_spec = pl.BlockSpec((blk, cols), lambda i: (step(i), 0))
    return in_spec, out_spec, jax.ShapeDtypeStruct((rows, cols), _BF)


def _call(body, *, name, grid, in_specs, args, out_specs, out_shape, scratch=(), casts=(), aliases=None):
    n_in, n_out, n_cast = len(in_specs), len(out_specs), len(casts)
    specs = [_cast_specs(c, grid) for c in casts]

    def wrapped(*refs):
        ins, refs = refs[:n_in], refs[n_in:]
        cast_in, refs = refs[:n_cast], refs[n_cast:]
        outs, refs = refs[:n_out], refs[n_out:]
        cast_out, scratch_refs = refs[:n_cast], refs[n_cast:]
        for src, dst in zip(cast_in, cast_out):
            dst[...] = src[...].astype(_BF)
        body(*ins, *outs, *scratch_refs)

    res = pl.pallas_call(
        wrapped,
        grid=(grid,),
        in_specs=list(in_specs) + [s[0] for s in specs],
        out_specs=list(out_specs) + [s[1] for s in specs],
        out_shape=list(out_shape) + [s[2] for s in specs],
        scratch_shapes=list(scratch),
        input_output_aliases=aliases or {},
        compiler_params=pltpu.CompilerParams(dimension_semantics=("arbitrary",), vmem_limit_bytes=_VMEM_LIMIT),
        name=name,
    )(*args, *[c.array for c in casts])
    return res[:n_out], res[n_out:]


def _ffn_body(*refs, split_in, final, n_prompt_tiles):
    refs = list(refs)
    i = pl.program_id(0)
    x_refs = [refs.pop(0) for _ in range(2 if split_in else 1)]
    g_ref, wg_ref, wu_ref, wd_ref = refs[:4]
    refs = refs[4:]
    is_prompt = i < n_prompt_tiles
    for r0 in range(0, _TM, _SUB):
        rows = slice(r0, r0 + _SUB)
        if split_in:
            x = jnp.where(is_prompt, x_refs[0][rows, :], x_refs[1][rows, :])
        else:
            x = x_refs[0][rows, :]
        h = _rms(x, g_ref[...]).astype(_BF)
        acc = None
        for c0, cn in _FFN_CHUNKS:
            a = _dot(h, wg_ref[:, c0:c0 + cn])
            b = _dot(h, wu_ref[:, c0:c0 + cn])
            act = (a * jax.nn.sigmoid(a) * b).astype(_BF)
            d = _dot(act, wd_ref[c0:c0 + cn, :])
            acc = d if acc is None else acc + d
        y = x + 0.5 * acc
        if final:
            fg_ref, yp_ref, ys_ref = refs
            y = _rms(y, fg_ref[...])
            to_prompt = jnp.broadcast_to(is_prompt, y.shape)
            pltpu.store(yp_ref.at[rows, :], y, mask=to_prompt)
            pltpu.store(ys_ref.at[rows, :], y, mask=jnp.logical_not(to_prompt))
        else:
            (o_ref,) = refs
            o_ref[rows, :] = y


def _ffn(xs, g, wg, wu, wd, *, n_prompt, n_sample, final_g=None, casts=()):
    d, dff = wg.shape
    npt, nst = n_prompt // _TM, n_sample // _TM
    split_in = len(xs) == 2
    final = final_g is not None
    row = lambda i: (i, 0)
    prow = lambda i: (jnp.minimum(i, npt - 1), 0)
    srow = lambda i: (jnp.maximum(i - npt, 0), 0)
    tile = (_TM, d)
    if split_in:
        in_specs = [pl.BlockSpec(tile, prow), pl.BlockSpec(tile, srow, pipeline_mode=pl.Buffered(1))]
    else:
        in_specs = [pl.BlockSpec(tile, row)]
    in_specs += [_resident((1, d)), _resident((d, dff)), _resident((d, dff)), _resident((dff, d))]
    args = list(xs) + [g.reshape(1, d), wg, wu, wd]
    if final:
        in_specs.append(_resident((1, d)))
        args.append(final_g.reshape(1, d))
        out_specs = [pl.BlockSpec(tile, prow), pl.BlockSpec(tile, srow)]
        out_shape = [jax.ShapeDtypeStruct((n_prompt, d), _F32), jax.ShapeDtypeStruct((n_sample, d), _F32)]
    else:
        out_specs = [pl.BlockSpec(tile, row)]
        out_shape = [jax.ShapeDtypeStruct((n_prompt + n_sample, d), _F32)]
    return _call(
        functools.partial(_ffn_body, split_in=split_in, final=final, n_prompt_tiles=npt),
        name="ffn_final" if final else ("ffn_first" if split_in else "ffn"),
        grid=npt + nst, in_specs=in_specs, args=args, out_specs=out_specs, out_shape=out_shape, casts=casts)


def _gelu(x):
    return 0.5 * x * (1.0 + lax.erf(x * (2.0 ** -0.5)))


def _even_tile(x_ref, g_ref, win_ref, lng_ref, lnb_ref, sw_ref, sb_ref, cw_ref, wout_ref, o_ref, *,
               chunk_len, sub, carry_ref=None, cs_ref=None, b1_ref=None, b2_ref=None, xg_ref=None, v_ref=None):
    sample = carry_ref is None
    tm = x_ref.shape[0]
    d_a = _NH_A * _HD_A
    d_b = cw_ref.shape[1]
    o = 2 * d_a
    tt = lax.broadcasted_iota(jnp.int32, (_CHUNK, _CHUNK), 0)
    ss = lax.broadcasted_iota(jnp.int32, (_CHUNK, _CHUNK), 1)
    keep = (ss <= tt) & ((tt // chunk_len) == (ss // chunk_len))
    mix_w = [jnp.where(keep, sw_ref[hd], 0.0).astype(_BF) for hd in range(_NH_A)]
    cw = cw_ref[...]

    def project(r0, prev):
        rows = slice(r0, r0 + sub)
        x = x_ref[rows, :]
        h = _rms(x, g_ref[...]).astype(_BF)
        z = _dot(h, win_ref[...])
        u = _gelu(z[:, 0:d_a])
        vv = _gelu(z[:, d_a:2 * d_a])
        mu = jnp.mean(vv, axis=-1, keepdims=True)
        vc = vv - mu
        var = jnp.mean(vc * vc, axis=-1, keepdims=True)
        v = vc * lax.rsqrt(var + _EPS) * lng_ref[...] + lnb_ref[...]
        if sample:
            v_ref[rows, :] = v
        gate_b = z[:, o:o + d_b]
        gate_c = z[:, o + d_b:o + 2 * d_b]
        x_in = z[:, o + 2 * d_b:o + 3 * d_b]
        xg = gate_c * x_in
        if sample:
            t8 = lax.broadcasted_iota(jnp.int32, (sub, 1), 0) % chunk_len
            back1 = jnp.where(t8 >= 1, pltpu.roll(xg, 1, axis=0), b1_ref[rows, :])
            back2 = jnp.where(t8 >= 2, pltpu.roll(xg, 2, axis=0), b2_ref[rows, :])
            xg_ref[rows, :] = xg
        else:
            ext = jnp.concatenate([prev, xg], axis=0)
            back1 = pltpu.roll(ext, 1, axis=0)[_SUBLANES:]
            back2 = pltpu.roll(ext, 2, axis=0)[_SUBLANES:]
            prev = xg[sub - _SUBLANES:]
        conv = back2 * cw[0:1] + back1 * cw[1:2] + xg * cw[2:3]
        return (x, u, v.astype(_BF), (gate_b * conv).astype(_BF)), prev

    def mix_and_output(r0, x, u, vb, y_b):
        y_parts = []
        for hd in range(_NH_A):
            cols = slice(hd * _HD_A, (hd + 1) * _HD_A)
            chunks = []
            for c0 in range(0, sub, _CHUNK):
                mixed = _dot(mix_w[hd], vb[c0:c0 + _CHUNK, cols]) + sb_ref[hd]
                chunks.append((u[c0:c0 + _CHUNK, cols] * mixed).astype(_BF))
            y_parts.append(jnp.concatenate(chunks, axis=0))
        y_parts.append(y_b)
        o_ref[r0:r0 + sub, :] = x + _dot(jnp.concatenate(y_parts, axis=1), wout_ref[...])

    prev = None if sample else carry_ref[...]
    staged = []
    for r0 in range(0, tm, sub):
        vals, prev = project(r0, prev)
        staged.append((r0, vals))
    for r0, vals in staged:
        mix_and_output(r0, *vals)
    if not sample:
        carry_ref[...] = prev
        cs_ref[0] = prev


def _even_body(x_ref, g_ref, win_ref, lng_ref, lnb_ref, swp_ref, sbp_ref, sws_ref, sbs_ref, cw_ref, wout_ref,
               b1_ref, b2_ref, o_ref, cs_ref, xg_ref, v_ref, carry_ref, *,
               n_prompt_tiles, tiles_per_seq, chunk_s, sub):
    i = pl.program_id(0)
    shared = (x_ref, g_ref, win_ref, lng_ref, lnb_ref)

    @pl.when(i < n_prompt_tiles)
    def _():
        @pl.when((i % tiles_per_seq) == 0)
        def _():
            carry_ref[...] = jnp.zeros_like(carry_ref)

        _even_tile(*shared, swp_ref, sbp_ref, cw_ref, wout_ref, o_ref, chunk_len=_CHUNK, sub=sub,
                   carry_ref=carry_ref, cs_ref=cs_ref)

    @pl.when(i >= n_prompt_tiles)
    def _():
        _even_tile(*shared, sws_ref, sbs_ref, cw_ref, wout_ref, o_ref, chunk_len=chunk_s, sub=sub,
                   b1_ref=b1_ref, b2_ref=b2_ref, xg_ref=xg_ref, v_ref=v_ref)


def _sgu_tiles(sgu_w, sgu_b, chunk_len):
    reps = _CHUNK // chunk_len
    sw = jnp.tile(sgu_w[:, :chunk_len, :chunk_len], (1, reps, reps))
    sb = jnp.broadcast_to(jnp.tile(sgu_b[:, :chunk_len], (1, reps))[:, :, None], (_NH_A, _CHUNK, _HD_A))
    return sw, sb


def _even_mixer(xc, g, w_in, ln_g, ln_b, sgu_w, sgu_b, conv_w, w_out, conv_state, *, n_prompt, seq_len, chunk_s):
    t, d = xc.shape
    n_sample = t - n_prompt
    d_in = w_in.shape[1]
    d_a = _NH_A * _HD_A
    d_b = conv_w.shape[1]
    npt = n_prompt // _TM
    n_seq = n_prompt // seq_len
    tiles_per_seq = seq_len // _TM
    row = lambda i: (i, 0)
    srow = lambda i: (jnp.maximum(i - npt, 0), 0)
    swp, sbp = _sgu_tiles(sgu_w, sgu_b, _CHUNK)
    sws, sbs = _sgu_tiles(sgu_w, sgu_b, chunk_s)
    pad = lambda a: jnp.pad(a, ((0, 0), (0, chunk_s - a.shape[1]), (0, 0))).reshape(n_sample, d_b)
    b1 = pad(conv_state[:, 1:2])
    b2 = pad(conv_state)
    once = lambda shape, imap: pl.BlockSpec(shape, imap, pipeline_mode=pl.Buffered(1))
    in_specs = [pl.BlockSpec((_TM, d), row), _resident((1, d)), _resident((d, d_in)), _resident((1, d_a)),
                _resident((1, d_a)), _resident(swp.shape), _resident(sbp.shape), _resident(sws.shape),
                _resident(sbs.shape), _resident(conv_w.shape), _resident(w_out.shape),
                once((_TM, d_b), srow), once((_TM, d_b), srow)]
    args = [xc, g.reshape(1, d), w_in, ln_g.reshape(1, d_a), ln_b.reshape(1, d_a), swp, sbp, sws, sbs, conv_w, w_out,
            b1, b2]
    out_specs = [pl.BlockSpec((_TM, d), row),
                 pl.BlockSpec((1, _SUBLANES, d_b), lambda i: (jnp.minimum(i // tiles_per_seq, n_seq - 1), 0, 0)),
                 pl.BlockSpec((_TM, d_b), srow), pl.BlockSpec((_TM, d_a), srow)]
    out_shape = [jax.ShapeDtypeStruct((t, d), _F32), jax.ShapeDtypeStruct((n_seq, _SUBLANES, d_b), _F32),
                 jax.ShapeDtypeStruct((n_sample, d_b), _F32), jax.ShapeDtypeStruct((n_sample, d_a), _F32)]
    scratch = [pltpu.VMEM((_SUBLANES, d_b), _F32)]
    outs, _ = _call(
        functools.partial(_even_body, n_prompt_tiles=npt, tiles_per_seq=tiles_per_seq, chunk_s=chunk_s, sub=_SUB),
        name="even", grid=t // _TM, in_specs=in_specs, args=args, out_specs=out_specs, out_shape=out_shape,
        scratch=scratch, aliases={0: 0})
    return outs


def _window_sums(ext):
    gc = ext.shape[1] // len(_POOL_WINDOWS)
    sums = []
    s = ext
    width = 1
    for w in _POOL_WINDOWS:
        while width < w:
            s = s + pltpu.roll(s, width, axis=0)
            width *= 2
        sums.append(s[:, 0:gc])
        s = s[:, gc:]
    return sums


def _odd_tile(x_ref, g_ref, win_ref, pw_ref, psc_ref, wout_ref, o_ref, ps_ref, *, sub, pos0=None,
              carry_ref=None, ctx_ref=None, ext_ref=None, seq_len=None):
    sample = carry_ref is None
    tm, d = x_ref.shape
    gc = d // len(_POOL_WINDOWS)
    def project(r0, prev):
        rows = slice(r0, r0 + sub)
        x = x_ref[rows, :]
        h = _rms(x, g_ref[...]).astype(_BF)
        p = _dot(h, win_ref[...])
        t_in_tile = r0 + lax.broadcasted_iota(jnp.int32, (sub, 1), 0)
        if sample:
            nb = sub // seq_len
            b0 = r0 // seq_len
            per_seq = _CTX_ROWS + seq_len
            lead = _CTX_ROWS - _POOL_CTX
            ext_ref[:, 0:lead, :] = jnp.zeros((nb, lead, d), _F32)
            for k in range(_POOL_CTX):
                ext_ref[:, lead + k, :] = ctx_ref[k, b0:b0 + nb, :]
            ext_ref[:, _CTX_ROWS:per_seq, :] = p.reshape(nb, seq_len, d)
            ext = ext_ref[...].reshape(nb * per_seq, d)
            pick = lambda s: s.reshape(nb, per_seq, s.shape[1])[:, _CTX_ROWS:, :].reshape(sub, s.shape[1])
            pos = _PAST_LEN + t_in_tile % seq_len
            for k in range(_POOL_CTX):
                ps_ref[k, b0:b0 + nb, :] = ext_ref[:, per_seq - _POOL_CTX + k, :]
        else:
            ext = jnp.concatenate([prev, p], axis=0)
            pick = lambda s: s[_CTX_ROWS:]
            pos = pos0 + t_in_tile
            prev = p[sub - _CTX_ROWS:]
        dlts = []
        for grp, (w, s) in enumerate(zip(_POOL_WINDOWS, _window_sums(ext))):
            cnt = jnp.minimum(pos + 1, w).astype(_F32)
            dlts.append((pick(s) / cnt - p[:, grp * gc:(grp + 1) * gc]).astype(_BF))
        return (x, dlts), prev

    def output(r0, x, dlts):
        ys = []
        for grp, dlt in enumerate(dlts):
            cols = slice(grp * gc, (grp + 1) * gc)
            ys.append((_dot(dlt, pw_ref[grp]) * psc_ref[:, cols]).astype(_BF))
        o_ref[r0:r0 + sub, :] = x + _dot(jnp.concatenate(ys, axis=1), wout_ref[...])

    prev = None if sample else carry_ref[...]
    staged = []
    for r0 in range(0, tm, sub):
        vals, prev = project(r0, prev)
        staged.append((r0, vals))
    for r0, vals in staged:
        output(r0, *vals)
    if not sample:
        carry_ref[...] = prev
        ps_ref[0] = prev


def _odd_body(x_ref, g_ref, win_ref, pw_ref, psc_ref, wout_ref, ctx_ref, o_ref, psp_ref, pss_ref,
              carry_ref, ext_ref, *, n_prompt_tiles, tiles_per_seq, seq_s):
    i = pl.program_id(0)
    shared = (x_ref, g_ref, win_ref, pw_ref, psc_ref, wout_ref, o_ref)

    @pl.when(i < n_prompt_tiles)
    def _():
        @pl.when((i % tiles_per_seq) == 0)
        def _():
            carry_ref[...] = jnp.zeros_like(carry_ref)

        _odd_tile(*shared, psp_ref, sub=_SUB, pos0=(i % tiles_per_seq) * x_ref.shape[0], carry_ref=carry_ref)

    @pl.when(i >= n_prompt_tiles)
    def _():
        _odd_tile(*shared, pss_ref, sub=ext_ref.shape[0] * seq_s, ctx_ref=ctx_ref, ext_ref=ext_ref,
                  seq_len=seq_s)


def _odd_mixer(xc, g, w_in, pool_w, pool_scale, w_out, pool_state, *, n_prompt, seq_len, seq_s):
    t, d = xc.shape
    npt = n_prompt // _TM
    n_seq = n_prompt // seq_len
    tiles_per_seq = seq_len // _TM
    seqs_per_tile = _TM // seq_s
    row = lambda i: (i, 0)
    ctx_spec = lambda **kw: pl.BlockSpec((_POOL_CTX, seqs_per_tile, d), lambda i: (0, jnp.maximum(i - npt, 0), 0), **kw)
    in_specs = [pl.BlockSpec((_TM, d), row), _resident((1, d)), _resident((d, d)), _resident(pool_w.shape),
                _resident((1, d)), _resident((d, d)), ctx_spec(pipeline_mode=pl.Buffered(1))]
    args = [xc, g.reshape(1, d), w_in, pool_w, pool_scale.reshape(1, d), w_out, pool_state]
    out_specs = [pl.BlockSpec((_TM, d), row),
                 pl.BlockSpec((1, _CTX_ROWS, d), lambda i: (jnp.minimum(i // tiles_per_seq, n_seq - 1), 0, 0)),
                 ctx_spec()]
    out_shape = [jax.ShapeDtypeStruct((t, d), _F32), jax.ShapeDtypeStruct((n_seq, _CTX_ROWS, d), _F32),
                 jax.ShapeDtypeStruct(pool_state.shape, _F32)]
    seqs_per_sub = _SUB // 2 // seq_s
    scratch = [pltpu.VMEM((_CTX_ROWS, d), _F32),
               pltpu.VMEM((seqs_per_sub, _CTX_ROWS + seq_s, d), _F32)]
    outs, _ = _call(
        functools.partial(_odd_body, n_prompt_tiles=npt, tiles_per_seq=tiles_per_seq, seq_s=seq_s),
        name="odd", grid=t // _TM, in_specs=in_specs, args=args, out_specs=out_specs, out_shape=out_shape,
        scratch=scratch, aliases={0: 0})
    return outs


def kernel(x_prompt, x_sample, state_conv, state_pool, norm_g, final_norm_g, ffn_w_gate, ffn_w_up, ffn_w_down,
           e_w_in, e_ln_g, e_ln_b, e_sgu_w, e_sgu_b, e_conv_w, e_w_out, o_w_in, o_pool_w, o_pool_scale, o_w_out):
    batch, seq, d = x_prompt.shape
    dec_batch, dec_seq, _ = x_sample.shape
    n_prompt, n_sample = batch * seq, dec_batch * dec_seq
    d_a = _NH_A * _HD_A
    d_b = e_conv_w.shape[-1]
    n_groups, gc = o_pool_w.shape[1:3]
    ffn = functools.partial(_ffn, n_prompt=n_prompt, n_sample=n_sample)
    ffn_casts = lambda layer, k: [_Cast(w, (layer, k)) for w in (ffn_w_gate, ffn_w_up, ffn_w_down)]

    w00 = [w[0, 0].astype(_BF) for w in (ffn_w_gate, ffn_w_up, ffn_w_down)]
    even_casts = [_Cast(e_w_in, (0,)), _Cast(e_w_out, (0,))]
    odd_casts = [_Cast(o_w_in, (0,)), _Cast(o_w_out, (0,)), _Cast(o_pool_w.reshape(-1, n_groups * gc, gc), (0,))]
    (xc,), (e_win, e_wout, *w01) = ffn(
        (x_prompt.reshape(n_prompt, d), x_sample.reshape(n_sample, d)), norm_g[0, 0], *w00,
        casts=even_casts + ffn_casts(0, 1))
    xc, conv_p, xg_s, v_s = _even_mixer(xc, norm_g[0, 1], e_win, e_ln_g[0], e_ln_b[0], e_sgu_w[0], e_sgu_b[0],
                                         e_conv_w[0], e_wout, state_conv[0], n_prompt=n_prompt, seq_len=seq,
                                         chunk_s=dec_seq)
    (xc,), (o_win, o_wout, o_pw, *w10) = ffn((xc,), norm_g[0, 2], *w01, casts=odd_casts + ffn_casts(1, 0))

    (xc,), w11 = ffn((xc,), norm_g[1, 0], *w10, casts=ffn_casts(1, 1))
    xc, pool_p, pool_s = _odd_mixer(xc, norm_g[1, 1], o_win, o_pw.reshape(n_groups, gc, gc), o_pool_scale[0], o_wout,
                                    jnp.transpose(state_pool[0], (1, 0, 2)), n_prompt=n_prompt, seq_len=seq,
                                    seq_s=dec_seq)
    (y_p, y_s), _ = ffn((xc,), norm_g[1, 2], *w11, final_g=final_norm_g)

    keep = _CONV_W - 1
    conv_prompt = conv_p[:, _SUBLANES - keep:][None]
    conv_sample = xg_s.reshape(dec_batch, dec_seq, d_b)[:, dec_seq - keep:][None]
    chunk_v_sample = v_s.reshape(1, dec_batch, dec_seq, d_a)
    pool_prompt = pool_p[:, pool_p.shape[1] - _POOL_CTX:][None]
    pool_sample = jnp.transpose(pool_s, (1, 0, 2))[None]
    return (y_p.reshape(batch, seq, d), y_s.reshape(dec_batch, dec_seq, d), conv_prompt, conv_sample,
            chunk_v_sample, pool_prompt, pool_sample)
```

```python
import collections
import functools

import jax
import jax.numpy as jnp
from jax import lax
from jax.experimental import pallas as pl
from jax.experimental.pallas import tpu as pltpu

_CHUNK = 128
_NH_A = 4
_HD_A = 128
_CONV_W = 3
_POOL_WINDOWS = (2, 4, 8, 16)
_POOL_CTX = max(_POOL_WINDOWS) - 1
_PAST_LEN = 16384
_EPS = 1e-6

_SUBLANES = 8
_BF16_ROWS = 16
_VMEM_LIMIT = 56 * 1024 * 1024

_TM = 1024
_SUB = 512
_FFN_CHUNKS = ((0, 1536), (1536, 1280))
_CTX_ROWS = 2 * _SUBLANES

_BF = jnp.bfloat16
_F32 = jnp.float32


def _rms(x, g):
    return x * lax.rsqrt(jnp.mean(x * x, axis=-1, keepdims=True) + _EPS) * g


def _dot(a, b):
    return jnp.dot(a, b, preferred_element_type=_F32)


def _resident(shape):
    nd = len(shape)
    return pl.BlockSpec(shape, lambda i: (0,) * nd, pipeline_mode=pl.Buffered(1))


def _rows(v):
    return jnp.broadcast_to(v.reshape(-1, v.shape[-1])[:1], (_SUBLANES, v.shape[-1]))


def _gain(gains, per_layer, layer, j):
    k = layer * per_layer + j
    spec = pl.BlockSpec((None,) + gains.shape[1:], lambda i: (k, 0, 0), pipeline_mode=pl.Buffered(1))
    return gains, spec


_Cast = collections.namedtuple("_Cast", "array lead")


def _cast_specs(cast, grid):
    lead = tuple(cast.lead)
    rows, cols = cast.array.shape[len(lead):]
    steps = max(s for s in (1, 2, 4, 8, 16, 32) if s <= grid and rows % (s * _BF16_ROWS) == 0)
    blk = rows // steps
    step = lambda i: jnp.minimum(i, steps - 1)
    in_spec = pl.BlockSpec((None,) * len(lead) + (blk, cols), lambda i: lead + (step(i), 0))
    out_spec = pl.BlockSpec((blk, cols), lambda i: (step(i), 0))
    return in_spec, out_spec, jax.ShapeDtypeStruct((rows, cols), _BF)


def _call(body, *, name, grid, in_specs, args, out_specs, out_shape, scratch=(), casts=()):
    n_in, n_out, n_cast = len(in_specs), len(out_specs), len(casts)
    specs = [_cast_specs(c, grid) for c in casts]

    def wrapped(*refs):
        ins, refs = refs[:n_in], refs[n_in:]
        cast_in, refs = refs[:n_cast], refs[n_cast:]
        outs, refs = refs[:n_out], refs[n_out:]
        cast_out, scratch_refs = refs[:n_cast], refs[n_cast:]
        for src, dst in zip(cast_in, cast_out):
            dst[...] = src[...].astype(_BF)
        body(*ins, *outs, *scratch_refs)

    res = pl.pallas_call(
        wrapped,
        grid=(grid,),
        in_specs=list(in_specs) + [s[0] for s in specs],
        out_specs=list(out_specs) + [s[1] for s in specs],
        out_shape=list(out_shape) + [s[2] for s in specs],
        scratch_shapes=list(scratch),
        compiler_params=pltpu.CompilerParams(dimension_semantics=("arbitrary",), vmem_limit_bytes=_VMEM_LIMIT),
        name=name,
    )(*args, *[c.array for c in casts])
    return res[:n_out], res[n_out:]


def _ffn_body(*refs, split_in, final, n_prompt_tiles):
    refs = list(refs)
    i = pl.program_id(0)
    x_refs = [refs.pop(0) for _ in range(2 if split_in else 1)]
    g_ref, wg_ref, wu_ref, wd_ref = refs[:4]
    refs = refs[4:]
    is_prompt = i < n_prompt_tiles
    for r0 in range(0, _TM, _SUB):
        rows = slice(r0, r0 + _SUB)
        if split_in:
            x = jnp.where(is_prompt, x_refs[0][rows, :], x_refs[1][rows, :])
        else:
            x = x_refs[0][rows, :]
        h = _rms(x, g_ref[0:1, :]).astype(_BF)
        acc = None
        for c0, cn in _FFN_CHUNKS:
            a = _dot(h, wg_ref[:, c0:c0 + cn])
            b = _dot(h, wu_ref[:, c0:c0 + cn])
            act = (a * jax.nn.sigmoid(a) * b).astype(_BF)
            d = _dot(act, wd_ref[c0:c0 + cn, :])
            acc = d if acc is None else acc + d
        y = x + 0.5 * acc
        if final:
            fg_ref, yp_ref, ys_ref = refs
            y = _rms(y, fg_ref[0:1, :])
            to_prompt = jnp.broadcast_to(is_prompt, y.shape)
            pltpu.store(yp_ref.at[rows, :], y, mask=to_prompt)
            pltpu.store(ys_ref.at[rows, :], y, mask=jnp.logical_not(to_prompt))
        else:
            (o_ref,) = refs
            o_ref[rows, :] = y


def _ffn(xs, g, wg, wu, wd, *, n_prompt, n_sample, final_g=None, casts=()):
    d, dff = wg.shape
    npt, nst = n_prompt // _TM, n_sample // _TM
    split_in = len(xs) == 2
    final = final_g is not None
    row = lambda i: (i, 0)
    prow = lambda i: (jnp.minimum(i, npt - 1), 0)
    srow = lambda i: (jnp.maximum(i - npt, 0), 0)
    tile = (_TM, d)
    if split_in:
        in_specs = [pl.BlockSpec(tile, prow), pl.BlockSpec(tile, srow, pipeline_mode=pl.Buffered(1))]
    else:
        in_specs = [pl.BlockSpec(tile, row)]
    in_specs += [g[1], _resident((d, dff)), _resident((d, dff)), _resident((dff, d))]
    args = list(xs) + [g[0], wg, wu, wd]
    if final:
        in_specs.append(_resident((_SUBLANES, d)))
        args.append(_rows(final_g))
        out_specs = [pl.BlockSpec(tile, prow), pl.BlockSpec(tile, srow)]
        out_shape = [jax.ShapeDtypeStruct((n_prompt, d), _F32), jax.ShapeDtypeStruct((n_sample, d), _F32)]
    else:
        out_specs = [pl.BlockSpec(tile, row)]
        out_shape = [jax.ShapeDtypeStruct((n_prompt + n_sample, d), _F32)]
    return _call(
        functools.partial(_ffn_body, split_in=split_in, final=final, n_prompt_tiles=npt),
        name="ffn_final" if final else ("ffn_first" if split_in else "ffn"),
        grid=npt + nst, in_specs=in_specs, args=args, out_specs=out_specs, out_shape=out_shape, casts=casts)


def _gelu(x):
    return 0.5 * x * (1.0 + lax.erf(x * (2.0 ** -0.5)))


def _even_tile(x_ref, g_ref, win_ref, lng_ref, lnb_ref, sw_ref, sb_ref, cw_ref, wout_ref, o_ref, *,
               chunk_len, sub, carry_ref=None, cs_ref=None, b1_ref=None, b2_ref=None, xg_ref=None, v_ref=None):
    sample = carry_ref is None
    tm = x_ref.shape[0]
    d_a = _NH_A * _HD_A
    d_b = cw_ref.shape[1]
    o = 2 * d_a
    tt = lax.broadcasted_iota(jnp.int32, (_CHUNK, _CHUNK), 0)
    ss = lax.broadcasted_iota(jnp.int32, (_CHUNK, _CHUNK), 1)
    keep = (ss <= tt) & ((tt // chunk_len) == (ss // chunk_len))
    mix_w = [jnp.where(keep, sw_ref[hd], 0.0).astype(_BF) for hd in range(_NH_A)]
    cw = cw_ref[...]

    def project(r0, prev):
        rows = slice(r0, r0 + sub)
        x = x_ref[rows, :]
        h = _rms(x, g_ref[0:1, :]).astype(_BF)
        z = _dot(h, win_ref[...])
        u = _gelu(z[:, 0:d_a])
        vv = _gelu(z[:, d_a:2 * d_a])
        mu = jnp.mean(vv, axis=-1, keepdims=True)
        vc = vv - mu
        var = jnp.mean(vc * vc, axis=-1, keepdims=True)
        v = vc * lax.rsqrt(var + _EPS) * lng_ref[0:1, :] + lnb_ref[0:1, :]
        if sample:
            v_ref[rows, :] = v
        gate_b = z[:, o:o + d_b]
        gate_c = z[:, o + d_b:o + 2 * d_b]
        x_in = z[:, o + 2 * d_b:o + 3 * d_b]
        xg = gate_c * x_in
        if sample:
            t8 = lax.broadcasted_iota(jnp.int32, (sub, 1), 0) % chunk_len
            back1 = jnp.where(t8 >= 1, pltpu.roll(xg, 1, axis=0), b1_ref[rows, :])
            back2 = jnp.where(t8 >= 2, pltpu.roll(xg, 2, axis=0), b2_ref[rows, :])
            xg_ref[rows, :] = xg
        else:
            ext = jnp.concatenate([prev, xg], axis=0)
            back1 = pltpu.roll(ext, 1, axis=0)[_SUBLANES:]
            back2 = pltpu.roll(ext, 2, axis=0)[_SUBLANES:]
            prev = xg[sub - _SUBLANES:]
        conv = back2 * cw[0:1] + back1 * cw[1:2] + xg * cw[2:3]
        return (x, u, v.astype(_BF), (gate_b * conv).astype(_BF)), prev

    def mix_and_output(r0, x, u, vb, y_b):
        y_parts = []
        for hd in range(_NH_A):
            cols = slice(hd * _HD_A, (hd + 1) * _HD_A)
            chunks = []
            for c0 in range(0, sub, _CHUNK):
                mixed = _dot(mix_w[hd], vb[c0:c0 + _CHUNK, cols]) + sb_ref[hd]
                chunks.append((u[c0:c0 + _CHUNK, cols] * mixed).astype(_BF))
            y_parts.append(jnp.concatenate(chunks, axis=0))
        y_parts.append(y_b)
        o_ref[r0:r0 + sub, :] = x + _dot(jnp.concatenate(y_parts, axis=1), wout_ref[...])

    prev = None if sample else carry_ref[...]
    staged = []
    for r0 in range(0, tm, sub):
        vals, prev = project(r0, prev)
        staged.append((r0, vals))
    for r0, vals in staged:
        mix_and_output(r0, *vals)
    if not sample:
        carry_ref[...] = prev
        cs_ref[0] = prev


def _even_body(x_ref, g_ref, win_ref, lng_ref, lnb_ref, swp_ref, sbp_ref, sws_ref, sbs_ref, cw_ref, wout_ref,
               b1_ref, b2_ref, o_ref, cs_ref, xg_ref, v_ref, carry_ref, *,
               n_prompt_tiles, tiles_per_seq, chunk_s, sub):
    i = pl.program_id(0)
    shared = (x_ref, g_ref, win_ref, lng_ref, lnb_ref)

    @pl.when(i < n_prompt_tiles)
    def _():
        @pl.when((i % tiles_per_seq) == 0)
        def _():
            carry_ref[...] = jnp.zeros_like(carry_ref)

        _even_tile(*shared, swp_ref, sbp_ref, cw_ref, wout_ref, o_ref, chunk_len=_CHUNK, sub=sub,
                   carry_ref=carry_ref, cs_ref=cs_ref)

    @pl.when(i >= n_prompt_tiles)
    def _():
        _even_tile(*shared, sws_ref, sbs_ref, cw_ref, wout_ref, o_ref, chunk_len=chunk_s, sub=sub,
                   b1_ref=b1_ref, b2_ref=b2_ref, xg_ref=xg_ref, v_ref=v_ref)


def _sgu_tiles(sgu_w, sgu_b, chunk_len):
    reps = _CHUNK // chunk_len
    sw = jnp.tile(sgu_w[:, :chunk_len, :chunk_len], (1, reps, reps))
    sb = jnp.tile(sgu_b[:, :chunk_len], (1, reps))[:, :, None]
    return sw, sb


def _even_mixer(xc, g, w_in, ln_g, ln_b, sgu_w, sgu_b, conv_w, w_out, conv_state, *, n_prompt, seq_len, chunk_s):
    t, d = xc.shape
    n_sample = t - n_prompt
    d_in = w_in.shape[1]
    d_a = _NH_A * _HD_A
    d_b = conv_w.shape[1]
    npt = n_prompt // _TM
    n_seq = n_prompt // seq_len
    tiles_per_seq = seq_len // _TM
    row = lambda i: (i, 0)
    srow = lambda i: (jnp.maximum(i - npt, 0), 0)
    cw8 = jnp.pad(conv_w, ((0, _SUBLANES - conv_w.shape[0]), (0, 0)))
    swp, sbp = _sgu_tiles(sgu_w, sgu_b, _CHUNK)
    sws, sbs = _sgu_tiles(sgu_w, sgu_b, chunk_s)
    pad = lambda a: jnp.pad(a, ((0, 0), (0, chunk_s - a.shape[1]), (0, 0))).reshape(n_sample, d_b)
    b1 = pad(conv_state[:, 1:2])
    b2 = pad(conv_state)
    once = lambda shape, imap: pl.BlockSpec(shape, imap, pipeline_mode=pl.Buffered(1))
    in_specs = [pl.BlockSpec((_TM, d), row), g[1], _resident((d, d_in)), _resident((_SUBLANES, d_a)),
                _resident((_SUBLANES, d_a)), _resident(swp.shape), _resident(sbp.shape), _resident(sws.shape),
                _resident(sbs.shape), _resident(cw8.shape), _resident(w_out.shape),
                once((_TM, d_b), srow), once((_TM, d_b), srow)]
    args = [xc, g[0], w_in, _rows(ln_g), _rows(ln_b), swp, sbp, sws, sbs, cw8, w_out,
            b1, b2]
    out_specs = [pl.BlockSpec((_TM, d), row),
                 pl.BlockSpec((1, _SUBLANES, d_b), lambda i: (jnp.minimum(i // tiles_per_seq, n_seq - 1), 0, 0)),
                 pl.BlockSpec((_TM, d_b), srow), pl.BlockSpec((_TM, d_a), srow)]
    out_shape = [jax.ShapeDtypeStruct((t, d), _F32), jax.ShapeDtypeStruct((n_seq, _SUBLANES, d_b), _F32),
                 jax.ShapeDtypeStruct((n_sample, d_b), _F32), jax.ShapeDtypeStruct((n_sample, d_a), _F32)]
    scratch = [pltpu.VMEM((_SUBLANES, d_b), _F32)]
    outs, _ = _call(
        functools.partial(_even_body, n_prompt_tiles=npt, tiles_per_seq=tiles_per_seq, chunk_s=chunk_s, sub=_SUB),
        name="even", grid=t // _TM, in_specs=in_specs, args=args, out_specs=out_specs, out_shape=out_shape,
        scratch=scratch)
    return outs


def _window_sums(ext):
    gc = ext.shape[1] // len(_POOL_WINDOWS)
    sums = []
    s = ext
    width = 1
    for w in _POOL_WINDOWS:
        while width < w:
            s = s + pltpu.roll(s, width, axis=0)
            width *= 2
        sums.append(s[:, 0:gc])
        s = s[:, gc:]
    return sums


def _odd_tile(x_ref, g_ref, win_ref, pw_ref, psc_ref, wout_ref, o_ref, ps_ref, *, sub, pos0=None,
              carry_ref=None, ctx_ref=None, ext_ref=None, seq_len=None):
    sample = carry_ref is None
    tm, d = x_ref.shape
    gc = d // len(_POOL_WINDOWS)
    def project(r0, prev):
        rows = slice(r0, r0 + sub)
        x = x_ref[rows, :]
        h = _rms(x, g_ref[0:1, :]).astype(_BF)
        p = _dot(h, win_ref[...])
        t_in_tile = r0 + lax.broadcasted_iota(jnp.int32, (sub, 1), 0)
        if sample:
            nb = sub // seq_len
            b0 = r0 // seq_len
            per_seq = _CTX_ROWS + seq_len
            lead = _CTX_ROWS - _POOL_CTX
            ext_ref[:, 0:lead, :] = jnp.zeros((nb, lead, d), _F32)
            for k in range(_POOL_CTX):
                ext_ref[:, lead + k, :] = ctx_ref[k, b0:b0 + nb, :]
            ext_ref[:, _CTX_ROWS:per_seq, :] = p.reshape(nb, seq_len, d)
            ext = ext_ref[...].reshape(nb * per_seq, d)
            pick = lambda s: s.reshape(nb, per_seq, s.shape[1])[:, _CTX_ROWS:, :].reshape(sub, s.shape[1])
            pos = _PAST_LEN + t_in_tile % seq_len
            for k in range(_POOL_CTX):
                ps_ref[k, b0:b0 + nb, :] = ext_ref[:, per_seq - _POOL_CTX + k, :]
        else:
            ext = jnp.concatenate([prev, p], axis=0)
            pick = lambda s: s[_CTX_ROWS:]
            pos = pos0 + t_in_tile
            prev = p[sub - _CTX_ROWS:]
        dlts = []
        for grp, (w, s) in enumerate(zip(_POOL_WINDOWS, _window_sums(ext))):
            cnt = jnp.minimum(pos + 1, w).astype(_F32)
            dlts.append((pick(s) / cnt - p[:, grp * gc:(grp + 1) * gc]).astype(_BF))
        return (x, dlts), prev

    def output(r0, x, dlts):
        ys = []
        for grp, dlt in enumerate(dlts):
            cols = slice(grp * gc, (grp + 1) * gc)
            ys.append((_dot(dlt, pw_ref[grp]) * psc_ref[0:1, cols]).astype(_BF))
        o_ref[r0:r0 + sub, :] = x + _dot(jnp.concatenate(ys, axis=1), wout_ref[...])

    prev = None if sample else carry_ref[...]
    staged = []
    for r0 in range(0, tm, sub):
        vals, prev = project(r0, prev)
        staged.append((r0, vals))
    for r0, vals in staged:
        output(r0, *vals)
    if not sample:
        carry_ref[...] = prev
        ps_ref[0] = prev


def _odd_body(x_ref, g_ref, win_ref, pw_ref, psc_ref, wout_ref, ctx_ref, o_ref, psp_ref, pss_ref,
              carry_ref, ext_ref, *, n_prompt_tiles, tiles_per_seq, seq_s):
    i = pl.program_id(0)
    shared = (x_ref, g_ref, win_ref, pw_ref, psc_ref, wout_ref, o_ref)

    @pl.when(i < n_prompt_tiles)
    def _():
        @pl.when((i % tiles_per_seq) == 0)
        def _():
            carry_ref[...] = jnp.zeros_like(carry_ref)

        _odd_tile(*shared, psp_ref, sub=_SUB // 2, pos0=(i % tiles_per_seq) * x_ref.shape[0], carry_ref=carry_ref)

    @pl.when(i >= n_prompt_tiles)
    def _():
        _odd_tile(*shared, pss_ref, sub=ext_ref.shape[0] * seq_s, ctx_ref=ctx_ref, ext_ref=ext_ref,
                  seq_len=seq_s)


def _odd_mixer(xc, g, w_in, pool_w, pool_scale, w_out, pool_state, *, n_prompt, seq_len, seq_s):
    t, d = xc.shape
    npt = n_prompt // _TM
    n_seq = n_prompt // seq_len
    tiles_per_seq = seq_len // _TM
    seqs_per_tile = _TM // seq_s
    row = lambda i: (i, 0)
    ctx_spec = lambda **kw: pl.BlockSpec((_POOL_CTX, seqs_per_tile, d), lambda i: (0, jnp.maximum(i - npt, 0), 0), **kw)
    in_specs = [pl.BlockSpec((_TM, d), row), g[1], _resident((d, d)), _resident(pool_w.shape),
                _resident((_SUBLANES, d)), _resident((d, d)), ctx_spec(pipeline_mode=pl.Buffered(1))]
    args = [xc, g[0], w_in, pool_w, _rows(pool_scale), w_out, pool_state]
    out_specs = [pl.BlockSpec((_TM, d), row),
                 pl.BlockSpec((1, _CTX_ROWS, d), lambda i: (jnp.minimum(i // tiles_per_seq, n_seq - 1), 0, 0)),
                 ctx_spec()]
    out_shape = [jax.ShapeDtypeStruct((t, d), _F32), jax.ShapeDtypeStruct((n_seq, _CTX_ROWS, d), _F32),
                 jax.ShapeDtypeStruct(pool_state.shape, _F32)]
    seqs_per_sub = _SUB // 2 // seq_s
    scratch = [pltpu.VMEM((_CTX_ROWS, d), _F32),
               pltpu.VMEM((seqs_per_sub, _CTX_ROWS + seq_s, d), _F32)]
    outs, _ = _call(
        functools.partial(_odd_body, n_prompt_tiles=npt, tiles_per_seq=tiles_per_seq, seq_s=seq_s),
        name="odd", grid=t // _TM, in_specs=in_specs, args=args, out_specs=out_specs, out_shape=out_shape,
        scratch=scratch)
    return outs


def kernel(x_prompt, x_sample, state_conv, state_pool, norm_g, final_norm_g, ffn_w_gate, ffn_w_up, ffn_w_down,
           e_w_in, e_ln_g, e_ln_b, e_sgu_w, e_sgu_b, e_conv_w, e_w_out, o_w_in, o_pool_w, o_pool_scale, o_w_out):
    batch, seq, d = x_prompt.shape
    dec_batch, dec_seq, _ = x_sample.shape
    n_prompt, n_sample = batch * seq, dec_batch * dec_seq
    d_a = _NH_A * _HD_A
    d_b = e_conv_w.shape[-1]
    n_groups, gc = o_pool_w.shape[1:3]
    ffn = functools.partial(_ffn, n_prompt=n_prompt, n_sample=n_sample)
    ffn_casts = lambda layer, k: [_Cast(w, (layer, k)) for w in (ffn_w_gate, ffn_w_up, ffn_w_down)]
    n_layers, per_layer = norm_g.shape[:2]
    gains = jnp.broadcast_to(norm_g.reshape(-1, 1, d), (n_layers * per_layer, _SUBLANES, d))
    gain = functools.partial(_gain, gains, per_layer)

    w00 = [w[0, 0].astype(_BF) for w in (ffn_w_gate, ffn_w_up, ffn_w_down)]
    even_casts = [_Cast(e_w_in, (0,)), _Cast(e_w_out, (0,))]
    odd_casts = [_Cast(o_w_in, (0,)), _Cast(o_w_out, (0,)), _Cast(o_pool_w.reshape(-1, n_groups * gc, gc), (0,))]
    (xc,), (e_win, e_wout, *w01) = ffn(
        (x_prompt.reshape(n_prompt, d), x_sample.reshape(n_sample, d)), gain(0, 0), *w00,
        casts=even_casts + ffn_casts(0, 1))
    xc, conv_p, xg_s, v_s = _even_mixer(xc, gain(0, 1), e_win, e_ln_g[0], e_ln_b[0], e_sgu_w[0], e_sgu_b[0],
                                         e_conv_w[0], e_wout, state_conv[0], n_prompt=n_prompt, seq_len=seq,
                                         chunk_s=dec_seq)
    (xc,), (o_win, o_wout, o_pw, *w10) = ffn((xc,), gain(0, 2), *w01, casts=odd_casts + ffn_casts(1, 0))

    (xc,), w11 = ffn((xc,), gain(1, 0), *w10, casts=ffn_casts(1, 1))
    xc, pool_p, pool_s = _odd_mixer(xc, gain(1, 1), o_win, o_pw.reshape(n_groups, gc, gc), o_pool_scale[0], o_wout,
                                    jnp.transpose(state_pool[0], (1, 0, 2)), n_prompt=n_prompt, seq_len=seq,
                                    seq_s=dec_seq)
    (y_p, y_s), _ = ffn((xc,), gain(1, 2), *w11, final_g=final_norm_g)

    keep = _CONV_W - 1
    conv_prompt = conv_p[:, _SUBLANES - keep:][None]
    conv_sample = xg_s.reshape(dec_batch, dec_seq, d_b)[:, dec_seq - keep:][None]
    chunk_v_sample = v_s.reshape(1, dec_batch, dec_seq, d_a)
    pool_prompt = pool_p[:, pool_p.shape[1] - _POOL_CTX:][None]
    pool_sample = jnp.transpose(pool_s, (1, 0, 2))[None]
    return (y_p.reshape(batch, seq, d), y_s.reshape(dec_batch, dec_seq, d), conv_prompt, conv_sample,
            chunk_v_sample, pool_prompt, pool_sample)
```

```python
import collections
import functools

import jax
import jax.numpy as jnp
from jax import lax
from jax.experimental import pallas as pl
from jax.experimental.pallas import tpu as pltpu

_CHUNK = 128
_NH_A = 4
_HD_A = 128
_CONV_W = 3
_POOL_WINDOWS = (2, 4, 8, 16)
_POOL_CTX = max(_POOL_WINDOWS) - 1
_PAST_LEN = 16384
_EPS = 1e-6

_SUBLANES = 8
_BF16_ROWS = 16
_VMEM_LIMIT = 56 * 1024 * 1024

_TM = 1024
_SUB = 512
_FFN_CHUNKS = ((0, 1536), (1536, 1280))
_CTX_ROWS = 2 * _SUBLANES

_BF = jnp.bfloat16
_F32 = jnp.float32


def _rms(x, g):
    return x * lax.rsqrt(jnp.mean(x * x, axis=-1, keepdims=True) + _EPS) * g


def _dot(a, b):
    return jnp.dot(a, b, preferred_element_type=_F32)


def _resident(shape):
    nd = len(shape)
    return pl.BlockSpec(shape, lambda i: (0,) * nd, pipeline_mode=pl.Buffered(1))


def _gain(norm_g, layer, j):
    n, m, d = norm_g.shape
    k = layer * m + j
    spec = pl.BlockSpec((None, 1, d), lambda i: (k, 0, 0), pipeline_mode=pl.Buffered(1))
    return norm_g.reshape(n * m, 1, d), spec


_Cast = collections.namedtuple("_Cast", "array lead")


def _cast_specs(cast, grid):
    lead = tuple(cast.lead)
    rows, cols = cast.array.shape[len(lead):]
    steps = max(s for s in (1, 2, 4, 8, 16, 32) if s <= grid and rows % (s * _BF16_ROWS) == 0)
    blk = rows // steps
    step = lambda i: jnp.minimum(i, steps - 1)
    in_spec = pl.BlockSpec((None,) * len(lead) + (blk, cols), lambda i: lead + (step(i), 0))
    out_spec = pl.BlockSpec((blk, cols), lambda i: (step(i), 0))
    return in_spec, out_spec, jax.ShapeDtypeStruct((rows, cols), _BF)


def _call(body, *, name, grid, in_specs, args, out_specs, out_shape, scratch=(), casts=()):
    n_in, n_out, n_cast = len(in_specs), len(out_specs), len(casts)
    specs = [_cast_specs(c, grid) for c in casts]

    def wrapped(*refs):
        ins, refs = refs[:n_in], refs[n_in:]
        cast_in, refs = refs[:n_cast], refs[n_cast:]
        outs, refs = refs[:n_out], refs[n_out:]
        cast_out, scratch_refs = refs[:n_cast], refs[n_cast:]
        for src, dst in zip(cast_in, cast_out):
            dst[...] = src[...].astype(_BF)
        body(*ins, *outs, *scratch_refs)

    res = pl.pallas_call(
        wrapped,
        grid=(grid,),
        in_specs=list(in_specs) + [s[0] for s in specs],
        out_specs=list(out_specs) + [s[1] for s in specs],
        out_shape=list(out_shape) + [s[2] for s in specs],
        scratch_shapes=list(scratch),
        compiler_params=pltpu.CompilerParams(dimension_semantics=("arbitrary",), vmem_limit_bytes=_VMEM_LIMIT),
        name=name,
    )(*args, *[c.array for c in casts])
    return res[:n_out], res[n_out:]


def _ffn_body(*refs, split_in, final, n_prompt_tiles):
    refs = list(refs)
    i = pl.program_id(0)
    x_refs = [refs.pop(0) for _ in range(2 if split_in else 1)]
    g_ref, wg_ref, wu_ref, wd_ref = refs[:4]
    refs = refs[4:]
    is_prompt = i < n_prompt_tiles
    for r0 in range(0, _TM, _SUB):
        rows = slice(r0, r0 + _SUB)
        if split_in:
            x = jnp.where(is_prompt, x_refs[0][rows, :], x_refs[1][rows, :])
        else:
            x = x_refs[0][rows, :]
        h = _rms(x, g_ref[...]).astype(_BF)
        acc = None
        for c0, cn in _FFN_CHUNKS:
            a = _dot(h, wg_ref[:, c0:c0 + cn])
            b = _dot(h, wu_ref[:, c0:c0 + cn])
            act = (a * jax.nn.sigmoid(a) * b).astype(_BF)
            d = _dot(act, wd_ref[c0:c0 + cn, :])
            acc = d if acc is None else acc + d
        y = x + 0.5 * acc
        if final:
            fg_ref, yp_ref, ys_ref = refs
            y = _rms(y, fg_ref[...])
            to_prompt = jnp.broadcast_to(is_prompt, y.shape)
            pltpu.store(yp_ref.at[rows, :], y, mask=to_prompt)
            pltpu.store(ys_ref.at[rows, :], y, mask=jnp.logical_not(to_prompt))
        else:
            (o_ref,) = refs
            o_ref[rows, :] = y


def _ffn(xs, g, wg, wu, wd, *, n_prompt, n_sample, final_g=None, casts=()):
    d, dff = wg.shape
    npt, nst = n_prompt // _TM, n_sample // _TM
    split_in = len(xs) == 2
    final = final_g is not None
    row = lambda i: (i, 0)
    prow = lambda i: (jnp.minimum(i, npt - 1), 0)
    srow = lambda i: (jnp.maximum(i - npt, 0), 0)
    tile = (_TM, d)
    if split_in:
        in_specs = [pl.BlockSpec(tile, prow), pl.BlockSpec(tile, srow, pipeline_mode=pl.Buffered(1))]
    else:
        in_specs = [pl.BlockSpec(tile, row)]
    in_specs += [g[1], _resident((d, dff)), _resident((d, dff)), _resident((dff, d))]
    args = list(xs) + [g[0], wg, wu, wd]
    if final:
        in_specs.append(_resident((1, d)))
        args.append(final_g.reshape(1, d))
        out_specs = [pl.BlockSpec(tile, prow), pl.BlockSpec(tile, srow)]
        out_shape = [jax.ShapeDtypeStruct((n_prompt, d), _F32), jax.ShapeDtypeStruct((n_sample, d), _F32)]
    else:
        out_specs = [pl.BlockSpec(tile, row)]
        out_shape = [jax.ShapeDtypeStruct((n_prompt + n_sample, d), _F32)]
    return _call(
        functools.partial(_ffn_body, split_in=split_in, final=final, n_prompt_tiles=npt),
        name="ffn_final" if final else ("ffn_first" if split_in else "ffn"),
        grid=npt + nst, in_specs=in_specs, args=args, out_specs=out_specs, out_shape=out_shape, casts=casts)


def _gelu(x):
    return 0.5 * x * (1.0 + lax.erf(x * (2.0 ** -0.5)))


def _even_tile(x_ref, g_ref, win_ref, lng_ref, lnb_ref, sw_ref, sb_ref, cw_ref, wout_ref, o_ref, *,
               chunk_len, sub, carry_ref=None, cs_ref=None, st_ref=None, seq_ref=None, v_ref=None):
    sample = carry_ref is None
    tm = x_ref.shape[0]
    d_a = _NH_A * _HD_A
    d_b = cw_ref.shape[1]
    o = 2 * d_a
    tt = lax.broadcasted_iota(jnp.int32, (_CHUNK, _CHUNK), 0)
    ss = lax.broadcasted_iota(jnp.int32, (_CHUNK, _CHUNK), 1)
    keep = (ss <= tt) & ((tt // chunk_len) == (ss // chunk_len))
    mix_w = [jnp.where(keep, sw_ref[hd], 0.0).astype(_BF) for hd in range(_NH_A)]
    cw = cw_ref[...]

    def project(r0, prev):
        rows = slice(r0, r0 + sub)
        x = x_ref[rows, :]
        h = _rms(x, g_ref[...]).astype(_BF)
        z = _dot(h, win_ref[...])
        u = _gelu(z[:, 0:d_a])
        vv = _gelu(z[:, d_a:2 * d_a])
        mu = jnp.mean(vv, axis=-1, keepdims=True)
        vc = vv - mu
        var = jnp.mean(vc * vc, axis=-1, keepdims=True)
        v = vc * lax.rsqrt(var + _EPS) * lng_ref[...] + lnb_ref[...]
        if sample:
            v_ref[rows, :] = v
        gate_b = z[:, o:o + d_b]
        gate_c = z[:, o + d_b:o + 2 * d_b]
        x_in = z[:, o + 2 * d_b:o + 3 * d_b]
        xg = gate_c * x_in
        if sample:
            nb, b0, ctx = sub // chunk_len, r0 // chunk_len, _CONV_W - 1
            seq_ref[...] = jnp.zeros(seq_ref.shape, _F32)
            seq_ref[:, 0, :] = st_ref[b0:b0 + nb, ctx - 1, :]
            before1 = seq_ref[...].reshape(sub, d_b)
            for k in range(ctx):
                seq_ref[:, k, :] = st_ref[b0:b0 + nb, k, :]
            before2 = seq_ref[...].reshape(sub, d_b)
            t8 = lax.broadcasted_iota(jnp.int32, (sub, 1), 0) % chunk_len
            back1 = jnp.where(t8 >= 1, pltpu.roll(xg, 1, axis=0), before1)
            back2 = jnp.where(t8 >= 2, pltpu.roll(xg, 2, axis=0), before2)
            seq_ref[...] = xg.reshape(nb, chunk_len, d_b)
            for k in range(ctx):
                cs_ref[b0:b0 + nb, k, :] = seq_ref[:, chunk_len - ctx + k, :]
        else:
            ext = jnp.concatenate([prev, xg], axis=0)
            back1 = pltpu.roll(ext, 1, axis=0)[_SUBLANES:]
            back2 = pltpu.roll(ext, 2, axis=0)[_SUBLANES:]
            prev = xg[sub - _SUBLANES:]
        conv = back2 * cw[0:1] + back1 * cw[1:2] + xg * cw[2:3]
        return (x, u, v.astype(_BF), (gate_b * conv).astype(_BF)), prev

    def mix_and_output(r0, x, u, vb, y_b):
        y_parts = []
        for hd in range(_NH_A):
            cols = slice(hd * _HD_A, (hd + 1) * _HD_A)
            chunks = []
            for c0 in range(0, sub, _CHUNK):
                mixed = _dot(mix_w[hd], vb[c0:c0 + _CHUNK, cols]) + sb_ref[hd]
                chunks.append((u[c0:c0 + _CHUNK, cols] * mixed).astype(_BF))
            y_parts.append(jnp.concatenate(chunks, axis=0))
        y_parts.append(y_b)
        o_ref[r0:r0 + sub, :] = x + _dot(jnp.concatenate(y_parts, axis=1), wout_ref[...])

    prev = None if sample else carry_ref[...]
    staged = []
    for r0 in range(0, tm, sub):
        vals, prev = project(r0, prev)
        staged.append((r0, vals))
    for r0, vals in staged:
        mix_and_output(r0, *vals)
    if not sample:
        carry_ref[...] = prev
        cs_ref[0] = prev


def _even_body(x_ref, g_ref, win_ref, lng_ref, lnb_ref, swp_ref, sbp_ref, sws_ref, sbs_ref, cw_ref, wout_ref,
               st_ref, o_ref, cs_ref, css_ref, v_ref, carry_ref, seq_ref, *,
               n_prompt_tiles, tiles_per_seq, chunk_s, sub):
    i = pl.program_id(0)
    shared = (x_ref, g_ref, win_ref, lng_ref, lnb_ref)

    @pl.when(i < n_prompt_tiles)
    def _():
        @pl.when((i % tiles_per_seq) == 0)
        def _():
            carry_ref[...] = jnp.zeros_like(carry_ref)

        _even_tile(*shared, swp_ref, sbp_ref, cw_ref, wout_ref, o_ref, chunk_len=_CHUNK, sub=sub,
                   carry_ref=carry_ref, cs_ref=cs_ref)

    @pl.when(i >= n_prompt_tiles)
    def _():
        _even_tile(*shared, sws_ref, sbs_ref, cw_ref, wout_ref, o_ref, chunk_len=chunk_s, sub=sub,
                   cs_ref=css_ref, st_ref=st_ref, seq_ref=seq_ref, v_ref=v_ref)


def _sgu_tiles(sgu_w, sgu_b, chunk_len):
    reps = _CHUNK // chunk_len
    sw = jnp.tile(sgu_w[:, :chunk_len, :chunk_len], (1, reps, reps))
    sb = jnp.tile(sgu_b[:, :chunk_len], (1, reps))[:, :, None]
    return sw, sb


def _even_mixer(xc, g, w_in, ln_g, ln_b, sgu_w, sgu_b, conv_w, w_out, conv_state, *, n_prompt, seq_len, chunk_s):
    t, d = xc.shape
    n_sample = t - n_prompt
    d_in = w_in.shape[1]
    d_a = _NH_A * _HD_A
    d_b = conv_w.shape[1]
    npt = n_prompt // _TM
    n_seq = n_prompt // seq_len
    tiles_per_seq = seq_len // _TM
    row = lambda i: (i, 0)
    srow = lambda i: (jnp.maximum(i - npt, 0), 0)
    swp, sbp = _sgu_tiles(sgu_w, sgu_b, _CHUNK)
    sws, sbs = _sgu_tiles(sgu_w, sgu_b, chunk_s)
    st_block = (_TM // chunk_s,) + conv_state.shape[1:]
    st_map = lambda i: (jnp.maximum(i - npt, 0), 0, 0)
    in_specs = [pl.BlockSpec((_TM, d), row), g[1], _resident((d, d_in)), _resident((1, d_a)),
                _resident((1, d_a)), _resident(swp.shape), _resident(sbp.shape), _resident(sws.shape),
                _resident(sbs.shape), _resident(conv_w.shape), _resident(w_out.shape),
                pl.BlockSpec(st_block, st_map, pipeline_mode=pl.Buffered(1))]
    args = [xc, g[0], w_in, ln_g.reshape(1, d_a), ln_b.reshape(1, d_a), swp, sbp, sws, sbs, conv_w, w_out,
            conv_state]
    out_specs = [pl.BlockSpec((_TM, d), row),
                 pl.BlockSpec((1, _SUBLANES, d_b), lambda i: (jnp.minimum(i // tiles_per_seq, n_seq - 1), 0, 0)),
                 pl.BlockSpec(st_block, st_map), pl.BlockSpec((_TM, d_a), srow)]
    out_shape = [jax.ShapeDtypeStruct((t, d), _F32), jax.ShapeDtypeStruct((n_seq, _SUBLANES, d_b), _F32),
                 jax.ShapeDtypeStruct(conv_state.shape, _F32), jax.ShapeDtypeStruct((n_sample, d_a), _F32)]
    scratch = [pltpu.VMEM((_SUBLANES, d_b), _F32), pltpu.VMEM((_SUB // chunk_s, chunk_s, d_b), _F32)]
    outs, _ = _call(
        functools.partial(_even_body, n_prompt_tiles=npt, tiles_per_seq=tiles_per_seq, chunk_s=chunk_s, sub=_SUB),
        name="even", grid=t // _TM, in_specs=in_specs, args=args, out_specs=out_specs, out_shape=out_shape,
        scratch=scratch)
    return outs


def _window_sums(ext):
    gc = ext.shape[1] // len(_POOL_WINDOWS)
    sums = []
    s = ext
    width = 1
    for w in _POOL_WINDOWS:
        while width < w:
            s = s + pltpu.roll(s, width, axis=0)
            width *= 2
        sums.append(s[:, 0:gc])
        s = s[:, gc:]
    return sums


def _odd_tile(x_ref, g_ref, win_ref, pw_ref, psc_ref, wout_ref, o_ref, ps_ref, *, sub, pos0=None,
              carry_ref=None, ctx_ref=None, ext_ref=None, seq_len=None):
    sample = carry_ref is None
    tm, d = x_ref.shape
    gc = d // len(_POOL_WINDOWS)
    def project(r0, prev):
        rows = slice(r0, r0 + sub)
        x = x_ref[rows, :]
        h = _rms(x, g_ref[...]).astype(_BF)
        p = _dot(h, win_ref[...])
        t_in_tile = r0 + lax.broadcasted_iota(jnp.int32, (sub, 1), 0)
        if sample:
            nb = sub // seq_len
            b0 = r0 // seq_len
            per_seq = _CTX_ROWS + seq_len
            lead = _CTX_ROWS - _POOL_CTX
            ext_ref[:, 0:lead, :] = jnp.zeros((nb, lead, d), _F32)
            for k in range(_POOL_CTX):
                ext_ref[:, lead + k, :] = ctx_ref[k, b0:b0 + nb, :]
            ext_ref[:, _CTX_ROWS:per_seq, :] = p.reshape(nb, seq_len, d)
            ext = ext_ref[...].reshape(nb * per_seq, d)
            pick = lambda s: s.reshape(nb, per_seq, s.shape[1])[:, _CTX_ROWS:, :].reshape(sub, s.shape[1])
            pos = _PAST_LEN + t_in_tile % seq_len
            for k in range(_POOL_CTX):
                ps_ref[k, b0:b0 + nb, :] = ext_ref[:, per_seq - _POOL_CTX + k, :]
        else:
            ext = jnp.concatenate([prev, p], axis=0)
            pick = lambda s: s[_CTX_ROWS:]
            pos = pos0 + t_in_tile
            prev = p[sub - _CTX_ROWS:]
        dlts = []
        for grp, (w, s) in enumerate(zip(_POOL_WINDOWS, _window_sums(ext))):
            cnt = jnp.minimum(pos + 1, w).astype(_F32)
            dlts.append((pick(s) / cnt - p[:, grp * gc:(grp + 1) * gc]).astype(_BF))
        return (x, dlts), prev

    def output(r0, x, dlts):
        ys = []
        for grp, dlt in enumerate(dlts):
            cols = slice(grp * gc, (grp + 1) * gc)
            ys.append((_dot(dlt, pw_ref[grp]) * psc_ref[:, cols]).astype(_BF))
        o_ref[r0:r0 + sub, :] = x + _dot(jnp.concatenate(ys, axis=1), wout_ref[...])

    prev = None if sample else carry_ref[...]
    staged = []
    for r0 in range(0, tm, sub):
        vals, prev = project(r0, prev)
        staged.append((r0, vals))
    for r0, vals in staged:
        output(r0, *vals)
    if not sample:
        carry_ref[...] = prev
        ps_ref[0] = prev


def _odd_body(x_ref, g_ref, win_ref, pw_ref, psc_ref, wout_ref, ctx_ref, o_ref, psp_ref, pss_ref,
              carry_ref, ext_ref, *, n_prompt_tiles, tiles_per_seq, seq_s):
    i = pl.program_id(0)
    shared = (x_ref, g_ref, win_ref, pw_ref, psc_ref, wout_ref, o_ref)

    @pl.when(i < n_prompt_tiles)
    def _():
        @pl.when((i % tiles_per_seq) == 0)
        def _():
            carry_ref[...] = jnp.zeros_like(carry_ref)

        _odd_tile(*shared, psp_ref, sub=_SUB // 2, pos0=(i % tiles_per_seq) * x_ref.shape[0], carry_ref=carry_ref)

    @pl.when(i >= n_prompt_tiles)
    def _():
        _odd_tile(*shared, pss_ref, sub=ext_ref.shape[0] * seq_s, ctx_ref=ctx_ref, ext_ref=ext_ref,
                  seq_len=seq_s)


def _odd_mixer(xc, g, w_in, pool_w, pool_scale, w_out, pool_state, *, n_prompt, seq_len, seq_s):
    t, d = xc.shape
    npt = n_prompt // _TM
    n_seq = n_prompt // seq_len
    tiles_per_seq = seq_len // _TM
    seqs_per_tile = _TM // seq_s
    row = lambda i: (i, 0)
    ctx_spec = lambda **kw: pl.BlockSpec((_POOL_CTX, seqs_per_tile, d), lambda i: (0, jnp.maximum(i - npt, 0), 0), **kw)
    in_specs = [pl.BlockSpec((_TM, d), row), g[1], _resident((d, d)), _resident(pool_w.shape),
                _resident((1, d)), _resident((d, d)), ctx_spec(pipeline_mode=pl.Buffered(1))]
    args = [xc, g[0], w_in, pool_w, pool_scale.reshape(1, d), w_out, pool_state]
    out_specs = [pl.BlockSpec((_TM, d), row),
                 pl.BlockSpec((1, _CTX_ROWS, d), lambda i: (jnp.minimum(i // tiles_per_seq, n_seq - 1), 0, 0)),
                 ctx_spec()]
    out_shape = [jax.ShapeDtypeStruct((t, d), _F32), jax.ShapeDtypeStruct((n_seq, _CTX_ROWS, d), _F32),
                 jax.ShapeDtypeStruct(pool_state.shape, _F32)]
    seqs_per_sub = _SUB // 2 // seq_s
    scratch = [pltpu.VMEM((_CTX_ROWS, d), _F32),
               pltpu.VMEM((seqs_per_sub, _CTX_ROWS + seq_s, d), _F32)]
    outs, _ = _call(
        functools.partial(_odd_body, n_prompt_tiles=npt, tiles_per_seq=tiles_per_seq, seq_s=seq_s),
        name="odd", grid=t // _TM, in_specs=in_specs, args=args, out_specs=out_specs, out_shape=out_shape,
        scratch=scratch)
    return outs


def kernel(x_prompt, x_sample, state_conv, state_pool, norm_g, final_norm_g, ffn_w_gate, ffn_w_up, ffn_w_down,
           e_w_in, e_ln_g, e_ln_b, e_sgu_w, e_sgu_b, e_conv_w, e_w_out, o_w_in, o_pool_w, o_pool_scale, o_w_out):
    batch, seq, d = x_prompt.shape
    dec_batch, dec_seq, _ = x_sample.shape
    n_prompt, n_sample = batch * seq, dec_batch * dec_seq
    d_a = _NH_A * _HD_A
    n_groups, gc = o_pool_w.shape[1:3]
    ffn = functools.partial(_ffn, n_prompt=n_prompt, n_sample=n_sample)
    ffn_casts = lambda layer, k: [_Cast(w, (layer, k)) for w in (ffn_w_gate, ffn_w_up, ffn_w_down)]
    gain = functools.partial(_gain, norm_g)

    w00 = [w[0, 0].astype(_BF) for w in (ffn_w_gate, ffn_w_up, ffn_w_down)]
    even_casts = [_Cast(e_w_in, (0,)), _Cast(e_w_out, (0,))]
    odd_casts = [_Cast(o_w_in, (0,)), _Cast(o_w_out, (0,)), _Cast(o_pool_w.reshape(-1, n_groups * gc, gc), (0,))]
    (xc,), (e_win, e_wout, *w01) = ffn(
        (x_prompt.reshape(n_prompt, d), x_sample.reshape(n_sample, d)), gain(0, 0), *w00,
        casts=even_casts + ffn_casts(0, 1))
    xc, conv_p, conv_s, v_s = _even_mixer(xc, gain(0, 1), e_win, e_ln_g[0], e_ln_b[0], e_sgu_w[0], e_sgu_b[0],
                                         e_conv_w[0], e_wout, state_conv[0], n_prompt=n_prompt, seq_len=seq,
                                         chunk_s=dec_seq)
    (xc,), (o_win, o_wout, o_pw, *w10) = ffn((xc,), gain(0, 2), *w01, casts=odd_casts + ffn_casts(1, 0))

    (xc,), w11 = ffn((xc,), gain(1, 0), *w10, casts=ffn_casts(1, 1))
    xc, pool_p, pool_s = _odd_mixer(xc, gain(1, 1), o_win, o_pw.reshape(n_groups, gc, gc), o_pool_scale[0], o_wout,
                                    jnp.transpose(state_pool[0], (1, 0, 2)), n_prompt=n_prompt, seq_len=seq,
                                    seq_s=dec_seq)
    (y_p, y_s), _ = ffn((xc,), gain(1, 2), *w11, final_g=final_norm_g)

    keep = _CONV_W - 1
    conv_prompt = conv_p[:, _SUBLANES - keep:][None]
    conv_sample = conv_s[None]
    chunk_v_sample = v_s.reshape(1, dec_batch, dec_seq, d_a)
    pool_prompt = pool_p[:, pool_p.shape[1] - _POOL_CTX:][None]
    pool_sample = jnp.transpose(pool_s, (1, 0, 2))[None]
    return (y_p.reshape(batch, seq, d), y_s.reshape(dec_batch, dec_seq, d), conv_prompt, conv_sample,
            chunk_v_sample, pool_prompt, pool_sample)
```

```python
import collections
import functools

import jax
import jax.numpy as jnp
from jax import lax
from jax.experimental import pallas as pl
from jax.experimental.pallas import tpu as pltpu

_CHUNK = 128
_NH_A = 4
_HD_A = 128
_CONV_W = 3
_POOL_WINDOWS = (2, 4, 8, 16)
_POOL_CTX = max(_POOL_WINDOWS) - 1
_PAST_LEN = 16384
_EPS = 1e-6

_SUBLANES = 8
_BF16_ROWS = 16
_VMEM_LIMIT = 56 * 1024 * 1024

_TM = 1024
_SUB = 512
_FFN_CHUNKS = ((0, 1536), (1536, 1280))
_CTX_ROWS = 2 * _SUBLANES

_BF = jnp.bfloat16
_F32 = jnp.float32


def _rms(x, g):
    return x * lax.rsqrt(jnp.mean(x * x, axis=-1, keepdims=True) + _EPS) * g


def _dot(a, b):
    return jnp.dot(a, b, preferred_element_type=_F32)


def _resident(shape):
    nd = len(shape)
    return pl.BlockSpec(shape, lambda i: (0,) * nd, pipeline_mode=pl.Buffered(1))


def _gain(norm_g, layer, j):
    n, m, d = norm_g.shape
    k = layer * m + j
    spec = pl.BlockSpec((None, 1, d), lambda i: (k, 0, 0), pipeline_mode=pl.Buffered(1))
    return norm_g.reshape(n * m, 1, d), spec


_Cast = collections.namedtuple("_Cast", "array lead")


def _cast_specs(cast, grid):
    lead = tuple(cast.lead)
    rows, cols = cast.array.shape[len(lead):]
    steps = max(s for s in (1, 2, 4, 8, 16, 32) if s <= grid and rows % (s * _BF16_ROWS) == 0)
    blk = rows // steps
    step = lambda i: jnp.minimum(i, steps - 1)
    in_spec = pl.BlockSpec((None,) * len(lead) + (blk, cols), lambda i: lead + (step(i), 0))
    out_spec = pl.BlockSpec((blk, cols), lambda i: (step(i), 0))
    return in_spec, out_spec, jax.ShapeDtypeStruct((rows, cols), _BF)


def _call(body, *, name, grid, in_specs, args, out_specs, out_shape, scratch=(), casts=()):
    n_in, n_out, n_cast = len(in_specs), len(out_specs), len(casts)
    specs = [_cast_specs(c, grid) for c in casts]

    def wrapped(*refs):
        ins, refs = refs[:n_in], refs[n_in:]
        cast_in, refs = refs[:n_cast], refs[n_cast:]
        outs, refs = refs[:n_out], refs[n_out:]
        cast_out, scratch_refs = refs[:n_cast], refs[n_cast:]
        for src, dst in zip(cast_in, cast_out):
            dst[...] = src[...].astype(_BF)
        body(*ins, *outs, *scratch_refs)

    res = pl.pallas_call(
        wrapped,
        grid=(grid,),
        in_specs=list(in_specs) + [s[0] for s in specs],
        out_specs=list(out_specs) + [s[1] for s in specs],
        out_shape=list(out_shape) + [s[2] for s in specs],
        scratch_shapes=list(scratch),
        compiler_params=pltpu.CompilerParams(dimension_semantics=("arbitrary",), vmem_limit_bytes=_VMEM_LIMIT),
        name=name,
    )(*args, *[c.array for c in casts])
    return res[:n_out], res[n_out:]


def _ffn_body(*refs, split_in, final, n_prompt_tiles):
    refs = list(refs)
    i = pl.program_id(0)
    x_refs = [refs.pop(0) for _ in range(2 if split_in else 1)]
    g_ref, wg_ref, wu_ref, wd_ref = refs[:4]
    refs = refs[4:]
    is_prompt = i < n_prompt_tiles
    for r0 in range(0, _TM, _SUB):
        rows = slice(r0, r0 + _SUB)
        if split_in:
            x = jnp.where(is_prompt, x_refs[0][rows, :], x_refs[1][rows, :])
        else:
            x = x_refs[0][rows, :]
        h = _rms(x, g_ref[...]).astype(_BF)
        acc = None
        for c0, cn in _FFN_CHUNKS:
            a = _dot(h, wg_ref[:, c0:c0 + cn])
            b = _dot(h, wu_ref[:, c0:c0 + cn])
            act = (a * jax.nn.sigmoid(a) * b).astype(_BF)
            d = _dot(act, wd_ref[c0:c0 + cn, :])
            acc = d if acc is None else acc + d
        y = x + 0.5 * acc
        if final:
            fg_ref, yp_ref, ys_ref = refs
            y = _rms(y, fg_ref[...])
            to_prompt = jnp.broadcast_to(is_prompt, y.shape)
            pltpu.store(yp_ref.at[rows, :], y, mask=to_prompt)
            pltpu.store(ys_ref.at[rows, :], y, mask=jnp.logical_not(to_prompt))
        else:
            (o_ref,) = refs
            o_ref[rows, :] = y


def _ffn(xs, g, wg, wu, wd, *, n_prompt, n_sample, final_g=None, casts=()):
    d, dff = wg.shape
    npt, nst = n_prompt // _TM, n_sample // _TM
    split_in = len(xs) == 2
    final = final_g is not None
    row = lambda i: (i, 0)
    prow = lambda i: (jnp.minimum(i, npt - 1), 0)
    srow = lambda i: (jnp.maximum(i - npt, 0), 0)
    tile = (_TM, d)
    if split_in:
        in_specs = [pl.BlockSpec(tile, prow), pl.BlockSpec(tile, srow, pipeline_mode=pl.Buffered(1))]
    else:
        in_specs = [pl.BlockSpec(tile, row)]
    in_specs += [g[1], _resident((d, dff)), _resident((d, dff)), _resident((dff, d))]
    args = list(xs) + [g[0], wg, wu, wd]
    if final:
        in_specs.append(_resident((1, d)))
        args.append(final_g.reshape(1, d))
        out_specs = [pl.BlockSpec(tile, prow), pl.BlockSpec(tile, srow)]
        out_shape = [jax.ShapeDtypeStruct((n_prompt, d), _F32), jax.ShapeDtypeStruct((n_sample, d), _F32)]
    else:
        out_specs = [pl.BlockSpec(tile, row)]
        out_shape = [jax.ShapeDtypeStruct((n_prompt + n_sample, d), _F32)]
    return _call(
        functools.partial(_ffn_body, split_in=split_in, final=final, n_prompt_tiles=npt),
        name="ffn_final" if final else ("ffn_first" if split_in else "ffn"),
        grid=npt + nst, in_specs=in_specs, args=args, out_specs=out_specs, out_shape=out_shape, casts=casts)


def _gelu(x):
    return 0.5 * x * (1.0 + lax.erf(x * (2.0 ** -0.5)))


def _even_tile(x_ref, g_ref, win_ref, lng_ref, lnb_ref, sw_ref, sb_ref, cw_ref, wout_ref, o_ref, *,
               chunk_len, sub, carry_ref=None, cs_ref=None, st_ref=None, seq_ref=None, v_ref=None):
    sample = carry_ref is None
    tm = x_ref.shape[0]
    d_a = _NH_A * _HD_A
    d_b = cw_ref.shape[1]
    o = 2 * d_a
    tt = lax.broadcasted_iota(jnp.int32, (_CHUNK, _CHUNK), 0)
    ss = lax.broadcasted_iota(jnp.int32, (_CHUNK, _CHUNK), 1)
    keep = (ss <= tt) & ((tt // chunk_len) == (ss // chunk_len))
    mix_w, mix_b = [], []
    for hd in range(_NH_A):
        corner = jnp.where(ss[:chunk_len] < chunk_len, sw_ref[hd, 0:chunk_len, :], 0.0)
        width = chunk_len
        while width < _CHUNK:
            corner = corner + pltpu.roll(corner, width, axis=1)
            width *= 2
        tiled = jnp.concatenate([corner] * (_CHUNK // chunk_len), axis=0)
        mix_w.append(jnp.where(keep, tiled, 0.0).astype(_BF))
        picked = jnp.where(ss == tt % chunk_len, sb_ref[hd:hd + 1, :], 0.0)
        mix_b.append(jnp.sum(picked, axis=1, keepdims=True))
    cw = cw_ref[...]

    def project(r0, prev):
        rows = slice(r0, r0 + sub)
        x = x_ref[rows, :]
        h = _rms(x, g_ref[...]).astype(_BF)
        z = _dot(h, win_ref[...])
        u = _gelu(z[:, 0:d_a])
        vv = _gelu(z[:, d_a:2 * d_a])
        mu = jnp.mean(vv, axis=-1, keepdims=True)
        vc = vv - mu
        var = jnp.mean(vc * vc, axis=-1, keepdims=True)
        v = vc * lax.rsqrt(var + _EPS) * lng_ref[...] + lnb_ref[...]
        if sample:
            v_ref[rows, :] = v
        gate_b = z[:, o:o + d_b]
        gate_c = z[:, o + d_b:o + 2 * d_b]
        x_in = z[:, o + 2 * d_b:o + 3 * d_b]
        xg = gate_c * x_in
        if sample:
            nb, b0, ctx = sub // chunk_len, r0 // chunk_len, _CONV_W - 1
            seq_ref[...] = jnp.zeros(seq_ref.shape, _F32)
            seq_ref[:, 0, :] = st_ref[b0:b0 + nb, ctx - 1, :]
            before1 = seq_ref[...].reshape(sub, d_b)
            for k in range(ctx):
                seq_ref[:, k, :] = st_ref[b0:b0 + nb, k, :]
            before2 = seq_ref[...].reshape(sub, d_b)
            t8 = lax.broadcasted_iota(jnp.int32, (sub, 1), 0) % chunk_len
            back1 = jnp.where(t8 >= 1, pltpu.roll(xg, 1, axis=0), before1)
            back2 = jnp.where(t8 >= 2, pltpu.roll(xg, 2, axis=0), before2)
            seq_ref[...] = xg.reshape(nb, chunk_len, d_b)
            for k in range(ctx):
                cs_ref[b0:b0 + nb, k, :] = seq_ref[:, chunk_len - ctx + k, :]
        else:
            ext = jnp.concatenate([prev, xg], axis=0)
            back1 = pltpu.roll(ext, 1, axis=0)[_SUBLANES:]
            back2 = pltpu.roll(ext, 2, axis=0)[_SUBLANES:]
            prev = xg[sub - _SUBLANES:]
        conv = back2 * cw[0:1] + back1 * cw[1:2] + xg * cw[2:3]
        return (x, u, v.astype(_BF), (gate_b * conv).astype(_BF)), prev

    def mix_and_output(r0, x, u, vb, y_b):
        y_parts = []
        for hd in range(_NH_A):
            cols = slice(hd * _HD_A, (hd + 1) * _HD_A)
            chunks = []
            for c0 in range(0, sub, _CHUNK):
                mixed = _dot(mix_w[hd], vb[c0:c0 + _CHUNK, cols]) + mix_b[hd]
                chunks.append((u[c0:c0 + _CHUNK, cols] * mixed).astype(_BF))
            y_parts.append(jnp.concatenate(chunks, axis=0))
        y_parts.append(y_b)
        o_ref[r0:r0 + sub, :] = x + _dot(jnp.concatenate(y_parts, axis=1), wout_ref[...])

    prev = None if sample else carry_ref[...]
    staged = []
    for r0 in range(0, tm, sub):
        vals, prev = project(r0, prev)
        staged.append((r0, vals))
    for r0, vals in staged:
        mix_and_output(r0, *vals)
    if not sample:
        carry_ref[...] = prev
        cs_ref[0] = prev


def _even_body(x_ref, g_ref, win_ref, lng_ref, lnb_ref, sw_ref, sb_ref, cw_ref, wout_ref,
               st_ref, o_ref, cs_ref, css_ref, v_ref, carry_ref, seq_ref, *,
               n_prompt_tiles, tiles_per_seq, chunk_s, sub):
    i = pl.program_id(0)
    shared = (x_ref, g_ref, win_ref, lng_ref, lnb_ref, sw_ref, sb_ref, cw_ref, wout_ref, o_ref)

    @pl.when(i < n_prompt_tiles)
    def _():
        @pl.when((i % tiles_per_seq) == 0)
        def _():
            carry_ref[...] = jnp.zeros_like(carry_ref)

        _even_tile(*shared, chunk_len=_CHUNK, sub=sub, carry_ref=carry_ref, cs_ref=cs_ref)

    @pl.when(i >= n_prompt_tiles)
    def _():
        _even_tile(*shared, chunk_len=chunk_s, sub=sub, cs_ref=css_ref, st_ref=st_ref, seq_ref=seq_ref, v_ref=v_ref)


def _even_mixer(xc, g, w_in, ln_g, ln_b, sgu_w, sgu_b, conv_w, w_out, conv_state, *, n_prompt, seq_len, chunk_s):
    t, d = xc.shape
    n_sample = t - n_prompt
    d_in = w_in.shape[1]
    d_a = _NH_A * _HD_A
    d_b = conv_w.shape[1]
    npt = n_prompt // _TM
    n_seq = n_prompt // seq_len
    tiles_per_seq = seq_len // _TM
    row = lambda i: (i, 0)
    srow = lambda i: (jnp.maximum(i - npt, 0), 0)
    st_block = (_TM // chunk_s,) + conv_state.shape[1:]
    st_map = lambda i: (jnp.maximum(i - npt, 0), 0, 0)
    in_specs = [pl.BlockSpec((_TM, d), row), g[1], _resident((d, d_in)), _resident((1, d_a)),
                _resident((1, d_a)), _resident(sgu_w.shape), _resident(sgu_b.shape), _resident(conv_w.shape),
                _resident(w_out.shape), pl.BlockSpec(st_block, st_map, pipeline_mode=pl.Buffered(1))]
    args = [xc, g[0], w_in, ln_g.reshape(1, d_a), ln_b.reshape(1, d_a), sgu_w, sgu_b, conv_w, w_out, conv_state]
    out_specs = [pl.BlockSpec((_TM, d), row),
                 pl.BlockSpec((1, _SUBLANES, d_b), lambda i: (jnp.minimum(i // tiles_per_seq, n_seq - 1), 0, 0)),
                 pl.BlockSpec(st_block, st_map), pl.BlockSpec((_TM, d_a), srow)]
    out_shape = [jax.ShapeDtypeStruct((t, d), _F32), jax.ShapeDtypeStruct((n_seq, _SUBLANES, d_b), _F32),
                 jax.ShapeDtypeStruct(conv_state.shape, _F32), jax.ShapeDtypeStruct((n_sample, d_a), _F32)]
    scratch = [pltpu.VMEM((_SUBLANES, d_b), _F32), pltpu.VMEM((_SUB // chunk_s, chunk_s, d_b), _F32)]
    outs, _ = _call(
        functools.partial(_even_body, n_prompt_tiles=npt, tiles_per_seq=tiles_per_seq, chunk_s=chunk_s, sub=_SUB),
        name="even", grid=t // _TM, in_specs=in_specs, args=args, out_specs=out_specs, out_shape=out_shape,
        scratch=scratch)
    return outs


def _window_sums(ext):
    gc = ext.shape[1] // len(_POOL_WINDOWS)
    sums = []
    s = ext
    width = 1
    for w in _POOL_WINDOWS:
        while width < w:
            s = s + pltpu.roll(s, width, axis=0)
            width *= 2
        sums.append(s[:, 0:gc])
        s = s[:, gc:]
    return sums


def _odd_tile(x_ref, g_ref, win_ref, pw_ref, psc_ref, wout_ref, o_ref, ps_ref, *, sub, pos0=None,
              carry_ref=None, ctx_ref=None, ext_ref=None, seq_len=None):
    sample = carry_ref is None
    tm, d = x_ref.shape
    gc = d // len(_POOL_WINDOWS)
    def project(r0, prev):
        rows = slice(r0, r0 + sub)
        x = x_ref[rows, :]
        h = _rms(x, g_ref[...]).astype(_BF)
        p = _dot(h, win_ref[...])
        t_in_tile = r0 + lax.broadcasted_iota(jnp.int32, (sub, 1), 0)
        if sample:
            nb = sub // seq_len
            b0 = r0 // seq_len
            per_seq = _CTX_ROWS + seq_len
            lead = _CTX_ROWS - _POOL_CTX
            ext_ref[:, 0:lead, :] = jnp.zeros((nb, lead, d), _F32)
            for k in range(_POOL_CTX):
                ext_ref[:, lead + k, :] = ctx_ref[k, b0:b0 + nb, :]
            ext_ref[:, _CTX_ROWS:per_seq, :] = p.reshape(nb, seq_len, d)
            ext = ext_ref[...].reshape(nb * per_seq, d)
            pick = lambda s: s.reshape(nb, per_seq, s.shape[1])[:, _CTX_ROWS:, :].reshape(sub, s.shape[1])
            pos = _PAST_LEN + t_in_tile % seq_len
            for k in range(_POOL_CTX):
                ps_ref[k, b0:b0 + nb, :] = ext_ref[:, per_seq - _POOL_CTX + k, :]
        else:
            ext = jnp.concatenate([prev, p], axis=0)
            pick = lambda s: s[_CTX_ROWS:]
            pos = pos0 + t_in_tile
            prev = p[sub - _CTX_ROWS:]
        dlts = []
        for grp, (w, s) in enumerate(zip(_POOL_WINDOWS, _window_sums(ext))):
            cnt = jnp.minimum(pos + 1, w).astype(_F32)
            dlts.append((pick(s) / cnt - p[:, grp * gc:(grp + 1) * gc]).astype(_BF))
        return (x, dlts), prev

    def output(r0, x, dlts):
        ys = []
        for grp, dlt in enumerate(dlts):
            cols = slice(grp * gc, (grp + 1) * gc)
            ys.append((_dot(dlt, pw_ref[grp]) * psc_ref[:, cols]).astype(_BF))
        o_ref[r0:r0 + sub, :] = x + _dot(jnp.concatenate(ys, axis=1), wout_ref[...])

    prev = None if sample else carry_ref[...]
    staged = []
    for r0 in range(0, tm, sub):
        vals, prev = project(r0, prev)
        staged.append((r0, vals))
    for r0, vals in staged:
        output(r0, *vals)
    if not sample:
        carry_ref[...] = prev
        ps_ref[0] = prev


def _odd_body(x_ref, g_ref, win_ref, pw_ref, psc_ref, wout_ref, ctx_ref, o_ref, psp_ref, pss_ref,
              carry_ref, ext_ref, *, n_prompt_tiles, tiles_per_seq, seq_s):
    i = pl.program_id(0)
    shared = (x_ref, g_ref, win_ref, pw_ref, psc_ref, wout_ref, o_ref)

    @pl.when(i < n_prompt_tiles)
    def _():
        @pl.when((i % tiles_per_seq) == 0)
        def _():
            carry_ref[...] = jnp.zeros_like(carry_ref)

        _odd_tile(*shared, psp_ref, sub=_SUB // 2, pos0=(i % tiles_per_seq) * x_ref.shape[0], carry_ref=carry_ref)

    @pl.when(i >= n_prompt_tiles)
    def _():
        _odd_tile(*shared, pss_ref, sub=ext_ref.shape[0] * seq_s, ctx_ref=ctx_ref, ext_ref=ext_ref,
                  seq_len=seq_s)


def _odd_mixer(xc, g, w_in, pool_w, pool_scale, w_out, pool_state, *, n_prompt, seq_len, seq_s):
    t, d = xc.shape
    npt = n_prompt // _TM
    n_seq = n_prompt // seq_len
    tiles_per_seq = seq_len // _TM
    seqs_per_tile = _TM // seq_s
    row = lambda i: (i, 0)
    ctx_spec = lambda **kw: pl.BlockSpec((_POOL_CTX, seqs_per_tile, d), lambda i: (0, jnp.maximum(i - npt, 0), 0), **kw)
    in_specs = [pl.BlockSpec((_TM, d), row), g[1], _resident((d, d)), _resident(pool_w.shape),
                _resident((1, d)), _resident((d, d)), ctx_spec(pipeline_mode=pl.Buffered(1))]
    args = [xc, g[0], w_in, pool_w, pool_scale.reshape(1, d), w_out, pool_state]
    out_specs = [pl.BlockSpec((_TM, d), row),
                 pl.BlockSpec((1, _CTX_ROWS, d), lambda i: (jnp.minimum(i // tiles_per_seq, n_seq - 1), 0, 0)),
                 ctx_spec()]
    out_shape = [jax.ShapeDtypeStruct((t, d), _F32), jax.ShapeDtypeStruct((n_seq, _CTX_ROWS, d), _F32),
                 jax.ShapeDtypeStruct(pool_state.shape, _F32)]
    seqs_per_sub = _SUB // 2 // seq_s
    scratch = [pltpu.VMEM((_CTX_ROWS, d), _F32),
               pltpu.VMEM((seqs_per_sub, _CTX_ROWS + seq_s, d), _F32)]
    outs, _ = _call(
        functools.partial(_odd_body, n_prompt_tiles=npt, tiles_per_seq=tiles_per_seq, seq_s=seq_s),
        name="odd", grid=t // _TM, in_specs=in_specs, args=args, out_specs=out_specs, out_shape=out_shape,
        scratch=scratch)
    return outs


def kernel(x_prompt, x_sample, state_conv, state_pool, norm_g, final_norm_g, ffn_w_gate, ffn_w_up, ffn_w_down,
           e_w_in, e_ln_g, e_ln_b, e_sgu_w, e_sgu_b, e_conv_w, e_w_out, o_w_in, o_pool_w, o_pool_scale, o_w_out):
    batch, seq, d = x_prompt.shape
    dec_batch, dec_seq, _ = x_sample.shape
    n_prompt, n_sample = batch * seq, dec_batch * dec_seq
    d_a = _NH_A * _HD_A
    n_groups, gc = o_pool_w.shape[1:3]
    ffn = functools.partial(_ffn, n_prompt=n_prompt, n_sample=n_sample)
    ffn_casts = lambda layer, k: [_Cast(w, (layer, k)) for w in (ffn_w_gate, ffn_w_up, ffn_w_down)]
    gain = functools.partial(_gain, norm_g)

    w00 = [w[0, 0].astype(_BF) for w in (ffn_w_gate, ffn_w_up, ffn_w_down)]
    even_casts = [_Cast(e_w_in, (0,)), _Cast(e_w_out, (0,))]
    odd_casts = [_Cast(o_w_in, (0,)), _Cast(o_w_out, (0,)), _Cast(o_pool_w.reshape(-1, n_groups * gc, gc), (0,))]
    (xc,), (e_win, e_wout, *w01) = ffn(
        (x_prompt.reshape(n_prompt, d), x_sample.reshape(n_sample, d)), gain(0, 0), *w00,
        casts=even_casts + ffn_casts(0, 1))
    xc, conv_p, conv_s, v_s = _even_mixer(xc, gain(0, 1), e_win, e_ln_g[0], e_ln_b[0], e_sgu_w[0], e_sgu_b[0],
                                         e_conv_w[0], e_wout, state_conv[0], n_prompt=n_prompt, seq_len=seq,
                                         chunk_s=dec_seq)
    (xc,), (o_win, o_wout, o_pw, *w10) = ffn((xc,), gain(0, 2), *w01, casts=odd_casts + ffn_casts(1, 0))

    (xc,), w11 = ffn((xc,), gain(1, 0), *w10, casts=ffn_casts(1, 1))
    xc, pool_p, pool_s = _odd_mixer(xc, gain(1, 1), o_win, o_pw.reshape(n_groups, gc, gc), o_pool_scale[0], o_wout,
                                    jnp.transpose(state_pool[0], (1, 0, 2)), n_prompt=n_prompt, seq_len=seq,
                                    seq_s=dec_seq)
    (y_p, y_s), _ = ffn((xc,), gain(1, 2), *w11, final_g=final_norm_g)

    keep = _CONV_W - 1
    conv_prompt = conv_p[:, _SUBLANES - keep:][None]
    conv_sample = conv_s[None]
    chunk_v_sample = v_s.reshape(1, dec_batch, dec_seq, d_a)
    pool_prompt = pool_p[:, pool_p.shape[1] - _POOL_CTX:][None]
    pool_sample = jnp.transpose(pool_s, (1, 0, 2))[None]
    return (y_p.reshape(batch, seq, d), y_s.reshape(dec_batch, dec_seq, d), conv_prompt, conv_sample,
            chunk_v_sample, pool_prompt, pool_sample)
```

```python
import collections
import functools

import jax
import jax.numpy as jnp
from jax import lax
from jax.experimental import pallas as pl
from jax.experimental.pallas import tpu as pltpu

_CHUNK = 128
_NH_A = 4
_HD_A = 128
_CONV_W = 3
_POOL_WINDOWS = (2, 4, 8, 16)
_POOL_CTX = max(_POOL_WINDOWS) - 1
_PAST_LEN = 16384
_EPS = 1e-6

_SUBLANES = 8
_BF16_ROWS = 16
_VMEM_LIMIT = 56 * 1024 * 1024

_TM = 1024
_SUB = 512
_FFN_CHUNKS = ((0, 1536), (1536, 1280))
_CAST_CHUNKS = 16
_CAST_SLOTS = 4
_CTX_ROWS = 2 * _SUBLANES

_BF = jnp.bfloat16
_F32 = jnp.float32


def _rms(x, g):
    return x * lax.rsqrt(jnp.mean(x * x, axis=-1, keepdims=True) + _EPS) * g


def _dot(a, b):
    return jnp.dot(a, b, preferred_element_type=_F32)


def _resident(shape):
    nd = len(shape)
    return pl.BlockSpec(shape, lambda i: (0,) * nd, pipeline_mode=pl.Buffered(1))


def _gain(norm_g, layer, j):
    n, m, d = norm_g.shape
    k = layer * m + j
    spec = pl.BlockSpec((None, 1, d), lambda i: (k, 0, 0), pipeline_mode=pl.Buffered(1))
    return norm_g.reshape(n * m, 1, d), spec


_Cast = collections.namedtuple("_Cast", "array lead")


def _cast_specs(cast, grid):
    lead = tuple(cast.lead)
    rows, cols = cast.array.shape[len(lead):]
    steps = max(s for s in (1, 2, 4, 8, 16, 32) if s <= grid and rows % (s * _BF16_ROWS) == 0)
    blk = rows // steps
    step = lambda i: jnp.minimum(i, steps - 1)
    in_spec = pl.BlockSpec((None,) * len(lead) + (blk, cols), lambda i: lead + (step(i), 0))
    out_spec = pl.BlockSpec((blk, cols), lambda i: (step(i), 0))
    return in_spec, out_spec, jax.ShapeDtypeStruct((rows, cols), _BF)


def _call(body, *, name, grid, in_specs, args, out_specs, out_shape, scratch=(), casts=()):
    n_in, n_out, n_cast = len(in_specs), len(out_specs), len(casts)
    specs = [_cast_specs(c, grid) for c in casts]

    def wrapped(*refs):
        ins, refs = refs[:n_in], refs[n_in:]
        cast_in, refs = refs[:n_cast], refs[n_cast:]
        outs, refs = refs[:n_out], refs[n_out:]
        cast_out, scratch_refs = refs[:n_cast], refs[n_cast:]
        for src, dst in zip(cast_in, cast_out):
            dst[...] = src[...].astype(_BF)
        body(*ins, *outs, *scratch_refs)

    res = pl.pallas_call(
        wrapped,
        grid=(grid,),
        in_specs=list(in_specs) + [s[0] for s in specs],
        out_specs=list(out_specs) + [s[1] for s in specs],
        out_shape=list(out_shape) + [s[2] for s in specs],
        scratch_shapes=list(scratch),
        compiler_params=pltpu.CompilerParams(dimension_semantics=("arbitrary",), vmem_limit_bytes=_VMEM_LIMIT),
        name=name,
    )(*args, *[c.array for c in casts])
    return res[:n_out], res[n_out:]


def _load_cast(srcs, dsts, stage, sem):
    slots, rows = stage.shape[:2]
    chunks = [(src, dst, c) for src, dst in zip(srcs, dsts) for c in range(dst.shape[0] // rows)]
    copy = lambda n: pltpu.make_async_copy(
        chunks[n][0].at[pl.ds(chunks[n][2] * rows, rows), :], stage.at[n % slots], sem.at[n % slots])
    for n in range(min(slots - 1, len(chunks))):
        copy(n).start()
    for n, (_, dst, c) in enumerate(chunks):
        if n + slots - 1 < len(chunks):
            copy(n + slots - 1).start()
        copy(n).wait()
        dst[c * rows:(c + 1) * rows, :] = stage[n % slots].astype(_BF)


def _ffn_body(*refs, split_in, final, n_prompt_tiles, f32_lead):
    refs = list(refs)
    i = pl.program_id(0)
    x_refs = [refs.pop(0) for _ in range(2 if split_in else 1)]
    g_ref, wg_ref, wu_ref, wd_ref = refs[:4]
    refs = refs[4:]
    if f32_lead is not None:
        *refs, wg_bf, wu_bf, stage, sem = refs

        @pl.when(i == 0)
        def _():
            _load_cast([wg_ref.at[f32_lead], wu_ref.at[f32_lead]], [wg_bf, wu_bf], stage, sem)

        wg_ref, wu_ref = wg_bf, wu_bf
    is_prompt = i < n_prompt_tiles
    for r0 in range(0, _TM, _SUB):
        rows = slice(r0, r0 + _SUB)
        if split_in:
            x = jnp.where(is_prompt, x_refs[0][rows, :], x_refs[1][rows, :])
        else:
            x = x_refs[0][rows, :]
        h = _rms(x, g_ref[...]).astype(_BF)
        acc = None
        for c0, cn in _FFN_CHUNKS:
            a = _dot(h, wg_ref[:, c0:c0 + cn])
            b = _dot(h, wu_ref[:, c0:c0 + cn])
            act = (a * jax.nn.sigmoid(a) * b).astype(_BF)
            d = _dot(act, wd_ref[c0:c0 + cn, :])
            acc = d if acc is None else acc + d
        y = x + 0.5 * acc
        if final:
            fg_ref, yp_ref, ys_ref = refs
            y = _rms(y, fg_ref[...])
            to_prompt = jnp.broadcast_to(is_prompt, y.shape)
            pltpu.store(yp_ref.at[rows, :], y, mask=to_prompt)
            pltpu.store(ys_ref.at[rows, :], y, mask=jnp.logical_not(to_prompt))
        else:
            (o_ref,) = refs
            o_ref[rows, :] = y


def _ffn(xs, g, wg, wu, wd, *, n_prompt, n_sample, final_g=None, casts=(), f32_lead=None):
    dff, d = wd.shape
    npt, nst = n_prompt // _TM, n_sample // _TM
    split_in = len(xs) == 2
    final = final_g is not None
    row = lambda i: (i, 0)
    prow = lambda i: (jnp.minimum(i, npt - 1), 0)
    srow = lambda i: (jnp.maximum(i - npt, 0), 0)
    tile = (_TM, d)
    if split_in:
        in_specs = [pl.BlockSpec(tile, prow), pl.BlockSpec(tile, srow, pipeline_mode=pl.Buffered(1))]
    else:
        in_specs = [pl.BlockSpec(tile, row)]
    scratch = []
    if f32_lead is None:
        in_specs += [g[1], _resident((d, dff)), _resident((d, dff)), _resident((dff, d))]
    else:
        in_specs += [g[1], pl.BlockSpec(memory_space=pl.ANY), pl.BlockSpec(memory_space=pl.ANY), _resident((dff, d))]
        scratch = [pltpu.VMEM((d, dff), _BF), pltpu.VMEM((d, dff), _BF),
                   pltpu.VMEM((_CAST_SLOTS, d // _CAST_CHUNKS, dff), _F32), pltpu.SemaphoreType.DMA((_CAST_SLOTS,))]
    args = list(xs) + [g[0], wg, wu, wd]
    if final:
        in_specs.append(_resident((1, d)))
        args.append(final_g.reshape(1, d))
        out_specs = [pl.BlockSpec(tile, prow), pl.BlockSpec(tile, srow)]
        out_shape = [jax.ShapeDtypeStruct((n_prompt, d), _F32), jax.ShapeDtypeStruct((n_sample, d), _F32)]
    else:
        out_specs = [pl.BlockSpec(tile, row)]
        out_shape = [jax.ShapeDtypeStruct((n_prompt + n_sample, d), _F32)]
    return _call(
        functools.partial(_ffn_body, split_in=split_in, final=final, n_prompt_tiles=npt, f32_lead=f32_lead),
        name="ffn_final" if final else ("ffn_first" if split_in else "ffn"),
        grid=npt + nst, in_specs=in_specs, args=args, out_specs=out_specs, out_shape=out_shape, scratch=scratch,
        casts=casts)


def _gelu(x):
    return 0.5 * x * (1.0 + lax.erf(x * (2.0 ** -0.5)))


def _even_tile(x_ref, g_ref, win_ref, lng_ref, lnb_ref, sw_ref, sb_ref, cw_ref, wout_ref, o_ref, *,
               chunk_len, sub, carry_ref=None, cs_ref=None, st_ref=None, seq_ref=None, v_ref=None):
    sample = carry_ref is None
    tm = x_ref.shape[0]
    d_a = _NH_A * _HD_A
    d_b = cw_ref.shape[1]
    o = 2 * d_a
    tt = lax.broadcasted_iota(jnp.int32, (_CHUNK, _CHUNK), 0)
    ss = lax.broadcasted_iota(jnp.int32, (_CHUNK, _CHUNK), 1)
    keep = (ss <= tt) & ((tt // chunk_len) == (ss // chunk_len))
    mix_w, mix_b = [], []
    for hd in range(_NH_A):
        corner = jnp.where(ss[:chunk_len] < chunk_len, sw_ref[hd, 0:chunk_len, :], 0.0)
        width = chunk_len
        while width < _CHUNK:
            corner = corner + pltpu.roll(corner, width, axis=1)
            width *= 2
        tiled = jnp.concatenate([corner] * (_CHUNK // chunk_len), axis=0)
        mix_w.append(jnp.where(keep, tiled, 0.0).astype(_BF))
        picked = jnp.where(ss == tt % chunk_len, sb_ref[hd:hd + 1, :], 0.0)
        mix_b.append(jnp.sum(picked, axis=1, keepdims=True))
    cw = cw_ref[...]

    def project(r0, prev):
        rows = slice(r0, r0 + sub)
        x = x_ref[rows, :]
        h = _rms(x, g_ref[...]).astype(_BF)
        z = _dot(h, win_ref[...])
        u = _gelu(z[:, 0:d_a])
        vv = _gelu(z[:, d_a:2 * d_a])
        mu = jnp.mean(vv, axis=-1, keepdims=True)
        vc = vv - mu
        var = jnp.mean(vc * vc, axis=-1, keepdims=True)
        v = vc * lax.rsqrt(var + _EPS) * lng_ref[...] + lnb_ref[...]
        if sample:
            v_ref[rows, :] = v
        gate_b = z[:, o:o + d_b]
        gate_c = z[:, o + d_b:o + 2 * d_b]
        x_in = z[:, o + 2 * d_b:o + 3 * d_b]
        xg = gate_c * x_in
        if sample:
            nb, b0, ctx = sub // chunk_len, r0 // chunk_len, _CONV_W - 1
            seq_ref[...] = jnp.zeros(seq_ref.shape, _F32)
            seq_ref[:, 0, :] = st_ref[b0:b0 + nb, ctx - 1, :]
            before1 = seq_ref[...].reshape(sub, d_b)
            for k in range(ctx):
                seq_ref[:, k, :] = st_ref[b0:b0 + nb, k, :]
            before2 = seq_ref[...].reshape(sub, d_b)
            t8 = lax.broadcasted_iota(jnp.int32, (sub, 1), 0) % chunk_len
            back1 = jnp.where(t8 >= 1, pltpu.roll(xg, 1, axis=0), before1)
            back2 = jnp.where(t8 >= 2, pltpu.roll(xg, 2, axis=0), before2)
            seq_ref[...] = xg.reshape(nb, chunk_len, d_b)
            for k in range(ctx):
                cs_ref[b0:b0 + nb, k, :] = seq_ref[:, chunk_len - ctx + k, :]
        else:
            ext = jnp.concatenate([prev, xg], axis=0)
            back1 = pltpu.roll(ext, 1, axis=0)[_SUBLANES:]
            back2 = pltpu.roll(ext, 2, axis=0)[_SUBLANES:]
            prev = xg[sub - _SUBLANES:]
        conv = back2 * cw[0:1] + back1 * cw[1:2] + xg * cw[2:3]
        return (x, u, v.astype(_BF), (gate_b * conv).astype(_BF)), prev

    def mix_and_output(r0, x, u, vb, y_b):
        y_parts = []
        for hd in range(_NH_A):
            cols = slice(hd * _HD_A, (hd + 1) * _HD_A)
            chunks = []
            for c0 in range(0, sub, _CHUNK):
                mixed = _dot(mix_w[hd], vb[c0:c0 + _CHUNK, cols]) + mix_b[hd]
                chunks.append((u[c0:c0 + _CHUNK, cols] * mixed).astype(_BF))
            y_parts.append(jnp.concatenate(chunks, axis=0))
        y_parts.append(y_b)
        o_ref[r0:r0 + sub, :] = x + _dot(jnp.concatenate(y_parts, axis=1), wout_ref[...])

    prev = None if sample else carry_ref[...]
    staged = []
    for r0 in range(0, tm, sub):
        vals, prev = project(r0, prev)
        staged.append((r0, vals))
    for r0, vals in staged:
        mix_and_output(r0, *vals)
    if not sample:
        carry_ref[...] = prev
        cs_ref[0] = prev


def _even_body(x_ref, g_ref, win_ref, lng_ref, lnb_ref, sw_ref, sb_ref, cw_ref, wout_ref,
               st_ref, o_ref, cs_ref, css_ref, v_ref, carry_ref, seq_ref, *,
               n_prompt_tiles, tiles_per_seq, chunk_s, sub):
    i = pl.program_id(0)
    shared = (x_ref, g_ref, win_ref, lng_ref, lnb_ref, sw_ref, sb_ref, cw_ref, wout_ref, o_ref)

    @pl.when(i < n_prompt_tiles)
    def _():
        @pl.when((i % tiles_per_seq) == 0)
        def _():
            carry_ref[...] = jnp.zeros_like(carry_ref)

        _even_tile(*shared, chunk_len=_CHUNK, sub=sub, carry_ref=carry_ref, cs_ref=cs_ref)

    @pl.when(i >= n_prompt_tiles)
    def _():
        _even_tile(*shared, chunk_len=chunk_s, sub=sub, cs_ref=css_ref, st_ref=st_ref, seq_ref=seq_ref, v_ref=v_ref)


def _even_mixer(xc, g, w_in, ln_g, ln_b, sgu_w, sgu_b, conv_w, w_out, conv_state, *, n_prompt, seq_len, chunk_s):
    t, d = xc.shape
    n_sample = t - n_prompt
    d_in = w_in.shape[1]
    d_a = _NH_A * _HD_A
    d_b = conv_w.shape[1]
    npt = n_prompt // _TM
    n_seq = n_prompt // seq_len
    tiles_per_seq = seq_len // _TM
    row = lambda i: (i, 0)
    srow = lambda i: (jnp.maximum(i - npt, 0), 0)
    st_block = (_TM // chunk_s,) + conv_state.shape[1:]
    st_map = lambda i: (jnp.maximum(i - npt, 0), 0, 0)
    in_specs = [pl.BlockSpec((_TM, d), row), g[1], _resident((d, d_in)), _resident((1, d_a)),
                _resident((1, d_a)), _resident(sgu_w.shape), _resident(sgu_b.shape), _resident(conv_w.shape),
                _resident(w_out.shape), pl.BlockSpec(st_block, st_map, pipeline_mode=pl.Buffered(1))]
    args = [xc, g[0], w_in, ln_g.reshape(1, d_a), ln_b.reshape(1, d_a), sgu_w, sgu_b, conv_w, w_out, conv_state]
    out_specs = [pl.BlockSpec((_TM, d), row),
                 pl.BlockSpec((1, _SUBLANES, d_b), lambda i: (jnp.minimum(i // tiles_per_seq, n_seq - 1), 0, 0)),
                 pl.BlockSpec(st_block, st_map), pl.BlockSpec((_TM, d_a), srow)]
    out_shape = [jax.ShapeDtypeStruct((t, d), _F32), jax.ShapeDtypeStruct((n_seq, _SUBLANES, d_b), _F32),
                 jax.ShapeDtypeStruct(conv_state.shape, _F32), jax.ShapeDtypeStruct((n_sample, d_a), _F32)]
    scratch = [pltpu.VMEM((_SUBLANES, d_b), _F32), pltpu.VMEM((_SUB // chunk_s, chunk_s, d_b), _F32)]
    outs, _ = _call(
        functools.partial(_even_body, n_prompt_tiles=npt, tiles_per_seq=tiles_per_seq, chunk_s=chunk_s, sub=_SUB),
        name="even", grid=t // _TM, in_specs=in_specs, args=args, out_specs=out_specs, out_shape=out_shape,
        scratch=scratch)
    return outs


def _window_sums(ext):
    gc = ext.shape[1] // len(_POOL_WINDOWS)
    sums = []
    s = ext
    width = 1
    for w in _POOL_WINDOWS:
        while width < w:
            s = s + pltpu.roll(s, width, axis=0)
            width *= 2
        sums.append(s[:, 0:gc])
        s = s[:, gc:]
    return sums


def _odd_tile(x_ref, g_ref, win_ref, pw_ref, psc_ref, wout_ref, o_ref, ps_ref, *, sub, pos0=None,
              carry_ref=None, ctx_ref=None, ext_ref=None, seq_len=None):
    sample = carry_ref is None
    tm, d = x_ref.shape
    gc = d // len(_POOL_WINDOWS)
    def project(r0, prev):
        rows = slice(r0, r0 + sub)
        x = x_ref[rows, :]
        h = _rms(x, g_ref[...]).astype(_BF)
        p = _dot(h, win_ref[...])
        t_in_tile = r0 + lax.broadcasted_iota(jnp.int32, (sub, 1), 0)
        if sample:
            nb = sub // seq_len
            b0 = r0 // seq_len
            per_seq = _CTX_ROWS + seq_len
            lead = _CTX_ROWS - _POOL_CTX
            ext_ref[:, 0:lead, :] = jnp.zeros((nb, lead, d), _F32)
            for k in range(_POOL_CTX):
                ext_ref[:, lead + k, :] = ctx_ref[k, b0:b0 + nb, :]
            ext_ref[:, _CTX_ROWS:per_seq, :] = p.reshape(nb, seq_len, d)
            ext = ext_ref[...].reshape(nb * per_seq, d)
            pick = lambda s: s.reshape(nb, per_seq, s.shape[1])[:, _CTX_ROWS:, :].reshape(sub, s.shape[1])
            pos = _PAST_LEN + t_in_tile % seq_len
            for k in range(_POOL_CTX):
                ps_ref[k, b0:b0 + nb, :] = ext_ref[:, per_seq - _POOL_CTX + k, :]
        else:
            ext = jnp.concatenate([prev, p], axis=0)
            pick = lambda s: s[_CTX_ROWS:]
            pos = pos0 + t_in_tile
            prev = p[sub - _CTX_ROWS:]
        dlts = []
        for grp, (w, s) in enumerate(zip(_POOL_WINDOWS, _window_sums(ext))):
            cnt = jnp.minimum(pos + 1, w).astype(_F32)
            dlts.append((pick(s) / cnt - p[:, grp * gc:(grp + 1) * gc]).astype(_BF))
        return (x, dlts), prev

    def output(r0, x, dlts):
        ys = []
        for grp, dlt in enumerate(dlts):
            cols = slice(grp * gc, (grp + 1) * gc)
            ys.append((_dot(dlt, pw_ref[grp]) * psc_ref[:, cols]).astype(_BF))
        o_ref[r0:r0 + sub, :] = x + _dot(jnp.concatenate(ys, axis=1), wout_ref[...])

    prev = None if sample else carry_ref[...]
    staged = []
    for r0 in range(0, tm, sub):
        vals, prev = project(r0, prev)
        staged.append((r0, vals))
    for r0, vals in staged:
        output(r0, *vals)
    if not sample:
        carry_ref[...] = prev
        ps_ref[0] = prev


def _odd_body(x_ref, g_ref, win_ref, pw_ref, psc_ref, wout_ref, ctx_ref, o_ref, psp_ref, pss_ref,
              carry_ref, ext_ref, *, n_prompt_tiles, tiles_per_seq, seq_s):
    i = pl.program_id(0)
    shared = (x_ref, g_ref, win_ref, pw_ref, psc_ref, wout_ref, o_ref)

    @pl.when(i < n_prompt_tiles)
    def _():
        @pl.when((i % tiles_per_seq) == 0)
        def _():
            carry_ref[...] = jnp.zeros_like(carry_ref)

        _odd_tile(*shared, psp_ref, sub=_SUB // 2, pos0=(i % tiles_per_seq) * x_ref.shape[0], carry_ref=carry_ref)

    @pl.when(i >= n_prompt_tiles)
    def _():
        _odd_tile(*shared, pss_ref, sub=ext_ref.shape[0] * seq_s, ctx_ref=ctx_ref, ext_ref=ext_ref,
                  seq_len=seq_s)


def _odd_mixer(xc, g, w_in, pool_w, pool_scale, w_out, pool_state, *, n_prompt, seq_len, seq_s):
    t, d = xc.shape
    npt = n_prompt // _TM
    n_seq = n_prompt // seq_len
    tiles_per_seq = seq_len // _TM
    seqs_per_tile = _TM // seq_s
    row = lambda i: (i, 0)
    ctx_spec = lambda **kw: pl.BlockSpec((_POOL_CTX, seqs_per_tile, d), lambda i: (0, jnp.maximum(i - npt, 0), 0), **kw)
    in_specs = [pl.BlockSpec((_TM, d), row), g[1], _resident((d, d)), _resident(pool_w.shape),
                _resident((1, d)), _resident((d, d)), ctx_spec(pipeline_mode=pl.Buffered(1))]
    args = [xc, g[0], w_in, pool_w, pool_scale.reshape(1, d), w_out, pool_state]
    out_specs = [pl.BlockSpec((_TM, d), row),
                 pl.BlockSpec((1, _CTX_ROWS, d), lambda i: (jnp.minimum(i // tiles_per_seq, n_seq - 1), 0, 0)),
                 ctx_spec()]
    out_shape = [jax.ShapeDtypeStruct((t, d), _F32), jax.ShapeDtypeStruct((n_seq, _CTX_ROWS, d), _F32),
                 jax.ShapeDtypeStruct(pool_state.shape, _F32)]
    seqs_per_sub = _SUB // 2 // seq_s
    scratch = [pltpu.VMEM((_CTX_ROWS, d), _F32),
               pltpu.VMEM((seqs_per_sub, _CTX_ROWS + seq_s, d), _F32)]
    outs, _ = _call(
        functools.partial(_odd_body, n_prompt_tiles=npt, tiles_per_seq=tiles_per_seq, seq_s=seq_s),
        name="odd", grid=t // _TM, in_specs=in_specs, args=args, out_specs=out_specs, out_shape=out_shape,
        scratch=scratch)
    return outs


def kernel(x_prompt, x_sample, state_conv, state_pool, norm_g, final_norm_g, ffn_w_gate, ffn_w_up, ffn_w_down,
           e_w_in, e_ln_g, e_ln_b, e_sgu_w, e_sgu_b, e_conv_w, e_w_out, o_w_in, o_pool_w, o_pool_scale, o_w_out):
    batch, seq, d = x_prompt.shape
    dec_batch, dec_seq, _ = x_sample.shape
    n_prompt, n_sample = batch * seq, dec_batch * dec_seq
    d_a = _NH_A * _HD_A
    n_groups, gc = o_pool_w.shape[1:3]
    ffn = functools.partial(_ffn, n_prompt=n_prompt, n_sample=n_sample)
    ffn_casts = lambda layer, k: [_Cast(w, (layer, k)) for w in (ffn_w_gate, ffn_w_up, ffn_w_down)]
    gain = functools.partial(_gain, norm_g)

    even_casts = [_Cast(e_w_in, (0,)), _Cast(e_w_out, (0,))]
    odd_casts = [_Cast(o_w_in, (0,)), _Cast(o_w_out, (0,)), _Cast(o_pool_w.reshape(-1, n_groups * gc, gc), (0,))]
    (xc,), (e_win, e_wout, *w01) = ffn(
        (x_prompt.reshape(n_prompt, d), x_sample.reshape(n_sample, d)), gain(0, 0),
        ffn_w_gate, ffn_w_up, ffn_w_down[0, 0].astype(_BF), f32_lead=(0, 0), casts=even_casts + ffn_casts(0, 1))
    xc, conv_p, conv_s, v_s = _even_mixer(xc, gain(0, 1), e_win, e_ln_g[0], e_ln_b[0], e_sgu_w[0], e_sgu_b[0],
                                         e_conv_w[0], e_wout, state_conv[0], n_prompt=n_prompt, seq_len=seq,
                                         chunk_s=dec_seq)
    (xc,), (o_win, o_wout, o_pw, *w10) = ffn((xc,), gain(0, 2), *w01, casts=odd_casts + ffn_casts(1, 0))

    (xc,), w11 = ffn((xc,), gain(1, 0), *w10, casts=ffn_casts(1, 1))
    xc, pool_p, pool_s = _odd_mixer(xc, gain(1, 1), o_win, o_pw.reshape(n_groups, gc, gc), o_pool_scale[0], o_wout,
                                    jnp.transpose(state_pool[0], (1, 0, 2)), n_prompt=n_prompt, seq_len=seq,
                                    seq_s=dec_seq)
    (y_p, y_s), _ = ffn((xc,), gain(1, 2), *w11, final_g=final_norm_g)

    keep = _CONV_W - 1
    conv_prompt = conv_p[:, _SUBLANES - keep:][None]
    conv_sample = conv_s[None]
    chunk_v_sample = v_s.reshape(1, dec_batch, dec_seq, d_a)
    pool_prompt = pool_p[:, pool_p.shape[1] - _POOL_CTX:][None]
    pool_sample = jnp.transpose(pool_s, (1, 0, 2))[None]
    return (y_p.reshape(batch, seq, d), y_s.reshape(dec_batch, dec_seq, d), conv_prompt, conv_sample,
            chunk_v_sample, pool_prompt, pool_sample)
```

```python
import collections
import functools

import jax
import jax.numpy as jnp
from jax import lax
from jax.experimental import pallas as pl
from jax.experimental.pallas import tpu as pltpu

_CHUNK = 128
_NH_A = 4
_HD_A = 128
_CONV_W = 3
_POOL_WINDOWS = (2, 4, 8, 16)
_POOL_CTX = max(_POOL_WINDOWS) - 1
_PAST_LEN = 16384
_EPS = 1e-6

_SUBLANES = 8
_BF16_ROWS = 16
_VMEM_LIMIT = 56 * 1024 * 1024

_TM = 1024
_SUB = 512
_MXU_COLS = 256
_CAST_CHUNKS = 32
_CAST_SLOTS = 8
_CTX_ROWS = 2 * _SUBLANES

_BF = jnp.bfloat16
_F32 = jnp.float32


def _rms(x, g):
    return x * lax.rsqrt(jnp.mean(x * x, axis=-1, keepdims=True) + _EPS) * g


def _dot(a, b):
    return jnp.dot(a, b, preferred_element_type=_F32)


def _resident(shape):
    nd = len(shape)
    return pl.BlockSpec(shape, lambda i: (0,) * nd, pipeline_mode=pl.Buffered(1))


def _gain(norm_g, layer, j):
    n, m, d = norm_g.shape
    k = layer * m + j
    spec = pl.BlockSpec((None, 1, d), lambda i: (k, 0, 0), pipeline_mode=pl.Buffered(1))
    return norm_g.reshape(n * m, 1, d), spec


_Cast = collections.namedtuple("_Cast", "array lead")


def _cast_specs(cast, grid):
    lead = tuple(cast.lead)
    rows, cols = cast.array.shape[len(lead):]
    steps = max(s for s in (1, 2, 4, 8, 16, 32) if s <= grid and rows % (s * _BF16_ROWS) == 0)
    blk = rows // steps
    step = lambda i: jnp.minimum(i, steps - 1)
    in_spec = pl.BlockSpec((None,) * len(lead) + (blk, cols), lambda i: lead + (step(i), 0))
    out_spec = pl.BlockSpec((blk, cols), lambda i: (step(i), 0))
    return in_spec, out_spec, jax.ShapeDtypeStruct((rows, cols), _BF)


def _call(body, *, name, grid, in_specs, args, out_specs, out_shape, scratch=(), casts=()):
    n_in, n_out, n_cast = len(in_specs), len(out_specs), len(casts)
    specs = [_cast_specs(c, grid) for c in casts]

    def wrapped(*refs):
        ins, refs = refs[:n_in], refs[n_in:]
        cast_in, refs = refs[:n_cast], refs[n_cast:]
        outs, refs = refs[:n_out], refs[n_out:]
        cast_out, scratch_refs = refs[:n_cast], refs[n_cast:]
        for src, dst in zip(cast_in, cast_out):
            dst[...] = src[...].astype(_BF)
        body(*ins, *outs, *scratch_refs)

    res = pl.pallas_call(
        wrapped,
        grid=(grid,),
        in_specs=list(in_specs) + [s[0] for s in specs],
        out_specs=list(out_specs) + [s[1] for s in specs],
        out_shape=list(out_shape) + [s[2] for s in specs],
        scratch_shapes=list(scratch),
        compiler_params=pltpu.CompilerParams(dimension_semantics=("arbitrary",), vmem_limit_bytes=_VMEM_LIMIT),
        name=name,
    )(*args, *[c.array for c in casts])
    return res[:n_out], res[n_out:]


def _hidden_chunks(dff):
    first = -(-(dff // _MXU_COLS) // 2) * _MXU_COLS
    return (0, first), (first, dff - first)


def _load_cast(srcs, dsts, stage, sem):
    slots, rows = stage.shape[:2]
    chunks = [(src, dst, c) for src, dst in zip(srcs, dsts) for c in range(dst.shape[0] // rows)]
    copy = lambda n: pltpu.make_async_copy(
        chunks[n][0].at[pl.ds(chunks[n][2] * rows, rows), :], stage.at[n % slots], sem.at[n % slots])
    for n in range(min(slots - 1, len(chunks))):
        copy(n).start()
    for n, (_, dst, c) in enumerate(chunks):
        if n + slots - 1 < len(chunks):
            copy(n + slots - 1).start()
        copy(n).wait()
        dst[c * rows:(c + 1) * rows, :] = stage[n % slots].astype(_BF)


def _ffn_body(*refs, split_in, final, n_prompt_tiles, f32_lead):
    refs = list(refs)
    i = pl.program_id(0)
    x_refs = [refs.pop(0) for _ in range(2 if split_in else 1)]
    g_ref, wg_ref, wu_ref, wd_ref = refs[:4]
    refs = refs[4:]
    if f32_lead is not None:
        *refs, wg_bf, wu_bf, stage, sem = refs

        @pl.when(i == 0)
        def _():
            _load_cast([wg_ref.at[f32_lead], wu_ref.at[f32_lead]], [wg_bf, wu_bf], stage, sem)

        wg_ref, wu_ref = wg_bf, wu_bf
    is_prompt = i < n_prompt_tiles
    for r0 in range(0, _TM, _SUB):
        rows = slice(r0, r0 + _SUB)
        if split_in:
            x = jnp.where(is_prompt, x_refs[0][rows, :], x_refs[1][rows, :])
        else:
            x = x_refs[0][rows, :]
        h = _rms(x, g_ref[...]).astype(_BF)
        acc = None
        for c0, cn in _hidden_chunks(wd_ref.shape[0]):
            a = _dot(h, wg_ref[:, c0:c0 + cn])
            b = _dot(h, wu_ref[:, c0:c0 + cn])
            act = (a * jax.nn.sigmoid(a) * b).astype(_BF)
            d = _dot(act, wd_ref[c0:c0 + cn, :])
            acc = d if acc is None else acc + d
        y = x + 0.5 * acc
        if final:
            fg_ref, yp_ref, ys_ref = refs
            y = _rms(y, fg_ref[...])
            to_prompt = jnp.broadcast_to(is_prompt, y.shape)
            pltpu.store(yp_ref.at[rows, :], y, mask=to_prompt)
            pltpu.store(ys_ref.at[rows, :], y, mask=jnp.logical_not(to_prompt))
        else:
            (o_ref,) = refs
            o_ref[rows, :] = y


def _ffn(xs, g, wg, wu, wd, *, n_prompt, n_sample, final_g=None, casts=(), f32_lead=None):
    dff, d = wd.shape
    npt, nst = n_prompt // _TM, n_sample // _TM
    split_in = len(xs) == 2
    final = final_g is not None
    row = lambda i: (i, 0)
    prow = lambda i: (jnp.minimum(i, npt - 1), 0)
    srow = lambda i: (jnp.maximum(i - npt, 0), 0)
    tile = (_TM, d)
    if split_in:
        in_specs = [pl.BlockSpec(tile, prow), pl.BlockSpec(tile, srow, pipeline_mode=pl.Buffered(1))]
    else:
        in_specs = [pl.BlockSpec(tile, row)]
    scratch = []
    if f32_lead is None:
        in_specs += [g[1], _resident((d, dff)), _resident((d, dff)), _resident((dff, d))]
    else:
        in_specs += [g[1], pl.BlockSpec(memory_space=pl.ANY), pl.BlockSpec(memory_space=pl.ANY), _resident((dff, d))]
        scratch = [pltpu.VMEM((d, dff), _BF), pltpu.VMEM((d, dff), _BF),
                   pltpu.VMEM((_CAST_SLOTS, d // _CAST_CHUNKS, dff), _F32), pltpu.SemaphoreType.DMA((_CAST_SLOTS,))]
    args = list(xs) + [g[0], wg, wu, wd]
    if final:
        in_specs.append(_resident((1, d)))
        args.append(final_g.reshape(1, d))
        out_specs = [pl.BlockSpec(tile, prow), pl.BlockSpec(tile, srow)]
        out_shape = [jax.ShapeDtypeStruct((n_prompt, d), _F32), jax.ShapeDtypeStruct((n_sample, d), _F32)]
    else:
        out_specs = [pl.BlockSpec(tile, row)]
        out_shape = [jax.ShapeDtypeStruct((n_prompt + n_sample, d), _F32)]
    return _call(
        functools.partial(_ffn_body, split_in=split_in, final=final, n_prompt_tiles=npt, f32_lead=f32_lead),
        name="ffn_final" if final else ("ffn_first" if split_in else "ffn"),
        grid=npt + nst, in_specs=in_specs, args=args, out_specs=out_specs, out_shape=out_shape, scratch=scratch,
        casts=casts)


def _gelu(x):
    return 0.5 * x * (1.0 + lax.erf(x * (2.0 ** -0.5)))


def _even_tile(x_ref, g_ref, win_ref, lng_ref, lnb_ref, sw_ref, sb_ref, cw_ref, wout_ref, o_ref, *,
               chunk_len, sub, carry_ref=None, cs_ref=None, st_ref=None, seq_ref=None, v_ref=None):
    sample = carry_ref is None
    tm = x_ref.shape[0]
    d_a = _NH_A * _HD_A
    d_b = cw_ref.shape[1]
    o = 2 * d_a
    tt = lax.broadcasted_iota(jnp.int32, (_CHUNK, _CHUNK), 0)
    ss = lax.broadcasted_iota(jnp.int32, (_CHUNK, _CHUNK), 1)
    keep = (ss <= tt) & ((tt // chunk_len) == (ss // chunk_len))
    mix_w, mix_b = [], []
    for hd in range(_NH_A):
        corner = jnp.where(ss[:chunk_len] < chunk_len, sw_ref[hd, 0:chunk_len, :], 0.0)
        width = chunk_len
        while width < _CHUNK:
            corner = corner + pltpu.roll(corner, width, axis=1)
            width *= 2
        tiled = jnp.concatenate([corner] * (_CHUNK // chunk_len), axis=0)
        mix_w.append(jnp.where(keep, tiled, 0.0).astype(_BF))
        picked = jnp.where(ss == tt % chunk_len, sb_ref[hd:hd + 1, :], 0.0)
        mix_b.append(jnp.sum(picked, axis=1, keepdims=True))
    cw = cw_ref[...]

    def project(r0, prev):
        rows = slice(r0, r0 + sub)
        x = x_ref[rows, :]
        h = _rms(x, g_ref[...]).astype(_BF)
        z = _dot(h, win_ref[...])
        u = _gelu(z[:, 0:d_a])
        vv = _gelu(z[:, d_a:2 * d_a])
        mu = jnp.mean(vv, axis=-1, keepdims=True)
        vc = vv - mu
        var = jnp.mean(vc * vc, axis=-1, keepdims=True)
        v = vc * lax.rsqrt(var + _EPS) * lng_ref[...] + lnb_ref[...]
        if sample:
            v_ref[rows, :] = v
        gate_b = z[:, o:o + d_b]
        gate_c = z[:, o + d_b:o + 2 * d_b]
        x_in = z[:, o + 2 * d_b:o + 3 * d_b]
        xg = gate_c * x_in
        if sample:
            nb, b0, ctx = sub // chunk_len, r0 // chunk_len, _CONV_W - 1
            seq_ref[...] = jnp.zeros(seq_ref.shape, _F32)
            seq_ref[:, 0, :] = st_ref[b0:b0 + nb, ctx - 1, :]
            before1 = seq_ref[...].reshape(sub, d_b)
            for k in range(ctx):
                seq_ref[:, k, :] = st_ref[b0:b0 + nb, k, :]
            before2 = seq_ref[...].reshape(sub, d_b)
            t8 = lax.broadcasted_iota(jnp.int32, (sub, 1), 0) % chunk_len
            back1 = jnp.where(t8 >= 1, pltpu.roll(xg, 1, axis=0), before1)
            back2 = jnp.where(t8 >= 2, pltpu.roll(xg, 2, axis=0), before2)
            seq_ref[...] = xg.reshape(nb, chunk_len, d_b)
            for k in range(ctx):
                cs_ref[b0:b0 + nb, k, :] = seq_ref[:, chunk_len - ctx + k, :]
        else:
            ext = jnp.concatenate([prev, xg], axis=0)
            back1 = pltpu.roll(ext, 1, axis=0)[_SUBLANES:]
            back2 = pltpu.roll(ext, 2, axis=0)[_SUBLANES:]
            prev = xg[sub - _SUBLANES:]
        conv = back2 * cw[0:1] + back1 * cw[1:2] + xg * cw[2:3]
        return (x, u, v.astype(_BF), (gate_b * conv).astype(_BF)), prev

    def mix_and_output(r0, x, u, vb, y_b):
        y_parts = []
        for hd in range(_NH_A):
            cols = slice(hd * _HD_A, (hd + 1) * _HD_A)
            chunks = []
            for c0 in range(0, sub, _CHUNK):
                mixed = _dot(mix_w[hd], vb[c0:c0 + _CHUNK, cols]) + mix_b[hd]
                chunks.append((u[c0:c0 + _CHUNK, cols] * mixed).astype(_BF))
            y_parts.append(jnp.concatenate(chunks, axis=0))
        y_parts.append(y_b)
        o_ref[r0:r0 + sub, :] = x + _dot(jnp.concatenate(y_parts, axis=1), wout_ref[...])

    prev = None if sample else carry_ref[...]
    staged = []
    for r0 in range(0, tm, sub):
        vals, prev = project(r0, prev)
        staged.append((r0, vals))
    for r0, vals in staged:
        mix_and_output(r0, *vals)
    if not sample:
        carry_ref[...] = prev
        cs_ref[0] = prev


def _even_body(x_ref, g_ref, win_ref, lng_ref, lnb_ref, sw_ref, sb_ref, cw_ref, wout_ref,
               st_ref, o_ref, cs_ref, css_ref, v_ref, carry_ref, seq_ref, *,
               n_prompt_tiles, tiles_per_seq, chunk_s, sub):
    i = pl.program_id(0)
    shared = (x_ref, g_ref, win_ref, lng_ref, lnb_ref, sw_ref, sb_ref, cw_ref, wout_ref, o_ref)

    @pl.when(i < n_prompt_tiles)
    def _():
        @pl.when((i % tiles_per_seq) == 0)
        def _():
            carry_ref[...] = jnp.zeros_like(carry_ref)

        _even_tile(*shared, chunk_len=_CHUNK, sub=sub, carry_ref=carry_ref, cs_ref=cs_ref)

    @pl.when(i >= n_prompt_tiles)
    def _():
        _even_tile(*shared, chunk_len=chunk_s, sub=sub, cs_ref=css_ref, st_ref=st_ref, seq_ref=seq_ref, v_ref=v_ref)


def _even_mixer(xc, g, w_in, ln_g, ln_b, sgu_w, sgu_b, conv_w, w_out, conv_state, *, n_prompt, seq_len, chunk_s):
    t, d = xc.shape
    n_sample = t - n_prompt
    d_in = w_in.shape[1]
    d_a = _NH_A * _HD_A
    d_b = conv_w.shape[1]
    npt = n_prompt // _TM
    n_seq = n_prompt // seq_len
    tiles_per_seq = seq_len // _TM
    row = lambda i: (i, 0)
    srow = lambda i: (jnp.maximum(i - npt, 0), 0)
    st_block = (_TM // chunk_s,) + conv_state.shape[1:]
    st_map = lambda i: (jnp.maximum(i - npt, 0), 0, 0)
    in_specs = [pl.BlockSpec((_TM, d), row), g[1], _resident((d, d_in)), _resident((1, d_a)),
                _resident((1, d_a)), _resident(sgu_w.shape), _resident(sgu_b.shape), _resident(conv_w.shape),
                _resident(w_out.shape), pl.BlockSpec(st_block, st_map, pipeline_mode=pl.Buffered(1))]
    args = [xc, g[0], w_in, ln_g.reshape(1, d_a), ln_b.reshape(1, d_a), sgu_w, sgu_b, conv_w, w_out, conv_state]
    out_specs = [pl.BlockSpec((_TM, d), row),
                 pl.BlockSpec((1, _SUBLANES, d_b), lambda i: (jnp.minimum(i // tiles_per_seq, n_seq - 1), 0, 0)),
                 pl.BlockSpec(st_block, st_map), pl.BlockSpec((_TM, d_a), srow)]
    out_shape = [jax.ShapeDtypeStruct((t, d), _F32), jax.ShapeDtypeStruct((n_seq, _SUBLANES, d_b), _F32),
                 jax.ShapeDtypeStruct(conv_state.shape, _F32), jax.ShapeDtypeStruct((n_sample, d_a), _F32)]
    scratch = [pltpu.VMEM((_SUBLANES, d_b), _F32), pltpu.VMEM((_SUB // chunk_s, chunk_s, d_b), _F32)]
    outs, _ = _call(
        functools.partial(_even_body, n_prompt_tiles=npt, tiles_per_seq=tiles_per_seq, chunk_s=chunk_s, sub=_SUB),
        name="even", grid=t // _TM, in_specs=in_specs, args=args, out_specs=out_specs, out_shape=out_shape,
        scratch=scratch)
    return outs


def _window_sums(ext):
    gc = ext.shape[1] // len(_POOL_WINDOWS)
    sums = []
    s = ext
    width = 1
    for w in _POOL_WINDOWS:
        while width < w:
            s = s + pltpu.roll(s, width, axis=0)
            width *= 2
        sums.append(s[:, 0:gc])
        s = s[:, gc:]
    return sums


def _odd_tile(x_ref, g_ref, win_ref, pw_ref, psc_ref, wout_ref, o_ref, ps_ref, *, sub, pos0=None,
              carry_ref=None, ctx_ref=None, ext_ref=None, seq_len=None):
    sample = carry_ref is None
    tm, d = x_ref.shape
    gc = d // len(_POOL_WINDOWS)
    def project(r0, prev):
        rows = slice(r0, r0 + sub)
        x = x_ref[rows, :]
        h = _rms(x, g_ref[...]).astype(_BF)
        p = _dot(h, win_ref[...])
        t_in_tile = r0 + lax.broadcasted_iota(jnp.int32, (sub, 1), 0)
        if sample:
            nb = sub // seq_len
            b0 = r0 // seq_len
            per_seq = _CTX_ROWS + seq_len
            lead = _CTX_ROWS - _POOL_CTX
            ext_ref[:, 0:lead, :] = jnp.zeros((nb, lead, d), _F32)
            for k in range(_POOL_CTX):
                ext_ref[:, lead + k, :] = ctx_ref[k, b0:b0 + nb, :]
            ext_ref[:, _CTX_ROWS:per_seq, :] = p.reshape(nb, seq_len, d)
            ext = ext_ref[...].reshape(nb * per_seq, d)
            pick = lambda s: s.reshape(nb, per_seq, s.shape[1])[:, _CTX_ROWS:, :].reshape(sub, s.shape[1])
            pos = _PAST_LEN + t_in_tile % seq_len
            for k in range(_POOL_CTX):
                ps_ref[k, b0:b0 + nb, :] = ext_ref[:, per_seq - _POOL_CTX + k, :]
        else:
            ext = jnp.concatenate([prev, p], axis=0)
            pick = lambda s: s[_CTX_ROWS:]
            pos = pos0 + t_in_tile
            prev = p[sub - _CTX_ROWS:]
        dlts = []
        for grp, (w, s) in enumerate(zip(_POOL_WINDOWS, _window_sums(ext))):
            cnt = jnp.minimum(pos + 1, w).astype(_F32)
            dlts.append((pick(s) / cnt - p[:, grp * gc:(grp + 1) * gc]).astype(_BF))
        return (x, dlts), prev

    def output(r0, x, dlts):
        ys = []
        for grp, dlt in enumerate(dlts):
            cols = slice(grp * gc, (grp + 1) * gc)
            ys.append((_dot(dlt, pw_ref[grp]) * psc_ref[:, cols]).astype(_BF))
        o_ref[r0:r0 + sub, :] = x + _dot(jnp.concatenate(ys, axis=1), wout_ref[...])

    prev = None if sample else carry_ref[...]
    staged = []
    for r0 in range(0, tm, sub):
        vals, prev = project(r0, prev)
        staged.append((r0, vals))
    for r0, vals in staged:
        output(r0, *vals)
    if not sample:
        carry_ref[...] = prev
        ps_ref[0] = prev


def _odd_body(x_ref, g_ref, win_ref, pw_ref, psc_ref, wout_ref, ctx_ref, o_ref, psp_ref, pss_ref,
              carry_ref, ext_ref, *, n_prompt_tiles, tiles_per_seq, seq_s):
    i = pl.program_id(0)
    shared = (x_ref, g_ref, win_ref, pw_ref, psc_ref, wout_ref, o_ref)

    @pl.when(i < n_prompt_tiles)
    def _():
        @pl.when((i % tiles_per_seq) == 0)
        def _():
            carry_ref[...] = jnp.zeros_like(carry_ref)

        _odd_tile(*shared, psp_ref, sub=_SUB // 2, pos0=(i % tiles_per_seq) * x_ref.shape[0], carry_ref=carry_ref)

    @pl.when(i >= n_prompt_tiles)
    def _():
        _odd_tile(*shared, pss_ref, sub=ext_ref.shape[0] * seq_s, ctx_ref=ctx_ref, ext_ref=ext_ref,
                  seq_len=seq_s)


def _odd_mixer(xc, g, w_in, pool_w, pool_scale, w_out, pool_state, *, n_prompt, seq_len, seq_s):
    t, d = xc.shape
    npt = n_prompt // _TM
    n_seq = n_prompt // seq_len
    tiles_per_seq = seq_len // _TM
    seqs_per_tile = _TM // seq_s
    row = lambda i: (i, 0)
    ctx_spec = lambda **kw: pl.BlockSpec((_POOL_CTX, seqs_per_tile, d), lambda i: (0, jnp.maximum(i - npt, 0), 0), **kw)
    in_specs = [pl.BlockSpec((_TM, d), row), g[1], _resident((d, d)), _resident(pool_w.shape),
                _resident((1, d)), _resident((d, d)), ctx_spec(pipeline_mode=pl.Buffered(1))]
    args = [xc, g[0], w_in, pool_w, pool_scale.reshape(1, d), w_out, pool_state]
    out_specs = [pl.BlockSpec((_TM, d), row),
                 pl.BlockSpec((1, _CTX_ROWS, d), lambda i: (jnp.minimum(i // tiles_per_seq, n_seq - 1), 0, 0)),
                 ctx_spec()]
    out_shape = [jax.ShapeDtypeStruct((t, d), _F32), jax.ShapeDtypeStruct((n_seq, _CTX_ROWS, d), _F32),
                 jax.ShapeDtypeStruct(pool_state.shape, _F32)]
    seqs_per_sub = _SUB // 2 // seq_s
    scratch = [pltpu.VMEM((_CTX_ROWS, d), _F32),
               pltpu.VMEM((seqs_per_sub, _CTX_ROWS + seq_s, d), _F32)]
    outs, _ = _call(
        functools.partial(_odd_body, n_prompt_tiles=npt, tiles_per_seq=tiles_per_seq, seq_s=seq_s),
        name="odd", grid=t // _TM, in_specs=in_specs, args=args, out_specs=out_specs, out_shape=out_shape,
        scratch=scratch)
    return outs


def kernel(x_prompt, x_sample, state_conv, state_pool, norm_g, final_norm_g, ffn_w_gate, ffn_w_up, ffn_w_down,
           e_w_in, e_ln_g, e_ln_b, e_sgu_w, e_sgu_b, e_conv_w, e_w_out, o_w_in, o_pool_w, o_pool_scale, o_w_out):
    batch, seq, d = x_prompt.shape
    dec_batch, dec_seq, _ = x_sample.shape
    n_prompt, n_sample = batch * seq, dec_batch * dec_seq
    d_a = _NH_A * _HD_A
    n_groups, gc = o_pool_w.shape[1:3]
    assert seq % _TM == 0 and n_sample % _TM == 0 and _TM % dec_seq == 0
    assert _CHUNK % dec_seq == 0 and dec_seq % _SUBLANES == 0 and dec_seq >= _CONV_W - 1
    assert norm_g.shape[0] == 2 and n_groups == len(_POOL_WINDOWS) and ffn_w_down.shape[2] % _MXU_COLS == 0
    ffn = functools.partial(_ffn, n_prompt=n_prompt, n_sample=n_sample)
    ffn_casts = lambda layer, k: [_Cast(w, (layer, k)) for w in (ffn_w_gate, ffn_w_up, ffn_w_down)]
    gain = functools.partial(_gain, norm_g)

    even_casts = [_Cast(e_w_in, (0,)), _Cast(e_w_out, (0,))]
    odd_casts = [_Cast(o_w_in, (0,)), _Cast(o_w_out, (0,)), _Cast(o_pool_w.reshape(-1, n_groups * gc, gc), (0,))]
    (xc,), (e_win, e_wout, *w01) = ffn(
        (x_prompt.reshape(n_prompt, d), x_sample.reshape(n_sample, d)), gain(0, 0),
        ffn_w_gate, ffn_w_up, ffn_w_down[0, 0].astype(_BF), f32_lead=(0, 0), casts=even_casts + ffn_casts(0, 1))
    xc, conv_p, conv_s, v_s = _even_mixer(xc, gain(0, 1), e_win, e_ln_g[0], e_ln_b[0], e_sgu_w[0], e_sgu_b[0],
                                         e_conv_w[0], e_wout, state_conv[0], n_prompt=n_prompt, seq_len=seq,
                                         chunk_s=dec_seq)
    (xc,), (o_win, o_wout, o_pw, *w10) = ffn((xc,), gain(0, 2), *w01, casts=odd_casts + ffn_casts(1, 0))

    (xc,), w11 = ffn((xc,), gain(1, 0), *w10, casts=ffn_casts(1, 1))
    xc, pool_p, pool_s = _odd_mixer(xc, gain(1, 1), o_win, o_pw.reshape(n_groups, gc, gc), o_pool_scale[0], o_wout,
                                    jnp.transpose(state_pool[0], (1, 0, 2)), n_prompt=n_prompt, seq_len=seq,
                                    seq_s=dec_seq)
    (y_p, y_s), _ = ffn((xc,), gain(1, 2), *w11, final_g=final_norm_g)

    keep = _CONV_W - 1
    conv_prompt = conv_p[:, _SUBLANES - keep:][None]
    conv_sample = conv_s[None]
    chunk_v_sample = v_s.reshape(1, dec_batch, dec_seq, d_a)
    pool_prompt = pool_p[:, pool_p.shape[1] - _POOL_CTX:][None]
    pool_sample = jnp.transpose(pool_s, (1, 0, 2))[None]
    return (y_p.reshape(batch, seq, d), y_s.reshape(dec_batch, dec_seq, d), conv_prompt, conv_sample,
            chunk_v_sample, pool_prompt, pool_sample)
```
